```python
import jax, jax.numpy as jnp
from jax import lax
import numpy as np

D_MODEL = 1024
BATCH = 8
SEQ = 4096
DEPTH = 1

GRID_W = 64
CTX_LEN = 256
EPS = 1e-6

LRU_WIDTH = 1280
LRU_BLOCKS = 8
LRU_BLOCK_DIM = LRU_WIDTH // LRU_BLOCKS
LRU_C = 8.0
CONV_WIDTH = 4
CONV_LEFT = 2

HEAD_DIM = 128
N_HEADS = D_MODEL // HEAD_DIM
N_KV_HEADS = 2
GROUP = N_HEADS // N_KV_HEADS
ATTN_WIDTH = N_HEADS * HEAD_DIM
KV_WIDTH = N_KV_HEADS * HEAD_DIM
ROPE_AXIS_DIM = HEAD_DIM // 2
ROPE_THETA = 10000.0
Q_BLOCK = 128

FFN_HIDDEN = ((8 * D_MODEL + 3 * 256 - 1) // (3 * 256)) * 256

IN_WIDTHS = (LRU_WIDTH, LRU_WIDTH, ATTN_WIDTH, KV_WIDTH, KV_WIDTH, D_MODEL, D_MODEL)
IN_COLS = 2 * LRU_WIDTH + ATTN_WIDTH + 2 * KV_WIDTH + 2 * D_MODEL

kernel_name = 'hybrid_rglru_gqa_dit_block'


def rmsnorm(x, g):
    xf = x.astype(jnp.float32)
    y = xf * lax.rsqrt(jnp.mean(xf * xf, axis=-1, keepdims=True) + EPS)
    return (y * g.astype(jnp.float32)).astype(x.dtype)


def modulate(h, shift, scale):
    return h * (1.0 + scale) + shift


def split_in(p):
    idx = []
    acc = 0
    for w in IN_WIDTHS[:-1]:
        acc += w
        idx.append(acc)
    return jnp.split(p, idx, axis=-1)


def axial_rope_tables(n_tokens):
    rows = n_tokens // GRID_W
    row = jnp.repeat(jnp.arange(rows, dtype=jnp.float32), GRID_W)
    col = jnp.tile(jnp.arange(GRID_W, dtype=jnp.float32), rows)
    inv_freq = ROPE_THETA ** (-jnp.arange(0, ROPE_AXIS_DIM, 2, dtype=jnp.float32) / ROPE_AXIS_DIM)
    ang_r = row[:, None] * inv_freq[None, :]
    ang_c = col[:, None] * inv_freq[None, :]
    expand = lambda t: t[None, :, None, :]
    return (expand(jnp.cos(ang_r)), expand(jnp.sin(ang_r)), expand(jnp.cos(ang_c)), expand(jnp.sin(ang_c)))


def rotate_axis(x, cos, sin):
    half = x.shape[-1] // 2
    x1, x2 = x[..., :half], x[..., half:]
    return jnp.concatenate([x1 * cos - x2 * sin, x2 * cos + x1 * sin], axis=-1)


def apply_axial_rope(x, tables):
    cr, sr, cc, sc = tables
    xf = x.astype(jnp.float32)
    out = jnp.concatenate([rotate_axis(xf[..., :ROPE_AXIS_DIM], cr, sr),
                           rotate_axis(xf[..., ROPE_AXIS_DIM:], cc, sc)], axis=-1)
    return out.astype(x.dtype)


def heads_prep(t, n_heads, gain, tables):
    b, l, _ = t.shape
    t = rmsnorm(t.reshape(b, l, n_heads, HEAD_DIM), gain)
    if tables is not None:
        t = apply_axial_rope(t, tables)
    return t


def block_attention(q, k, v):
    b, l, _, d = q.shape
    nb = l // Q_BLOCK
    qb = q.reshape(b, nb, Q_BLOCK, N_KV_HEADS, GROUP, d).transpose(1, 0, 3, 4, 2, 5)
    kt = k.transpose(0, 2, 1, 3).astype(jnp.float32)
    vt = v.transpose(0, 2, 1, 3).astype(jnp.float32)
    scale = 1.0 / np.sqrt(HEAD_DIM).astype(np.float32)

    def one_block(qi):
        s = jnp.einsum('bkgqd,bksd->bkgqs', qi.astype(jnp.float32), kt) * scale
        p = jax.nn.softmax(s, axis=-1)
        return jnp.einsum('bkgqs,bksd->bkgqd', p, vt).astype(v.dtype)

    o = lax.map(one_block, qb)
    return o.transpose(1, 0, 4, 2, 3, 5).reshape(b, l, N_HEADS * d)


def dw_conv(u, w, bias):
    l = u.shape[1]
    up = jnp.pad(u, ((0, 0), (CONV_LEFT, CONV_WIDTH - 1 - CONV_LEFT), (0, 0)))
    out = bias
    for j in range(CONV_WIDTH):
        out = out + up[:, j:j + l] * w[j]
    return out


def rglru_coeffs(xc, wa, ba, wx, bx, lam):
    b, l, _ = xc.shape
    xf = xc.astype(jnp.float32)
    xb = xf.reshape(b, l, LRU_BLOCKS, LRU_BLOCK_DIM)
    r = jax.nn.sigmoid(jnp.einsum('blnd,nde->blne', xb, wa.astype(jnp.float32)).reshape(b, l, LRU_WIDTH) + ba)
    i = jax.nn.sigmoid(jnp.einsum('blnd,nde->blne', xb, wx.astype(jnp.float32)).reshape(b, l, LRU_WIDTH) + bx)
    log_a = -LRU_C * r * jax.nn.softplus(-lam.astype(jnp.float32))
    a = jnp.exp(log_a)
    coef = jnp.sqrt(-jnp.expm1(2.0 * log_a))
    return a, coef * (i * xf)


def scan_combine(left, right):
    a_l, b_l = left
    a_r, b_r = right
    return a_r * a_l, a_r * b_l + b_r


def linear_scan(a, b, h0, reverse):
    edge = -1 if reverse else 0
    b = b.at[:, edge].add(a[:, edge] * h0)
    _, h = lax.associative_scan(scan_combine, (a, b), reverse=reverse, axis=1)
    return h


def bidirectional_rglru(xc_lat, xc_ctx, wa, ba, wx, bx, lam):
    bsz = xc_lat.shape[0]
    h_lat = jnp.zeros(xc_lat.shape, jnp.float32)
    h_ctx = jnp.zeros(xc_ctx.shape, jnp.float32)
    for d, reverse in enumerate((False, True)):
        a_c, b_c = rglru_coeffs(xc_ctx, wa[d], ba[d], wx[d], bx[d], lam[d])
        hc = linear_scan(a_c, b_c, jnp.zeros((bsz, LRU_WIDTH), jnp.float32), reverse)
        h_final = hc[:, 0] if reverse else hc[:, -1]
        a_l, b_l = rglru_coeffs(xc_lat, wa[d], ba[d], wx[d], bx[d], lam[d])
        hl = linear_scan(a_l, b_l, h_final, reverse)
        h_lat = h_lat + hl
        h_ctx = h_ctx + hc
    return h_lat.astype(xc_lat.dtype), h_ctx.astype(xc_ctx.dtype)


def hybrid_mixer(h, hc, tables, with_ctx_out, w_in, conv_w, conv_b, lru_wa, lru_ba, lru_wx, lru_bx,
                 lru_lambda, q_norm, k_norm, w_out_lru, w_out_attn, w_out):
    u, g, q, k, v, ga, gb = split_in(h @ w_in)
    uc, gc, qc, kc, vc, gac, gbc = split_in(hc @ w_in)

    xl = dw_conv(u, conv_w, conv_b)
    xcx = dw_conv(uc, conv_w, conv_b)
    lru_lat, lru_ctx = bidirectional_rglru(xl, xcx, lru_wa, lru_ba, lru_wx, lru_bx, lru_lambda)
    ya = (lru_lat * jax.nn.gelu(g)) @ w_out_lru

    q_l = heads_prep(q, N_HEADS, q_norm, tables)
    k_l = heads_prep(k, N_KV_HEADS, k_norm, tables)
    v_l = v.reshape(v.shape[0], v.shape[1], N_KV_HEADS, HEAD_DIM)
    k_c = heads_prep(kc, N_KV_HEADS, k_norm, None)
    v_c = vc.reshape(vc.shape[0], vc.shape[1], N_KV_HEADS, HEAD_DIM)
    attn_lat = block_attention(q_l, jnp.concatenate([k_c, k_l], axis=1), jnp.concatenate([v_c, v_l], axis=1))
    yb = attn_lat @ w_out_attn

    y = (jax.nn.sigmoid(ga) * ya + jax.nn.sigmoid(gb) * yb) @ w_out

    y_ctx = None
    if with_ctx_out:
        ya_c = (lru_ctx * jax.nn.gelu(gc)) @ w_out_lru
        q_c = heads_prep(qc, N_HEADS, q_norm, None)
        yb_c = block_attention(q_c, k_c, v_c) @ w_out_attn
        y_ctx = (jax.nn.sigmoid(gac) * ya_c + jax.nn.sigmoid(gbc) * yb_c) @ w_out
    return y, y_ctx


def swiglu(h, w_ffn_in, w_ffn_out):
    gate, up = jnp.split(h @ w_ffn_in, 2, axis=-1)
    return (jax.nn.silu(gate) * up) @ w_ffn_out


def setup_inputs(seed: int = 0) -> dict:
    key = jax.random.key(seed)
    ks = jax.random.split(key, 24)

    def nrm(k, shape, scale):
        return jax.random.normal(k, shape, jnp.float32) * scale

    u = jax.random.uniform(ks[10], (DEPTH, 2, LRU_WIDTH), jnp.float32, minval=0.9, maxval=0.999)
    p = u ** (1.0 / LRU_C)
    lam = jnp.log(p) - jnp.log1p(-p)
    return {
        'x': nrm(ks[0], (BATCH, SEQ, D_MODEL), 1.0),
        'c': nrm(ks[1], (BATCH, D_MODEL), 1.0),
        'ctx': nrm(ks[2], (BATCH, CTX_LEN, D_MODEL), 1.0),
        'c_ctx': nrm(ks[3], (D_MODEL,), 1.0),
        'w_mod': nrm(ks[4], (DEPTH, D_MODEL, 6 * D_MODEL), 0.5 * D_MODEL ** -0.5),
        'b_mod': nrm(ks[5], (DEPTH, 6 * D_MODEL), 0.01),
        'norm_mix': 1.0 + nrm(ks[6], (DEPTH, D_MODEL), 0.02),
        'w_in': nrm(ks[7], (DEPTH, D_MODEL, IN_COLS), D_MODEL ** -0.5),
        'conv_w': nrm(ks[8], (DEPTH, CONV_WIDTH, LRU_WIDTH), CONV_WIDTH ** -0.5),
        'conv_b': nrm(ks[9], (DEPTH, LRU_WIDTH), 0.01),
        'lru_wa': nrm(ks[11], (DEPTH, 2, LRU_BLOCKS, LRU_BLOCK_DIM, LRU_BLOCK_DIM), LRU_BLOCK_DIM ** -0.5),
        'lru_ba': nrm(ks[12], (DEPTH, 2, LRU_WIDTH), 0.01),
        'lru_wx': nrm(ks[13], (DEPTH, 2, LRU_BLOCKS, LRU_BLOCK_DIM, LRU_BLOCK_DIM), LRU_BLOCK_DIM ** -0.5),
        'lru_bx': nrm(ks[14], (DEPTH, 2, LRU_WIDTH), 0.01),
        'lru_lambda': lam,
        'q_norm': 1.0 + nrm(ks[15], (DEPTH, HEAD_DIM), 0.02),
        'k_norm': 1.0 + nrm(ks[16], (DEPTH, HEAD_DIM), 0.02),
        'w_out_lru': nrm(ks[17], (DEPTH, LRU_WIDTH, D_MODEL), LRU_WIDTH ** -0.5),
        'w_out_attn': nrm(ks[18], (DEPTH, ATTN_WIDTH, D_MODEL), ATTN_WIDTH ** -0.5),
        'w_out': nrm(ks[19], (DEPTH, D_MODEL, D_MODEL), D_MODEL ** -0.5),
        'norm_ffn': 1.0 + nrm(ks[20], (DEPTH, D_MODEL), 0.02),
        'w_ffn_in': nrm(ks[21], (DEPTH, D_MODEL, 2 * FFN_HIDDEN), D_MODEL ** -0.5),
        'w_ffn_out': nrm(ks[22], (DEPTH, FFN_HIDDEN, D_MODEL), FFN_HIDDEN ** -0.5),
    }


def reference(x, c, ctx, c_ctx, w_mod, b_mod, norm_mix, w_in, conv_w, conv_b, lru_wa, lru_ba, lru_wx,
              lru_bx, lru_lambda, q_norm, k_norm, w_out_lru, w_out_attn, w_out, norm_ffn, w_ffn_in, w_ffn_out):
    tables = axial_rope_tables(x.shape[1])
    silu_c = jax.nn.silu(c)
    silu_cc = jax.nn.silu(c_ctx)
    for i in range(DEPTH):
        last = i == DEPTH - 1
        mod = (silu_c @ w_mod[i] + b_mod[i])[:, None, :]
        mod_c = silu_cc @ w_mod[i] + b_mod[i]
        sh_m, sc_m, g_m, sh_f, sc_f, g_f = jnp.split(mod, 6, axis=-1)
        shc_m, scc_m, gc_m, shc_f, scc_f, gc_f = jnp.split(mod_c, 6, axis=-1)

        h = modulate(rmsnorm(x, norm_mix[i]), sh_m, sc_m)
        hc = modulate(rmsnorm(ctx, norm_mix[i]), shc_m, scc_m)
        y, y_ctx = hybrid_mixer(h, hc, tables, not last, w_in[i], conv_w[i], conv_b[i], lru_wa[i], lru_ba[i],
                                lru_wx[i], lru_bx[i], lru_lambda[i], q_norm[i], k_norm[i], w_out_lru[i],
                                w_out_attn[i], w_out[i])
        x = x + g_m * y
        x = x + g_f * swiglu(modulate(rmsnorm(x, norm_ffn[i]), sh_f, sc_f), w_ffn_in[i], w_ffn_out[i])
        if not last:
            ctx = ctx + gc_m * y_ctx
            ctx = ctx + gc_f * swiglu(modulate(rmsnorm(ctx, norm_ffn[i]), shc_f, scc_f), w_ffn_in[i], w_ffn_out[i])
    return x
```

```python
import functools

import jax
import jax.numpy as jnp
import numpy as np
from jax import lax
from jax.experimental import pallas as pl
from jax.experimental.pallas import tpu as pltpu

D_MODEL = 1024
GRID_W = 64
EPS = 1e-6
LRU_WIDTH = 1280
LRU_BLOCKS = 8
LRU_BLOCK_DIM = LRU_WIDTH // LRU_BLOCKS
LRU_HALF = LRU_WIDTH // 2
LRU_C = 8.0
CONV_WIDTH = 4
CONV_LEFT = 2
HEAD_DIM = 128
N_HEADS = 8
N_KV_HEADS = 2
GROUP = N_HEADS // N_KV_HEADS
ATTN_WIDTH = N_HEADS * HEAD_DIM
KV_WIDTH = N_KV_HEADS * HEAD_DIM
ROPE_AXIS_DIM = HEAD_DIM // 2
ROPE_THETA = 10000.0
FFN_HIDDEN = 2816

OFF_U = 0
OFF_G = OFF_U + LRU_WIDTH
OFF_Q = OFF_G + LRU_WIDTH
OFF_K = OFF_Q + ATTN_WIDTH
OFF_V = OFF_K + KV_WIDTH
OFF_GA = OFF_V + KV_WIDTH
OFF_GB = OFF_GA + D_MODEL
IN_COLS = OFF_GB + D_MODEL

TOKEN_TILE = 512
SCAN_TILE = 64
Q_TILE = 256
KEY_CHUNK = 512
FFN_CHUNK = 1408
VMEM_LIMIT = 56 * 1024 * 1024

F32 = jnp.float32
BF16 = jnp.bfloat16


def _sigmoid(x):
    return 0.5 * jnp.tanh(0.5 * x) + 0.5


def _gelu_tanh(x):
    return 0.5 * x * (1.0 + jnp.tanh(np.float32(np.sqrt(2.0 / np.pi)) * (x + 0.044715 * (x * x * x))))


def _rms_scale(x):
    return lax.rsqrt(jnp.mean(x * x, axis=-1, keepdims=True) + EPS)


def _resident(shape):
    nd = len(shape)
    return pl.BlockSpec(shape, lambda *_: (0,) * nd, pipeline_mode=pl.Buffered(1))


def _mod_kernel(c_ref, w_ref, b_ref, o_ref):
    c = c_ref[...]
    s = c * _sigmoid(c)
    o_ref[...] = jnp.dot(s, w_ref[...], preferred_element_type=F32,
                         precision=lax.Precision.HIGHEST) + b_ref[...]


def _mod_call(c_all, w_mod, b_mod):
    rows = c_all.shape[0]
    n = w_mod.shape[1]
    tn = 1024
    return pl.pallas_call(
        _mod_kernel,
        grid=(n // tn,),
        in_specs=[pl.BlockSpec((rows, D_MODEL), lambda j: (0, 0)),
                  pl.BlockSpec((D_MODEL, tn), lambda j: (0, j)),
                  pl.BlockSpec((1, tn), lambda j: (0, j))],
        out_specs=pl.BlockSpec((rows, tn), lambda j: (0, j)),
        out_shape=jax.ShapeDtypeStruct((rows, n), F32),
        compiler_params=pltpu.CompilerParams(dimension_semantics=("arbitrary",),
                                             vmem_limit_bytes=VMEM_LIMIT),
        name="mod",
    )(c_all, w_mod, b_mod.reshape(1, n))


def _head_norm_rope(t, gain, tables):
    t = t * _rms_scale(t) * gain
    if tables is not None:
        cos, sin_hi, sin_lo = tables
        t = (t * cos + pltpu.roll(t, HEAD_DIM - ROPE_AXIS_DIM // 2, 1) * sin_hi
             + pltpu.roll(t, ROPE_AXIS_DIM // 2, 1) * sin_lo)
    return t


def _in_proj_kernel(*refs, latent):
    if latent:
        (x_ref, mod_ref, nrm_ref, w_ref, qn_ref, kn_ref, rc_ref, rh_ref, rl_ref,
         u_ref, g_ref, q_ref, k_ref, v_ref, ga_ref, gb_ref) = refs
        tables = (rc_ref[...], rh_ref[...], rl_ref[...])
    else:
        x_ref, mod_ref, nrm_ref, w_ref, kn_ref, u_ref, k_ref, v_ref = refs
        tables = None
    x = x_ref[0]
    shift = mod_ref[0, :, 0:D_MODEL]
    scale = mod_ref[0, :, D_MODEL:2 * D_MODEL]
    h = (x * _rms_scale(x) * nrm_ref[...]) * (1.0 + scale) + shift
    h = h.astype(BF16)

    def proj(off, width):
        return jnp.dot(h, w_ref[:, off:off + width], preferred_element_type=F32)

    u_ref[...] = proj(OFF_U, LRU_WIDTH)
    kk = proj(OFF_K, KV_WIDTH)
    for j in range(N_KV_HEADS):
        sl = slice(j * HEAD_DIM, (j + 1) * HEAD_DIM)
        k_ref[0, :, sl] = _head_norm_rope(kk[:, sl], kn_ref[...], tables).astype(BF16)
    v_ref[0] = proj(OFF_V, KV_WIDTH).astype(BF16)
    if latent:
        g_ref[0] = _gelu_tanh(proj(OFF_G, LRU_WIDTH)).astype(BF16)
        qq = proj(OFF_Q, ATTN_WIDTH)
        qscale = np.float32(1.0 / np.sqrt(HEAD_DIM))
        for j in range(N_HEADS):
            sl = slice(j * HEAD_DIM, (j + 1) * HEAD_DIM)
            q_ref[0, :, sl] = (_head_norm_rope(qq[:, sl], qn_ref[...], tables) * qscale).astype(BF16)
        ga_ref[0] = _sigmoid(proj(OFF_GA, D_MODEL)).astype(BF16)
        gb_ref[0] = _sigmoid(proj(OFF_GB, D_MODEL)).astype(BF16)


def _in_proj_call(x, mod, norm_g, w_in_bf, q_gain, k_gain, tables, latent):
    bsz, n_tok, _ = x.shape
    tl = min(TOKEN_TILE, n_tok)
    grid = (bsz, n_tok // tl)
    mod_rows = mod.shape[0]
    mod_map = (lambda b, l: (b, 0, 0)) if mod_rows > 1 else (lambda b, l: (0, 0, 0))
    tok = lambda width: pl.BlockSpec((1, tl, width), lambda b, l: (b, l, 0))
    in_specs = [tok(D_MODEL),
                pl.BlockSpec((1, 1, mod.shape[-1]), mod_map),
                _resident((1, D_MODEL)),
                _resident((D_MODEL, IN_COLS))]
    args = [x, mod, norm_g.reshape(1, D_MODEL), w_in_bf]
    u_spec = pl.BlockSpec((tl, LRU_WIDTH), lambda b, l: (l, b))
    u_shape = jax.ShapeDtypeStruct((n_tok, bsz * LRU_WIDTH), F32)
    bf = lambda width: jax.ShapeDtypeStruct((bsz, n_tok, width), BF16)
    if latent:
        in_specs += [_resident((1, HEAD_DIM)), _resident((1, HEAD_DIM))]
        in_specs += [pl.BlockSpec((tl, HEAD_DIM), lambda b, l: (l, 0))] * 3
        args += [q_gain.reshape(1, HEAD_DIM), k_gain.reshape(1, HEAD_DIM), *tables]
        out_specs = [u_spec, tok(LRU_WIDTH), tok(ATTN_WIDTH), tok(KV_WIDTH), tok(KV_WIDTH),
                     tok(D_MODEL), tok(D_MODEL)]
        out_shape = [u_shape, bf(LRU_WIDTH), bf(ATTN_WIDTH), bf(KV_WIDTH), bf(KV_WIDTH),
                     bf(D_MODEL), bf(D_MODEL)]
    else:
        in_specs += [_resident((1, HEAD_DIM))]
        args += [k_gain.reshape(1, HEAD_DIM)]
        out_specs = [u_spec, tok(KV_WIDTH), tok(KV_WIDTH)]
        out_shape = [u_shape, bf(KV_WIDTH), bf(KV_WIDTH)]
    return pl.pallas_call(
        functools.partial(_in_proj_kernel, latent=latent),
        grid=grid, in_specs=in_specs, out_specs=out_specs, out_shape=out_shape,
        compiler_params=pltpu.CompilerParams(dimension_semantics=("arbitrary", "arbitrary"),
                                             vmem_limit_bytes=VMEM_LIMIT),
        name="in_proj_latent" if latent else "in_proj_ctx",
    )(*args)


def _lru_kernel(u_ref, prev_ref, next_ref, h0_ref, cw_ref, cb_ref, wg_ref, bg_ref, lam_ref,
                h_ref, hlast_ref, a_s, b_s, carry_s, *, n_tiles):
    d = pl.program_id(0)
    t = pl.program_id(1)
    tile = jnp.where(d == 0, t, n_tiles - 1 - t)
    ts = u_ref.shape[0]
    bsz = u_ref.shape[1]

    @pl.when(t == 0)
    def _():
        carry_s[...] = h0_ref[0]

    u = u_ref[...]
    prev = prev_ref[...] * jnp.where(tile > 0, 1.0, 0.0)
    nxt = next_ref[...] * jnp.where(tile < n_tiles - 1, 1.0, 0.0)
    ext = jnp.concatenate([prev, u, nxt], axis=0)
    xc = cb_ref[...][None]
    for j in range(CONV_WIDTH):
        xc = xc + ext[j:j + ts] * cw_ref[j:j + 1, :][None]
    xc2 = xc.reshape(ts * bsz, LRU_WIDTH)
    xb = xc2.astype(BF16)

    sp = jnp.maximum(-lam_ref[0], 0.0) + jnp.log1p(jnp.exp(-jnp.abs(lam_ref[0])))
    for hf in range(2):
        cols = slice(hf * LRU_HALF, (hf + 1) * LRU_HALF)
        gates = jnp.dot(xb[:, cols], wg_ref[0, hf], preferred_element_type=F32)
        r = _sigmoid(gates[:, :LRU_HALF] + bg_ref[0, 0:1, cols])
        i = _sigmoid(gates[:, LRU_HALF:] + bg_ref[0, 1:2, cols])
        a = jnp.exp((-LRU_C) * r * sp[:, cols])
        bcoef = jnp.sqrt(1.0 - a * a) * (i * xc2[:, cols])
        a_s[:, :, cols] = a.reshape(ts, bsz, LRU_HALF)
        b_s[:, :, cols] = bcoef.reshape(ts, bsz, LRU_HALF)

    def step(s, h):
        idx = jnp.where(d == 0, s, ts - 1 - s)
        h = a_s[idx] * h + b_s[idx]
        h_ref[0, idx] = h
        return h

    h = lax.fori_loop(0, ts, step, carry_s[...], unroll=8)
    carry_s[...] = h

    @pl.when(t == n_tiles - 1)
    def _():
        hlast_ref[0] = h


def _lru_call(u_t, h0, conv_w, conv_b, w_gate, b_gate, lam):
    n_tok, bsz, _ = u_t.shape
    ts = SCAN_TILE
    n_tiles = n_tok // ts
    pos = lambda d, t: jnp.where(d == 0, t, n_tiles - 1 - t)
    in_specs = [
        pl.BlockSpec((ts, bsz, LRU_WIDTH), lambda d, t: (pos(d, t), 0, 0)),
        pl.BlockSpec((CONV_LEFT, bsz, LRU_WIDTH),
                     lambda d, t: (jnp.maximum(pos(d, t) * (ts // CONV_LEFT) - 1, 0), 0, 0)),
        pl.BlockSpec((1, bsz, LRU_WIDTH),
                     lambda d, t: (jnp.minimum((pos(d, t) + 1) * ts, n_tok - 1), 0, 0)),
        pl.BlockSpec((1, bsz, LRU_WIDTH), lambda d, t: (d, 0, 0)),
        _resident((CONV_WIDTH, LRU_WIDTH)),
        _resident((1, LRU_WIDTH)),
        pl.BlockSpec((1, 2, LRU_HALF, 2 * LRU_HALF), lambda d, t: (d, 0, 0, 0)),
        pl.BlockSpec((1, 2, LRU_WIDTH), lambda d, t: (d, 0, 0)),
        pl.BlockSpec((1, 1, LRU_WIDTH), lambda d, t: (d, 0, 0)),
    ]
    out_specs = [
        pl.BlockSpec((1, ts, bsz, LRU_WIDTH), lambda d, t: (d, pos(d, t), 0, 0)),
        pl.BlockSpec((1, bsz, LRU_WIDTH), lambda d, t: (d, 0, 0)),
    ]
    out_shape = [jax.ShapeDtypeStruct((2, n_tok, bsz, LRU_WIDTH), F32),
                 jax.ShapeDtypeStruct((2, bsz, LRU_WIDTH), F32)]
    return pl.pallas_call(
        functools.partial(_lru_kernel, n_tiles=n_tiles),
        grid=(2, n_tiles), in_specs=in_specs, out_specs=out_specs, out_shape=out_shape,
        scratch_shapes=[pltpu.VMEM((ts, bsz, LRU_WIDTH), F32),
                        pltpu.VMEM((ts, bsz, LRU_WIDTH), F32),
                        pltpu.VMEM((bsz, LRU_WIDTH), F32)],
        compiler_params=pltpu.CompilerParams(dimension_semantics=("arbitrary", "arbitrary"),
                                             vmem_limit_bytes=VMEM_LIMIT),
        name="lru",
    )(u_t, u_t, u_t, h0, conv_w, conv_b.reshape(1, LRU_WIDTH), w_gate, b_gate,
      lam.reshape(2, 1, LRU_WIDTH))


def _attn_kernel(q_ref, kc_ref, vc_ref, kl_ref, vl_ref, o_ref):
    tq = q_ref.shape[1]
    q = jnp.concatenate([q_ref[0, :, g * HEAD_DIM:(g + 1) * HEAD_DIM] for g in range(GROUP)], axis=0)
    rows = GROUP * tq

    def update(k, v, carry):
        m, l, acc = carry
        s = lax.dot_general(q, k, (((1,), (1,)), ((), ())), preferred_element_type=F32)
        m_new = jnp.maximum(m, jnp.max(s, axis=-1, keepdims=True))
        alpha = jnp.exp(m - m_new)
        p = jnp.exp(s - m_new)
        l = alpha * l + jnp.sum(p, axis=-1, keepdims=True)
        acc = alpha * acc + jnp.dot(p.astype(BF16), v, preferred_element_type=F32)
        return m_new, l, acc

    carry = (jnp.full((rows, 1), -1e30, F32), jnp.zeros((rows, 1), F32),
             jnp.zeros((rows, HEAD_DIM), F32))
    carry = update(kc_ref[0], vc_ref[0], carry)

    def body(c, carry):
        start = pl.multiple_of(c * KEY_CHUNK, KEY_CHUNK)
        return update(kl_ref[0, pl.ds(start, KEY_CHUNK), :], vl_ref[0, pl.ds(start, KEY_CHUNK), :], carry)

    _, l, acc = lax.fori_loop(0, kl_ref.shape[1] // KEY_CHUNK, body, carry)
    o = acc / l
    for g in range(GROUP):
        o_ref[0, :, g * HEAD_DIM:(g + 1) * HEAD_DIM] = o[g * tq:(g + 1) * tq].astype(BF16)


def _attn_call(q, k_ctx, v_ctx, k_lat, v_lat):
    bsz, n_tok, _ = q.shape
    n_ctx = k_ctx.shape[1]
    tq = Q_TILE
    gw = GROUP * HEAD_DIM
    kv = lambda n: pl.BlockSpec((1, n, HEAD_DIM), lambda b, j, i: (b, 0, j))
    return pl.pallas_call(
        _attn_kernel,
        grid=(bsz, N_KV_HEADS, n_tok // tq),
        in_specs=[pl.BlockSpec((1, tq, gw), lambda b, j, i: (b, i, j)),
                  kv(n_ctx), kv(n_ctx), kv(n_tok), kv(n_tok)],
        out_specs=pl.BlockSpec((1, tq, gw), lambda b, j, i: (b, i, j)),
        out_shape=jax.ShapeDtypeStruct((bsz, n_tok, ATTN_WIDTH), BF16),
        compiler_params=pltpu.CompilerParams(
            dimension_semantics=("arbitrary", "arbitrary", "arbitrary"),
            vmem_limit_bytes=VMEM_LIMIT),
        name="attn",
    )(q, k_ctx, v_ctx, k_lat, v_lat)


def _mix_out_kernel(h_ref, g_ref, at_ref, ga_ref, gb_ref, x_ref, mod_ref,
                    wl_ref, wa_ref, wo_ref, o_ref):
    za = ((h_ref[0] + h_ref[1]) * g_ref[0].astype(F32)).astype(BF16)
    ya = jnp.dot(za, wl_ref[...], preferred_element_type=F32)
    yb = jnp.dot(at_ref[0], wa_ref[...], preferred_element_type=F32)
    mix = (ga_ref[0].astype(F32) * ya + gb_ref[0].astype(F32) * yb).astype(BF16)
    y = jnp.dot(mix, wo_ref[...], preferred_element_type=F32)
    o_ref[0] = x_ref[0] + mod_ref[0, :, 2 * D_MODEL:3 * D_MODEL] * y


def _mix_out_call(h_t, gg, attn, sga, sgb, x, mod, w_lru_bf, w_attn_bf, w_out_bf):
    bsz, n_tok, _ = x.shape
    tl = TOKEN_TILE
    tok = lambda width: pl.BlockSpec((1, tl, width), lambda b, l: (b, l, 0))
    return pl.pallas_call(
        _mix_out_kernel,
        grid=(bsz, n_tok // tl),
        in_specs=[pl.BlockSpec((2, tl, LRU_WIDTH), lambda b, l: (0, l, b)),
                  tok(LRU_WIDTH), tok(ATTN_WIDTH), tok(D_MODEL), tok(D_MODEL), tok(D_MODEL),
                  pl.BlockSpec((1, 1, mod.shape[-1]), lambda b, l: (b, 0, 0)),
                  _resident((LRU_WIDTH, D_MODEL)), _resident((ATTN_WIDTH, D_MODEL)),
                  _resident((D_MODEL, D_MODEL))],
        out_specs=tok(D_MODEL),
        out_shape=jax.ShapeDtypeStruct((bsz, n_tok, D_MODEL), F32),
        compiler_params=pltpu.CompilerParams(dimension_semantics=("arbitrary", "arbitrary"),
                                             vmem_limit_bytes=VMEM_LIMIT),
        name="mix_out",
    )(h_t, gg, attn, sga, sgb, x, mod, w_lru_bf, w_attn_bf, w_out_bf)


def _ffn_kernel(x_ref, mod_ref, nrm_ref, wi_ref, wo_ref, o_ref):
    x = x_ref[0]
    shift = mod_ref[0, :, 3 * D_MODEL:4 * D_MODEL]
    scale = mod_ref[0, :, 4 * D_MODEL:5 * D_MODEL]
    gate_out = mod_ref[0, :, 5 * D_MODEL:6 * D_MODEL]
    h = ((x * _rms_scale(x) * nrm_ref[...]) * (1.0 + scale) + shift).astype(BF16)
    acc = jnp.zeros(x.shape, F32)
    for j in range(FFN_HIDDEN // FFN_CHUNK):
        lo = j * FFN_CHUNK
        gate = jnp.dot(h, wi_ref[:, lo:lo + FFN_CHUNK], preferred_element_type=F32)
        up = jnp.dot(h, wi_ref[:, FFN_HIDDEN + lo:FFN_HIDDEN + lo + FFN_CHUNK],
                     preferred_element_type=F32)
        act = (gate * _sigmoid(gate) * up).astype(BF16)
        acc = acc + jnp.dot(act, wo_ref[lo:lo + FFN_CHUNK, :], preferred_element_type=F32)
    o_ref[0] = x + gate_out * acc


def _ffn_call(x, mod, norm_g, w_in_bf, w_out_bf):
    bsz, n_tok, _ = x.shape
    tl = TOKEN_TILE
    tok = pl.BlockSpec((1, tl, D_MODEL), lambda b, l: (b, l, 0))
    return pl.pallas_call(
        _ffn_kernel,
        grid=(bsz, n_tok // tl),
        in_specs=[tok, pl.BlockSpec((1, 1, mod.shape[-1]), lambda b, l: (b, 0, 0)),
                  _resident((1, D_MODEL)),
                  _resident((D_MODEL, 2 * FFN_HIDDEN)), _resident((FFN_HIDDEN, D_MODEL))],
        out_specs=tok,
        out_shape=jax.ShapeDtypeStruct((bsz, n_tok, D_MODEL), F32),
        compiler_params=pltpu.CompilerParams(dimension_semantics=("arbitrary", "arbitrary"),
                                             vmem_limit_bytes=VMEM_LIMIT),
        name="ffn",
    )(x, mod, norm_g.reshape(1, D_MODEL), w_in_bf, w_out_bf)


def _rope_tables(n_tok):
    rows = n_tok // GRID_W
    row = jnp.repeat(jnp.arange(rows, dtype=F32), GRID_W)
    col = jnp.tile(jnp.arange(GRID_W, dtype=F32), rows)
    inv_freq = ROPE_THETA ** (-jnp.arange(0, ROPE_AXIS_DIM, 2, dtype=F32) / ROPE_AXIS_DIM)
    ang_r = row[:, None] * inv_freq[None, :]
    ang_c = col[:, None] * inv_freq[None, :]
    cr, sr, cc, sc = (f(a) for a in (ang_r, ang_c) for f in (jnp.cos, jnp.sin))
    zero = jnp.zeros_like(sr)
    cos = jnp.concatenate([cr, cr, cc, cc], axis=-1)
    sin_hi = jnp.concatenate([-sr, zero, -sc, zero], axis=-1)
    sin_lo = jnp.concatenate([zero, sr, zero, sc], axis=-1)
    return cos, sin_hi, sin_lo


def _gate_weights(wa, wx):
    per_half = LRU_BLOCKS // 2

    def block_diag(w):
        w = w.reshape(2, 2, per_half, LRU_BLOCK_DIM, LRU_BLOCK_DIM)
        eye = jnp.eye(per_half, dtype=w.dtype)
        full = jnp.einsum('dhnij,nm->dhnimj', w, eye)
        return full.reshape(2, 2, LRU_HALF, LRU_HALF)

    return jnp.concatenate([block_diag(wa), block_diag(wx)], axis=-1).astype(BF16)


def kernel(x, c, ctx, c_ctx, w_mod, b_mod, norm_mix, w_in, conv_w, conv_b, lru_wa, lru_ba, lru_wx,
           lru_bx, lru_lambda, q_norm, k_norm, w_out_lru, w_out_attn, w_out, norm_ffn, w_ffn_in,
           w_ffn_out):
    bsz, n_tok, _ = x.shape
    n_ctx = ctx.shape[1]
    assert w_mod.shape[0] == 1, "single trunk layer"
    tables = _rope_tables(n_tok)

    c_all = jnp.concatenate([c, c_ctx[None, :]], axis=0)
    c_all = jnp.pad(c_all, ((0, 16 - c_all.shape[0]), (0, 0)))
    mod_all = _mod_call(c_all, w_mod[0], b_mod[0])
    mod = mod_all[:bsz].reshape(bsz, 1, -1)
    mod_c = mod_all[bsz:bsz + 1].reshape(1, 1, -1)

    w_in_bf = w_in[0].astype(BF16)
    u_c, k_c, v_c = _in_proj_call(ctx, mod_c, norm_mix[0], w_in_bf, None, k_norm[0], None, latent=False)
    u_l, gg, q, k_l, v_l, sga, sgb = _in_proj_call(x, mod, norm_mix[0], w_in_bf, q_norm[0], k_norm[0],
                                                   tables, latent=True)

    w_gate = _gate_weights(lru_wa[0], lru_wx[0])
    b_gate = jnp.stack([lru_ba[0], lru_bx[0]], axis=1)
    zeros = jnp.zeros((2, bsz, LRU_WIDTH), F32)
    _, h_seed = _lru_call(u_c.reshape(n_ctx, bsz, LRU_WIDTH), zeros, conv_w[0], conv_b[0],
                          w_gate, b_gate, lru_lambda[0])
    h_t, _ = _lru_call(u_l.reshape(n_tok, bsz, LRU_WIDTH), h_seed, conv_w[0], conv_b[0],
                       w_gate, b_gate, lru_lambda[0])

    attn = _attn_call(q, k_c, v_c, k_l, v_l)

    x1 = _mix_out_call(h_t.reshape(2, n_tok, bsz * LRU_WIDTH), gg, attn, sga, sgb, x, mod,
                       w_out_lru[0].astype(BF16), w_out_attn[0].astype(BF16), w_out[0].astype(BF16))
    return _ffn_call(x1, mod, norm_ffn[0], w_ffn_in[0].astype(BF16), w_ffn_out[0].astype(BF16))
```

```python
import functools

import jax
import jax.numpy as jnp
import numpy as np
from jax import lax
from jax.experimental import pallas as pl
from jax.experimental.pallas import tpu as pltpu

D_MODEL = 1024
GRID_W = 64
EPS = 1e-6
LRU_WIDTH = 1280
LRU_BLOCKS = 8
LRU_BLOCK_DIM = LRU_WIDTH // LRU_BLOCKS
LRU_C = 8.0
CONV_WIDTH = 4
CONV_LEFT = 2
HEAD_DIM = 128
LANES = 128
N_HEADS = 8
N_KV_HEADS = 2
GROUP = N_HEADS // N_KV_HEADS
ATTN_WIDTH = N_HEADS * HEAD_DIM
KV_WIDTH = N_KV_HEADS * HEAD_DIM
ROPE_AXIS_DIM = HEAD_DIM // 2
ROPE_THETA = 10000.0
FFN_HIDDEN = 2816

OFF_U = 0
OFF_G = OFF_U + LRU_WIDTH
OFF_Q = OFF_G + LRU_WIDTH
OFF_K = OFF_Q + ATTN_WIDTH
OFF_V = OFF_K + KV_WIDTH
OFF_GA = OFF_V + KV_WIDTH
OFF_GB = OFF_GA + D_MODEL
IN_COLS = OFF_GB + D_MODEL

TOKEN_TILE = 512
SCAN_TILE = 64
PERM_STEPS = 32
HALO = 16
GATE_COLS = 256
GATE_K = 512
Q_TILE = 256
KEY_CHUNK = 512
FFN_CHUNK = 1408
VMEM_LIMIT = 56 * 1024 * 1024

F32 = jnp.float32
BF16 = jnp.bfloat16


def _gate_k_offsets():
    offs = []
    for j in range(LRU_WIDTH // GATE_COLS):
        first_row = (j * GATE_COLS // LRU_BLOCK_DIM) * LRU_BLOCK_DIM
        last_row = ((j + 1) * GATE_COLS - 1) // LRU_BLOCK_DIM * LRU_BLOCK_DIM + LRU_BLOCK_DIM
        k0 = min(first_row // LANES * LANES, LRU_WIDTH - GATE_K)
        assert k0 <= first_row and last_row <= k0 + GATE_K
        offs.append(k0)
    return tuple(offs)


GATE_K0 = _gate_k_offsets()


def _sigmoid(x):
    return 0.5 * jnp.tanh(0.5 * x) + 0.5


def _gelu_tanh(x):
    return 0.5 * x * (1.0 + jnp.tanh(np.float32(np.sqrt(2.0 / np.pi)) * (x + 0.044715 * (x * x * x))))


def _rms_scale(x):
    return lax.rsqrt(jnp.mean(x * x, axis=-1, keepdims=True) + EPS)


def _resident(shape):
    nd = len(shape)
    return pl.BlockSpec(shape, lambda *_: (0,) * nd, pipeline_mode=pl.Buffered(1))


def _mod_kernel(c_ref, w_ref, b_ref, o_ref):
    c = c_ref[...]
    s = c * _sigmoid(c)
    o_ref[...] = jnp.dot(s, w_ref[...], preferred_element_type=F32,
                         precision=lax.Precision.HIGHEST) + b_ref[...]


def _mod_call(c_all, w_mod, b_mod):
    rows = c_all.shape[0]
    n = w_mod.shape[1]
    tn = 1024
    return pl.pallas_call(
        _mod_kernel,
        grid=(n // tn,),
        in_specs=[pl.BlockSpec((rows, D_MODEL), lambda j: (0, 0)),
                  pl.BlockSpec((D_MODEL, tn), lambda j: (0, j)),
                  pl.BlockSpec((1, tn), lambda j: (0, j))],
        out_specs=pl.BlockSpec((rows, tn), lambda j: (0, j)),
        out_shape=jax.ShapeDtypeStruct((rows, n), F32),
        compiler_params=pltpu.CompilerParams(dimension_semantics=("arbitrary",),
                                             vmem_limit_bytes=VMEM_LIMIT),
        name="mod",
    )(c_all, w_mod, b_mod.reshape(1, n))


def _head_norm_rope(t, gain, tables):
    t = t * _rms_scale(t) * gain
    if tables is not None:
        cos, sin_hi, sin_lo = tables
        t = (t * cos + pltpu.roll(t, HEAD_DIM - ROPE_AXIS_DIM // 2, 1) * sin_hi
             + pltpu.roll(t, ROPE_AXIS_DIM // 2, 1) * sin_lo)
    return t


def _in_proj_kernel(*refs, latent):
    if latent:
        (x_ref, mod_ref, nrm_ref, w_ref, qn_ref, kn_ref, rc_ref, rh_ref, rl_ref,
         u_ref, g_ref, q_ref, k_ref, vt_ref, ga_ref, gb_ref) = refs
        tables = (rc_ref[...], rh_ref[...], rl_ref[...])
    else:
        x_ref, mod_ref, nrm_ref, w_ref, kn_ref, u_ref, k_ref, vt_ref = refs
        tables = None
    x = x_ref[0]
    shift = mod_ref[0, :, 0:D_MODEL]
    scale = mod_ref[0, :, D_MODEL:2 * D_MODEL]
    h = (x * _rms_scale(x) * nrm_ref[...]) * (1.0 + scale) + shift
    h = h.astype(BF16)

    def proj(off, width):
        return jnp.dot(h, w_ref[:, off:off + width], preferred_element_type=F32)

    u_ref[0] = proj(OFF_U, LRU_WIDTH).astype(BF16)
    kk = proj(OFF_K, KV_WIDTH)
    for j in range(N_KV_HEADS):
        sl = slice(j * HEAD_DIM, (j + 1) * HEAD_DIM)
        k_ref[0, :, sl] = _head_norm_rope(kk[:, sl], kn_ref[...], tables).astype(BF16)
    vt_ref[0] = proj(OFF_V, KV_WIDTH).T.astype(BF16)
    if latent:
        g_ref[0] = _gelu_tanh(proj(OFF_G, LRU_WIDTH)).astype(BF16)
        qq = proj(OFF_Q, ATTN_WIDTH)
        qscale = np.float32(np.log2(np.e) / np.sqrt(HEAD_DIM))
        for j in range(N_HEADS):
            sl = slice(j * HEAD_DIM, (j + 1) * HEAD_DIM)
            q_ref[0, :, sl] = (_head_norm_rope(qq[:, sl], qn_ref[...], tables) * qscale).astype(BF16)
        ga_ref[0] = _sigmoid(proj(OFF_GA, D_MODEL)).astype(BF16)
        gb_ref[0] = _sigmoid(proj(OFF_GB, D_MODEL)).astype(BF16)


def _in_proj_call(x, mod, norm_g, w_in_bf, q_gain, k_gain, tables, latent):
    bsz, n_tok, _ = x.shape
    tl = min(TOKEN_TILE, n_tok)
    grid = (bsz, n_tok // tl)
    mod_rows = mod.shape[0]
    mod_map = (lambda b, l: (b, 0, 0)) if mod_rows > 1 else (lambda b, l: (0, 0, 0))
    tok = lambda width: pl.BlockSpec((1, tl, width), lambda b, l: (b, l, 0))
    in_specs = [tok(D_MODEL),
                pl.BlockSpec((1, 1, mod.shape[-1]), mod_map),
                _resident((1, D_MODEL)),
                _resident((D_MODEL, IN_COLS))]
    args = [x, mod, norm_g.reshape(1, D_MODEL), w_in_bf]
    vt_spec = pl.BlockSpec((1, KV_WIDTH, tl), lambda b, l: (b, 0, l))
    vt_shape = jax.ShapeDtypeStruct((bsz, KV_WIDTH, n_tok), BF16)
    bf = lambda width: jax.ShapeDtypeStruct((bsz, n_tok, width), BF16)
    if latent:
        in_specs += [_resident((1, HEAD_DIM)), _resident((1, HEAD_DIM))]
        in_specs += [pl.BlockSpec((tl, HEAD_DIM), lambda b, l: (l, 0))] * 3
        args += [q_gain.reshape(1, HEAD_DIM), k_gain.reshape(1, HEAD_DIM), *tables]
        out_specs = [tok(LRU_WIDTH), tok(LRU_WIDTH), tok(ATTN_WIDTH), tok(KV_WIDTH), vt_spec,
                     tok(D_MODEL), tok(D_MODEL)]
        out_shape = [bf(LRU_WIDTH), bf(LRU_WIDTH), bf(ATTN_WIDTH), bf(KV_WIDTH), vt_shape,
                     bf(D_MODEL), bf(D_MODEL)]
    else:
        in_specs += [_resident((1, HEAD_DIM))]
        args += [k_gain.reshape(1, HEAD_DIM)]
        out_specs = [tok(LRU_WIDTH), tok(KV_WIDTH), vt_spec]
        out_shape = [bf(LRU_WIDTH), bf(KV_WIDTH), vt_shape]
    return pl.pallas_call(
        functools.partial(_in_proj_kernel, latent=latent),
        grid=grid, in_specs=in_specs, out_specs=out_specs, out_shape=out_shape,
        compiler_params=pltpu.CompilerParams(dimension_semantics=("arbitrary", "arbitrary"),
                                             vmem_limit_bytes=VMEM_LIMIT),
        name="in_proj_latent" if latent else "in_proj_ctx",
    )(*args)


def _lru_kernel(u_ref, prev_ref, next_ref, h0_ref, cw_ref, cb_ref, wg_ref, bg_ref, lam_ref,
                perm_ref, permt_ref, halo_ref, h_ref, hlast_ref, ext_s, a_s, b_s, carry_s,
                *, n_tiles, reverse):
    t = pl.program_id(0)
    tile = (n_tiles - 1 - t) if reverse else t
    bsz, ts, _ = u_ref.shape

    @pl.when(t == 0)
    def _():
        carry_s[...] = h0_ref[...]

    for p in range(ts // PERM_STEPS):
        steps = slice(p * PERM_STEPS, (p + 1) * PERM_STEPS)
        u_bm = u_ref[:, steps, :].reshape(bsz * PERM_STEPS, LRU_WIDTH)
        u_tm = jnp.dot(perm_ref[...], u_bm, preferred_element_type=F32)
        ext_s[CONV_LEFT + p * PERM_STEPS:CONV_LEFT + (p + 1) * PERM_STEPS] = (
            u_tm.reshape(PERM_STEPS, bsz, LRU_WIDTH))
    edge = jnp.concatenate([prev_ref[:, HALO // 2:, :].reshape(bsz * HALO // 2, LRU_WIDTH),
                            next_ref[:, :HALO // 2, :].reshape(bsz * HALO // 2, LRU_WIDTH)], axis=0)
    halo = jnp.dot(halo_ref[...], edge, preferred_element_type=F32)
    ext_s[0:CONV_LEFT] = (halo[0:CONV_LEFT * bsz]
                          * jnp.where(tile > 0, 1.0, 0.0)).reshape(CONV_LEFT, bsz, LRU_WIDTH)
    ext_s[CONV_LEFT + ts:] = (halo[CONV_LEFT * bsz:(CONV_LEFT + 1) * bsz]
                              * jnp.where(tile < n_tiles - 1, 1.0, 0.0)).reshape(1, bsz, LRU_WIDTH)

    xc = cb_ref[...][None]
    for j in range(CONV_WIDTH):
        xc = xc + ext_s[j:j + ts] * cw_ref[j:j + 1, :][None]
    xc2 = xc.reshape(ts * bsz, LRU_WIDTH)
    xb = xc2.astype(BF16)

    sp = jnp.maximum(-lam_ref[...], 0.0) + jnp.log1p(jnp.exp(-jnp.abs(lam_ref[...])))
    decay = sp * np.float32(-0.5 * LRU_C * np.log2(np.e))
    xh = 0.5 * xc2
    for j, k0 in enumerate(GATE_K0):
        cols = slice(j * GATE_COLS, (j + 1) * GATE_COLS)
        gates = jnp.dot(xb[:, k0:k0 + GATE_K], wg_ref[j], preferred_element_type=F32)
        t_r = jnp.tanh(gates[:, :GATE_COLS] + bg_ref[0:1, cols])
        t_i = jnp.tanh(gates[:, GATE_COLS:] + bg_ref[1:2, cols])
        a = jnp.exp2(decay[:, cols] * t_r + decay[:, cols])
        om = 1.0 - a * a
        coef = om * lax.rsqrt(jnp.maximum(om, 1e-30))
        a_s[:, :, cols] = a.reshape(ts, bsz, GATE_COLS)
        b_s[:, :, cols] = (coef * ((t_i + 1.0) * xh[:, cols])).reshape(ts, bsz, GATE_COLS)

    def step(s, h):
        idx = (ts - 1 - s) if reverse else s
        h = a_s[idx] * h + b_s[idx]
        b_s[idx] = h
        return h

    h = lax.fori_loop(0, ts, step, carry_s[...], unroll=8)
    carry_s[...] = h
    for p in range(ts // PERM_STEPS):
        steps = slice(p * PERM_STEPS, (p + 1) * PERM_STEPS)
        h_tm = b_s[steps].reshape(PERM_STEPS * bsz, LRU_WIDTH).astype(BF16)
        h_bm = jnp.dot(permt_ref[...], h_tm, preferred_element_type=F32)
        h_ref[:, steps, :] = h_bm.reshape(bsz, PERM_STEPS, LRU_WIDTH).astype(BF16)

    @pl.when(t == n_tiles - 1)
    def _():
        hlast_ref[...] = h


def _perm_matrices(bsz):
    n = bsz * PERM_STEPS
    r_out = np.arange(n)
    s, b = r_out // bsz, r_out % bsz
    perm = np.zeros((n, n), np.float32)
    perm[r_out, b * PERM_STEPS + s] = 1.0
    half = HALO // 2
    halo = np.zeros((4 * bsz, 2 * bsz * half), np.float32)
    for b_ in range(bsz):
        for k in range(CONV_LEFT):
            halo[k * bsz + b_, b_ * half + half - CONV_LEFT + k] = 1.0
        halo[CONV_LEFT * bsz + b_, bsz * half + b_ * half] = 1.0
    return jnp.asarray(perm, BF16), jnp.asarray(perm.T, BF16), jnp.asarray(halo, BF16)


def _lru_call(u, h0, conv_w, conv_b, w_gate, b_gate, lam, reverse):
    bsz, n_tok, _ = u.shape
    ts = SCAN_TILE
    n_tiles = n_tok // ts
    pos = (lambda t: n_tiles - 1 - t) if reverse else (lambda t: t)
    per = ts // HALO
    perm, perm_t, halo = _perm_matrices(bsz)
    in_specs = [
        pl.BlockSpec((bsz, ts, LRU_WIDTH), lambda t: (0, pos(t), 0)),
        pl.BlockSpec((bsz, HALO, LRU_WIDTH), lambda t: (0, jnp.maximum(pos(t) * per - 1, 0), 0)),
        pl.BlockSpec((bsz, HALO, LRU_WIDTH),
                     lambda t: (0, jnp.minimum((pos(t) + 1) * per, n_tok // HALO - 1), 0)),
        _resident((bsz, LRU_WIDTH)),
        _resident((CONV_WIDTH, LRU_WIDTH)),
        _resident((1, LRU_WIDTH)),
        _resident((len(GATE_K0), GATE_K, 2 * GATE_COLS)),
        _resident((2, LRU_WIDTH)),
        _resident((1, LRU_WIDTH)),
        _resident(perm.shape), _resident(perm_t.shape), _resident(halo.shape),
    ]
    out_specs = [pl.BlockSpec((bsz, ts, LRU_WIDTH), lambda t: (0, pos(t), 0)),
                 pl.BlockSpec((bsz, LRU_WIDTH), lambda t: (0, 0))]
    out_shape = [jax.ShapeDtypeStruct((bsz, n_tok, LRU_WIDTH), BF16),
                 jax.ShapeDtypeStruct((bsz, LRU_WIDTH), F32)]
    return pl.pallas_call(
        functools.partial(_lru_kernel, n_tiles=n_tiles, reverse=reverse),
        grid=(n_tiles,), in_specs=in_specs, out_specs=out_specs, out_shape=out_shape,
        scratch_shapes=[pltpu.VMEM((ts + CONV_WIDTH - 1, bsz, LRU_WIDTH), F32),
                        pltpu.VMEM((ts, bsz, LRU_WIDTH), F32),
                        pltpu.VMEM((ts, bsz, LRU_WIDTH), F32),
                        pltpu.VMEM((bsz, LRU_WIDTH), F32)],
        compiler_params=pltpu.CompilerParams(dimension_semantics=("arbitrary",),
                                             vmem_limit_bytes=VMEM_LIMIT),
        name="lru_bwd" if reverse else "lru_fwd",
    )(u, u, u, h0, conv_w, conv_b.reshape(1, LRU_WIDTH), w_gate, b_gate, lam.reshape(1, LRU_WIDTH),
      perm, perm_t, halo)


def _attn_kernel(q_ref, kc_ref, vc_ref, kl_ref, vl_ref, o_ref, sa_s, sb_s, acc_s, m_s, l_s):
    n_chunks = kl_ref.shape[1] // KEY_CHUNK

    def scores(k, g):
        qg = q_ref[0, :, g * HEAD_DIM:(g + 1) * HEAD_DIM]
        return lax.dot_general(k, qg, (((1,), (1,)), ((), ())), preferred_element_type=F32)

    def softmax_pv(st, vt, g, first):
        m_c = jnp.max(st, axis=0, keepdims=True)
        if first:
            m_new = m_c
        else:
            m_old = m_s[g]
            m_new = jnp.maximum(m_old, m_c)
            alpha = jnp.exp2(m_old - m_new)
        p = jnp.exp2(st - m_new)
        p_sum = jnp.sum(p, axis=0, keepdims=True)
        pv = jnp.dot(vt, p.astype(BF16), preferred_element_type=F32)
        if first:
            l_s[g] = p_sum
            acc_s[g] = pv
        else:
            l_s[g] = alpha * l_s[g] + p_sum
            acc_s[g] = alpha * acc_s[g] + pv
        m_s[g] = m_new

    def lat_keys(c):
        return kl_ref[0, pl.ds(pl.multiple_of(c * KEY_CHUNK, KEY_CHUNK), KEY_CHUNK), :]

    def lat_vals(c):
        return vl_ref[0, :, pl.ds(pl.multiple_of(c * KEY_CHUNK, KEY_CHUNK), KEY_CHUNK)]

    def stage(k_next, st_in, st_out, vt, first=False):
        for g in range(GROUP):
            if k_next is not None:
                st_out[g] = scores(k_next, g)
            softmax_pv(st_in[g], vt, g, first)

    k0 = kl_ref[0, 0:KEY_CHUNK, :]
    for g in range(GROUP):
        sc = scores(kc_ref[0], g)
        sa_s[g] = scores(k0, g)
        softmax_pv(sc, vc_ref[0], g, True)

    def body(i, carry):
        stage(lat_keys(2 * i + 1), sa_s, sb_s, lat_vals(2 * i))
        stage(lat_keys(2 * i + 2), sb_s, sa_s, lat_vals(2 * i + 1))
        return carry

    lax.fori_loop(0, n_chunks // 2 - 1, body, 0)
    stage(lat_keys(n_chunks - 1), sa_s, sb_s, lat_vals(n_chunks - 2))
    stage(None, sb_s, None, lat_vals(n_chunks - 1))
    for g in range(GROUP):
        o = (acc_s[g] / l_s[g]).T
        o_ref[0, :, g * HEAD_DIM:(g + 1) * HEAD_DIM] = o.astype(BF16)


def _attn_call(q, k_ctx, vt_ctx, k_lat, vt_lat):
    bsz, n_tok, _ = q.shape
    n_ctx = k_ctx.shape[1]
    tq = Q_TILE
    gw = GROUP * HEAD_DIM
    keys = lambda n: pl.BlockSpec((1, n, HEAD_DIM), lambda b, j, i: (b, 0, j))
    vals = lambda n: pl.BlockSpec((1, HEAD_DIM, n), lambda b, j, i: (b, j, 0))
    return pl.pallas_call(
        _attn_kernel,
        grid=(bsz, N_KV_HEADS, n_tok // tq),
        in_specs=[pl.BlockSpec((1, tq, gw), lambda b, j, i: (b, i, j)),
                  keys(n_ctx), vals(n_ctx), keys(n_tok), vals(n_tok)],
        out_specs=pl.BlockSpec((1, tq, gw), lambda b, j, i: (b, i, j)),
        out_shape=jax.ShapeDtypeStruct((bsz, n_tok, ATTN_WIDTH), BF16),
        scratch_shapes=[pltpu.VMEM((GROUP, KEY_CHUNK, tq), F32),
                        pltpu.VMEM((GROUP, KEY_CHUNK, tq), F32),
                        pltpu.VMEM((GROUP, HEAD_DIM, tq), F32),
                        pltpu.VMEM((GROUP, 1, tq), F32),
                        pltpu.VMEM((GROUP, 1, tq), F32)],
        compiler_params=pltpu.CompilerParams(
            dimension_semantics=("arbitrary", "arbitrary", "arbitrary"),
            vmem_limit_bytes=VMEM_LIMIT),
        name="attn",
    )(q, k_ctx, vt_ctx, k_lat, vt_lat)


def _mix_out_kernel(hf_ref, hb_ref, g_ref, at_ref, ga_ref, gb_ref, x_ref, mod_ref,
                    wl_ref, wa_ref, wo_ref, o_ref):
    lru = hf_ref[0].astype(F32) + hb_ref[0].astype(F32)
    za = (lru * g_ref[0].astype(F32)).astype(BF16)
    ya = jnp.dot(za, wl_ref[...], preferred_element_type=F32)
    yb = jnp.dot(at_ref[0], wa_ref[...], preferred_element_type=F32)
    mix = (ga_ref[0].astype(F32) * ya + gb_ref[0].astype(F32) * yb).astype(BF16)
    y = jnp.dot(mix, wo_ref[...], preferred_element_type=F32)
    o_ref[0] = x_ref[0] + mod_ref[0, :, 2 * D_MODEL:3 * D_MODEL] * y


def _mix_out_call(h_fwd, h_bwd, gg, attn, sga, sgb, x, mod, w_lru_bf, w_attn_bf, w_out_bf):
    bsz, n_tok, _ = x.shape
    tl = TOKEN_TILE
    tok = lambda width: pl.BlockSpec((1, tl, width), lambda b, l: (b, l, 0))
    return pl.pallas_call(
        _mix_out_kernel,
        grid=(bsz, n_tok // tl),
        in_specs=[tok(LRU_WIDTH), tok(LRU_WIDTH), tok(LRU_WIDTH), tok(ATTN_WIDTH), tok(D_MODEL),
                  tok(D_MODEL), tok(D_MODEL),
                  pl.BlockSpec((1, 1, mod.shape[-1]), lambda b, l: (b, 0, 0)),
                  _resident((LRU_WIDTH, D_MODEL)), _resident((ATTN_WIDTH, D_MODEL)),
                  _resident((D_MODEL, D_MODEL))],
        out_specs=tok(D_MODEL),
        out_shape=jax.ShapeDtypeStruct((bsz, n_tok, D_MODEL), F32),
        compiler_params=pltpu.CompilerParams(dimension_semantics=("arbitrary", "arbitrary"),
                                             vmem_limit_bytes=VMEM_LIMIT),
        name="mix_out",
    )(h_fwd, h_bwd, gg, attn, sga, sgb, x, mod, w_lru_bf, w_attn_bf, w_out_bf)


def _ffn_kernel(x_ref, mod_ref, nrm_ref, wi_ref, wo_ref, o_ref):
    x = x_ref[0]
    shift = mod_ref[0, :, 3 * D_MODEL:4 * D_MODEL]
    scale = mod_ref[0, :, 4 * D_MODEL:5 * D_MODEL]
    gate_out = mod_ref[0, :, 5 * D_MODEL:6 * D_MODEL]
    h = ((x * _rms_scale(x) * nrm_ref[...]) * (1.0 + scale) + shift).astype(BF16)
    acc = jnp.zeros(x.shape, F32)
    for j in range(FFN_HIDDEN // FFN_CHUNK):
        lo = j * FFN_CHUNK
        gate = jnp.dot(h, wi_ref[:, lo:lo + FFN_CHUNK], preferred_element_type=F32)
        up = jnp.dot(h, wi_ref[:, FFN_HIDDEN + lo:FFN_HIDDEN + lo + FFN_CHUNK],
                     preferred_element_type=F32)
        act = (gate * _sigmoid(gate) * up).astype(BF16)
        acc = acc + jnp.dot(act, wo_ref[lo:lo + FFN_CHUNK, :], preferred_element_type=F32)
    o_ref[0] = x + gate_out * acc


def _ffn_call(x, mod, norm_g, w_in_bf, w_out_bf):
    bsz, n_tok, _ = x.shape
    tl = TOKEN_TILE
    tok = pl.BlockSpec((1, tl, D_MODEL), lambda b, l: (b, l, 0))
    return pl.pallas_call(
        _ffn_kernel,
        grid=(bsz, n_tok // tl),
        in_specs=[tok, pl.BlockSpec((1, 1, mod.shape[-1]), lambda b, l: (b, 0, 0)),
                  _resident((1, D_MODEL)),
                  _resident((D_MODEL, 2 * FFN_HIDDEN)), _resident((FFN_HIDDEN, D_MODEL))],
        out_specs=tok,
        out_shape=jax.ShapeDtypeStruct((bsz, n_tok, D_MODEL), F32),
        compiler_params=pltpu.CompilerParams(dimension_semantics=("arbitrary", "arbitrary"),
                                             vmem_limit_bytes=VMEM_LIMIT),
        name="ffn",
    )(x, mod, norm_g.reshape(1, D_MODEL), w_in_bf, w_out_bf)


def _rope_tables(n_tok):
    rows = n_tok // GRID_W
    row = jnp.repeat(jnp.arange(rows, dtype=F32), GRID_W)
    col = jnp.tile(jnp.arange(GRID_W, dtype=F32), rows)
    inv_freq = ROPE_THETA ** (-jnp.arange(0, ROPE_AXIS_DIM, 2, dtype=F32) / ROPE_AXIS_DIM)
    ang_r = row[:, None] * inv_freq[None, :]
    ang_c = col[:, None] * inv_freq[None, :]
    cr, sr, cc, sc = (f(a) for a in (ang_r, ang_c) for f in (jnp.cos, jnp.sin))
    zero = jnp.zeros_like(sr)
    cos = jnp.concatenate([cr, cr, cc, cc], axis=-1)
    sin_hi = jnp.concatenate([-sr, zero, -sc, zero], axis=-1)
    sin_lo = jnp.concatenate([zero, sr, zero, sc], axis=-1)
    return cos, sin_hi, sin_lo


def _gate_weights(wa, wx):
    def dense(w):
        eye = jnp.eye(LRU_BLOCKS, dtype=w.dtype)
        return jnp.einsum('nij,nm->nimj', w, eye).reshape(LRU_WIDTH, LRU_WIDTH)

    da, dx = dense(wa), dense(wx)
    tiles = []
    for j, k0 in enumerate(GATE_K0):
        cols = slice(j * GATE_COLS, (j + 1) * GATE_COLS)
        tiles.append(jnp.concatenate([da[k0:k0 + GATE_K, cols], dx[k0:k0 + GATE_K, cols]], axis=1))
    return (0.5 * jnp.stack(tiles)).astype(BF16)


def kernel(x, c, ctx, c_ctx, w_mod, b_mod, norm_mix, w_in, conv_w, conv_b, lru_wa, lru_ba, lru_wx,
           lru_bx, lru_lambda, q_norm, k_norm, w_out_lru, w_out_attn, w_out, norm_ffn, w_ffn_in,
           w_ffn_out):
    bsz, n_tok, _ = x.shape
    assert w_mod.shape[0] == 1, "single trunk layer"
    tables = _rope_tables(n_tok)

    c_all = jnp.concatenate([c, c_ctx[None, :]], axis=0)
    c_all = jnp.pad(c_all, ((0, 16 - c_all.shape[0]), (0, 0)))
    mod_all = _mod_call(c_all, w_mod[0], b_mod[0])
    mod = mod_all[:bsz].reshape(bsz, 1, -1)
    mod_c = mod_all[bsz:bsz + 1].reshape(1, 1, -1)

    w_in_bf = w_in[0].astype(BF16)
    u_c, k_c, vt_c = _in_proj_call(ctx, mod_c, norm_mix[0], w_in_bf, None, k_norm[0], None, latent=False)
    u_l, gg, q, k_l, vt_l, sga, sgb = _in_proj_call(x, mod, norm_mix[0], w_in_bf, q_norm[0], k_norm[0],
                                                    tables, latent=True)

    b_gate = 0.5 * jnp.stack([lru_ba[0], lru_bx[0]], axis=1)
    zeros = jnp.zeros((bsz, LRU_WIDTH), F32)
    h_dirs = []
    for d, reverse in enumerate((False, True)):
        lru = functools.partial(_lru_call, conv_w=conv_w[0], conv_b=conv_b[0],
                                w_gate=_gate_weights(lru_wa[0, d], lru_wx[0, d]),
                                b_gate=b_gate[d], lam=lru_lambda[0, d], reverse=reverse)
        _, h_seed = lru(u_c, zeros)
        h_dir, _ = lru(u_l, h_seed)
        h_dirs.append(h_dir)

    attn = _attn_call(q, k_c, vt_c, k_l, vt_l)

    x1 = _mix_out_call(h_dirs[0], h_dirs[1], gg, attn, sga, sgb, x, mod,
                       w_out_lru[0].astype(BF16), w_out_attn[0].astype(BF16), w_out[0].astype(BF16))
    return _ffn_call(x1, mod, norm_ffn[0], w_ffn_in[0].astype(BF16), w_ffn_out[0].astype(BF16))
```

```python
import functools

import jax
import jax.numpy as jnp
import numpy as np
from jax import lax
from jax.experimental import pallas as pl
from jax.experimental.pallas import tpu as pltpu

D_MODEL = 1024
GRID_W = 64
EPS = 1e-6
LRU_WIDTH = 1280
LRU_BLOCKS = 8
LRU_BLOCK_DIM = LRU_WIDTH // LRU_BLOCKS
LRU_C = 8.0
CONV_WIDTH = 4
CONV_LEFT = 2
HEAD_DIM = 128
LANES = 128
N_HEADS = 8
N_KV_HEADS = 2
GROUP = N_HEADS // N_KV_HEADS
ATTN_WIDTH = N_HEADS * HEAD_DIM
KV_WIDTH = N_KV_HEADS * HEAD_DIM
ROPE_AXIS_DIM = HEAD_DIM // 2
ROPE_THETA = 10000.0
FFN_HIDDEN = 2816

OFF_U = 0
OFF_G = OFF_U + LRU_WIDTH
OFF_Q = OFF_G + LRU_WIDTH
OFF_K = OFF_Q + ATTN_WIDTH
OFF_V = OFF_K + KV_WIDTH
OFF_GA = OFF_V + KV_WIDTH
OFF_GB = OFF_GA + D_MODEL
IN_COLS = OFF_GB + D_MODEL

TOKEN_TILE = 512
SCAN_TILE = 64
PERM_STEPS = 32
HALO = 16
GATE_COLS = 256
GATE_K = 512
Q_TILE = 256
KEY_CHUNK = 512
MXU_TILE = 256
FFN_SPLITS = (0, 6 * MXU_TILE, FFN_HIDDEN)
SCORE_BOUND_SLACK = 1.02
MAX_FIXED_SHIFT = 60.0
VMEM_LIMIT = 56 * 1024 * 1024

F32 = jnp.float32
BF16 = jnp.bfloat16


def _gate_k_offsets():
    offs = []
    for j in range(LRU_WIDTH // GATE_COLS):
        first_row = (j * GATE_COLS // LRU_BLOCK_DIM) * LRU_BLOCK_DIM
        last_row = ((j + 1) * GATE_COLS - 1) // LRU_BLOCK_DIM * LRU_BLOCK_DIM + LRU_BLOCK_DIM
        k0 = min(first_row // LANES * LANES, LRU_WIDTH - GATE_K)
        assert k0 <= first_row and last_row <= k0 + GATE_K
        offs.append(k0)
    return tuple(offs)


GATE_K0 = _gate_k_offsets()


def _sigmoid(x):
    return 0.5 * jnp.tanh(0.5 * x) + 0.5


def _gelu_tanh(x):
    return 0.5 * x * (1.0 + jnp.tanh(np.float32(np.sqrt(2.0 / np.pi)) * (x + 0.044715 * (x * x * x))))


def _rms_scale(x):
    return lax.rsqrt(jnp.mean(x * x, axis=-1, keepdims=True) + EPS)


def _resident(shape):
    nd = len(shape)
    return pl.BlockSpec(shape, lambda *_: (0,) * nd, pipeline_mode=pl.Buffered(1))


def _mod_kernel(c_ref, w_ref, b_ref, o_ref):
    c = c_ref[...]
    s = c * _sigmoid(c)
    o_ref[...] = jnp.dot(s, w_ref[...], preferred_element_type=F32,
                         precision=lax.Precision.HIGHEST) + b_ref[...]


def _mod_call(c_all, w_mod, b_mod):
    rows = c_all.shape[0]
    n = w_mod.shape[1]
    tn = 1024
    return pl.pallas_call(
        _mod_kernel,
        grid=(n // tn,),
        in_specs=[pl.BlockSpec((rows, D_MODEL), lambda j: (0, 0)),
                  pl.BlockSpec((D_MODEL, tn), lambda j: (0, j)),
                  pl.BlockSpec((1, tn), lambda j: (0, j))],
        out_specs=pl.BlockSpec((rows, tn), lambda j: (0, j)),
        out_shape=jax.ShapeDtypeStruct((rows, n), F32),
        compiler_params=pltpu.CompilerParams(dimension_semantics=("arbitrary",),
                                             vmem_limit_bytes=VMEM_LIMIT),
        name="mod",
    )(c_all, w_mod, b_mod.reshape(1, n))


def _head_norm_rope(t, gain, tables):
    t = t * _rms_scale(t) * gain
    if tables is not None:
        cos, sin_hi, sin_lo = tables
        t = (t * cos + pltpu.roll(t, HEAD_DIM - ROPE_AXIS_DIM // 2, 1) * sin_hi
             + pltpu.roll(t, ROPE_AXIS_DIM // 2, 1) * sin_lo)
    return t


def _in_proj_kernel(*refs, latent):
    if latent:
        (x_ref, mod_ref, nrm_ref, w_ref, qn_ref, kn_ref, rc_ref, rh_ref, rl_ref,
         u_ref, g_ref, q_ref, k_ref, vt_ref, ga_ref, gb_ref) = refs
        tables = (rc_ref[...], rh_ref[...], rl_ref[...])
    else:
        x_ref, mod_ref, nrm_ref, w_ref, kn_ref, u_ref, k_ref, vt_ref = refs
        tables = None
    x = x_ref[0]
    shift = mod_ref[0, :, 0:D_MODEL]
    scale = mod_ref[0, :, D_MODEL:2 * D_MODEL]
    h = (x * _rms_scale(x) * nrm_ref[...]) * (1.0 + scale) + shift
    h = h.astype(BF16)

    def proj(off, width):
        return jnp.dot(h, w_ref[:, off:off + width], preferred_element_type=F32)

    u_ref[0] = proj(OFF_U, LRU_WIDTH).astype(BF16)
    kk = proj(OFF_K, KV_WIDTH)
    for j in range(N_KV_HEADS):
        sl = slice(j * HEAD_DIM, (j + 1) * HEAD_DIM)
        k_ref[0, :, sl] = _head_norm_rope(kk[:, sl], kn_ref[...], tables).astype(BF16)
    vt_ref[0] = proj(OFF_V, KV_WIDTH).T.astype(BF16)
    if latent:
        g_ref[0] = _gelu_tanh(proj(OFF_G, LRU_WIDTH)).astype(BF16)
        qq = proj(OFF_Q, ATTN_WIDTH)
        qscale = np.float32(np.log2(np.e) / np.sqrt(HEAD_DIM))
        for j in range(N_HEADS):
            sl = slice(j * HEAD_DIM, (j + 1) * HEAD_DIM)
            q_ref[0, :, sl] = (_head_norm_rope(qq[:, sl], qn_ref[...], tables) * qscale).astype(BF16)
        ga_ref[0] = _sigmoid(proj(OFF_GA, D_MODEL)).astype(BF16)
        gb_ref[0] = _sigmoid(proj(OFF_GB, D_MODEL)).astype(BF16)


def _in_proj_call(x, mod, norm_g, w_in_bf, q_gain, k_gain, tables, latent):
    bsz, n_tok, _ = x.shape
    tl = min(TOKEN_TILE, n_tok)
    grid = (bsz, n_tok // tl)
    mod_rows = mod.shape[0]
    mod_map = (lambda b, l: (b, 0, 0)) if mod_rows > 1 else (lambda b, l: (0, 0, 0))
    tok = lambda width: pl.BlockSpec((1, tl, width), lambda b, l: (b, l, 0))
    in_specs = [tok(D_MODEL),
                pl.BlockSpec((1, 1, mod.shape[-1]), mod_map),
                _resident((1, D_MODEL)),
                _resident((D_MODEL, IN_COLS))]
    args = [x, mod, norm_g.reshape(1, D_MODEL), w_in_bf]
    vt_spec = pl.BlockSpec((1, KV_WIDTH, tl), lambda b, l: (b, 0, l))
    vt_shape = jax.ShapeDtypeStruct((bsz, KV_WIDTH, n_tok), BF16)
    bf = lambda width: jax.ShapeDtypeStruct((bsz, n_tok, width), BF16)
    if latent:
        in_specs += [_resident((1, HEAD_DIM)), _resident((1, HEAD_DIM))]
        in_specs += [pl.BlockSpec((tl, HEAD_DIM), lambda b, l: (l, 0))] * 3
        args += [q_gain.reshape(1, HEAD_DIM), k_gain.reshape(1, HEAD_DIM), *tables]
        out_specs = [tok(LRU_WIDTH), tok(LRU_WIDTH), tok(ATTN_WIDTH), tok(KV_WIDTH), vt_spec,
                     tok(D_MODEL), tok(D_MODEL)]
        out_shape = [bf(LRU_WIDTH), bf(LRU_WIDTH), bf(ATTN_WIDTH), bf(KV_WIDTH), vt_shape,
                     bf(D_MODEL), bf(D_MODEL)]
    else:
        in_specs += [_resident((1, HEAD_DIM))]
        args += [k_gain.reshape(1, HEAD_DIM)]
        out_specs = [tok(LRU_WIDTH), tok(KV_WIDTH), vt_spec]
        out_shape = [bf(LRU_WIDTH), bf(KV_WIDTH), vt_shape]
    return pl.pallas_call(
        functools.partial(_in_proj_kernel, latent=latent),
        grid=grid, in_specs=in_specs, out_specs=out_specs, out_shape=out_shape,
        compiler_params=pltpu.CompilerParams(dimension_semantics=("arbitrary", "arbitrary"),
                                             vmem_limit_bytes=VMEM_LIMIT),
        name="in_proj_latent" if latent else "in_proj_ctx",
    )(*args)


def _lru_kernel(u_ref, prev_ref, next_ref, h0_ref, cw_ref, cb_ref, wg_ref, bg_ref, lam_ref,
                perm_ref, permt_ref, halo_ref, h_ref, hlast_ref, ext_s, a_s, b_s, carry_s,
                *, n_tiles, reverse):
    t = pl.program_id(0)
    tile = (n_tiles - 1 - t) if reverse else t
    bsz, ts, _ = u_ref.shape

    @pl.when(t == 0)
    def _():
        carry_s[...] = h0_ref[...]

    for p in range(ts // PERM_STEPS):
        steps = slice(p * PERM_STEPS, (p + 1) * PERM_STEPS)
        u_bm = u_ref[:, steps, :].reshape(bsz * PERM_STEPS, LRU_WIDTH)
        u_tm = jnp.dot(perm_ref[...], u_bm, preferred_element_type=F32)
        ext_s[CONV_LEFT + p * PERM_STEPS:CONV_LEFT + (p + 1) * PERM_STEPS] = (
            u_tm.reshape(PERM_STEPS, bsz, LRU_WIDTH))
    edge = jnp.concatenate([prev_ref[:, HALO // 2:, :].reshape(bsz * HALO // 2, LRU_WIDTH),
                            next_ref[:, :HALO // 2, :].reshape(bsz * HALO // 2, LRU_WIDTH)], axis=0)
    halo = jnp.dot(halo_ref[...], edge, preferred_element_type=F32)
    ext_s[0:CONV_LEFT] = (halo[0:CONV_LEFT * bsz]
                          * jnp.where(tile > 0, 1.0, 0.0)).reshape(CONV_LEFT, bsz, LRU_WIDTH)
    ext_s[CONV_LEFT + ts:] = (halo[CONV_LEFT * bsz:(CONV_LEFT + 1) * bsz]
                              * jnp.where(tile < n_tiles - 1, 1.0, 0.0)).reshape(1, bsz, LRU_WIDTH)

    xc = cb_ref[...][None]
    for j in range(CONV_WIDTH):
        xc = xc + ext_s[j:j + ts] * cw_ref[j:j + 1, :][None]
    xc2 = xc.reshape(ts * bsz, LRU_WIDTH)
    xb = xc2.astype(BF16)

    sp = jnp.maximum(-lam_ref[...], 0.0) + jnp.log1p(jnp.exp(-jnp.abs(lam_ref[...])))
    decay = sp * np.float32(-0.5 * LRU_C * np.log2(np.e))
    xh = 0.5 * xc2
    for j, k0 in enumerate(GATE_K0):
        cols = slice(j * GATE_COLS, (j + 1) * GATE_COLS)
        gates = jnp.dot(xb[:, k0:k0 + GATE_K], wg_ref[j], preferred_element_type=F32)
        t_r = jnp.tanh(gates[:, :GATE_COLS] + bg_ref[0:1, cols])
        t_i = jnp.tanh(gates[:, GATE_COLS:] + bg_ref[1:2, cols])
        a = jnp.exp2(decay[:, cols] * t_r + decay[:, cols])
        om = 1.0 - a * a
        coef = om * lax.rsqrt(jnp.maximum(om, 1e-30))
        a_s[:, :, cols] = a.reshape(ts, bsz, GATE_COLS)
        b_s[:, :, cols] = (coef * ((t_i + 1.0) * xh[:, cols])).reshape(ts, bsz, GATE_COLS)

    def step(s, h):
        idx = (ts - 1 - s) if reverse else s
        h = a_s[idx] * h + b_s[idx]
        b_s[idx] = h
        return h

    h = lax.fori_loop(0, ts, step, carry_s[...], unroll=8)
    carry_s[...] = h
    for p in range(ts // PERM_STEPS):
        steps = slice(p * PERM_STEPS, (p + 1) * PERM_STEPS)
        h_tm = b_s[steps].reshape(PERM_STEPS * bsz, LRU_WIDTH).astype(BF16)
        h_bm = jnp.dot(permt_ref[...], h_tm, preferred_element_type=F32)
        h_ref[:, steps, :] = h_bm.reshape(bsz, PERM_STEPS, LRU_WIDTH).astype(BF16)

    @pl.when(t == n_tiles - 1)
    def _():
        hlast_ref[...] = h


def _perm_matrices(bsz):
    n = bsz * PERM_STEPS
    r_out = np.arange(n)
    s, b = r_out // bsz, r_out % bsz
    perm = np.zeros((n, n), np.float32)
    perm[r_out, b * PERM_STEPS + s] = 1.0
    half = HALO // 2
    halo = np.zeros((4 * bsz, 2 * bsz * half), np.float32)
    for b_ in range(bsz):
        for k in range(CONV_LEFT):
            halo[k * bsz + b_, b_ * half + half - CONV_LEFT + k] = 1.0
        halo[CONV_LEFT * bsz + b_, bsz * half + b_ * half] = 1.0
    return jnp.asarray(perm, BF16), jnp.asarray(perm.T, BF16), jnp.asarray(halo, BF16)


def _lru_call(u, h0, conv_w, conv_b, w_gate, b_gate, lam, reverse):
    bsz, n_tok, _ = u.shape
    ts = SCAN_TILE
    n_tiles = n_tok // ts
    pos = (lambda t: n_tiles - 1 - t) if reverse else (lambda t: t)
    per = ts // HALO
    perm, perm_t, halo = _perm_matrices(bsz)
    in_specs = [
        pl.BlockSpec((bsz, ts, LRU_WIDTH), lambda t: (0, pos(t), 0)),
        pl.BlockSpec((bsz, HALO, LRU_WIDTH), lambda t: (0, jnp.maximum(pos(t) * per - 1, 0), 0)),
        pl.BlockSpec((bsz, HALO, LRU_WIDTH),
                     lambda t: (0, jnp.minimum((pos(t) + 1) * per, n_tok // HALO - 1), 0)),
        _resident((bsz, LRU_WIDTH)),
        _resident((CONV_WIDTH, LRU_WIDTH)),
        _resident((1, LRU_WIDTH)),
        _resident((len(GATE_K0), GATE_K, 2 * GATE_COLS)),
        _resident((2, LRU_WIDTH)),
        _resident((1, LRU_WIDTH)),
        _resident(perm.shape), _resident(perm_t.shape), _resident(halo.shape),
    ]
    out_specs = [pl.BlockSpec((bsz, ts, LRU_WIDTH), lambda t: (0, pos(t), 0)),
                 pl.BlockSpec((bsz, LRU_WIDTH), lambda t: (0, 0))]
    out_shape = [jax.ShapeDtypeStruct((bsz, n_tok, LRU_WIDTH), BF16),
                 jax.ShapeDtypeStruct((bsz, LRU_WIDTH), F32)]
    return pl.pallas_call(
        functools.partial(_lru_kernel, n_tiles=n_tiles, reverse=reverse),
        grid=(n_tiles,), in_specs=in_specs, out_specs=out_specs, out_shape=out_shape,
        scratch_shapes=[pltpu.VMEM((ts + CONV_WIDTH - 1, bsz, LRU_WIDTH), F32),
                        pltpu.VMEM((ts, bsz, LRU_WIDTH), F32),
                        pltpu.VMEM((ts, bsz, LRU_WIDTH), F32),
                        pltpu.VMEM((bsz, LRU_WIDTH), F32)],
        compiler_params=pltpu.CompilerParams(dimension_semantics=("arbitrary",),
                                             vmem_limit_bytes=VMEM_LIMIT),
        name="lru_bwd" if reverse else "lru_fwd",
    )(u, u, u, h0, conv_w, conv_b.reshape(1, LRU_WIDTH), w_gate, b_gate, lam.reshape(1, LRU_WIDTH),
      perm, perm_t, halo)


def _attn_kernel(q_ref, kc_ref, vc_ref, kl_ref, vl_ref, o_ref, sa_s, sb_s, acc_s, m_s, l_s):
    n_chunks = kl_ref.shape[1] // KEY_CHUNK

    def scores(k, g):
        qg = q_ref[0, :, g * HEAD_DIM:(g + 1) * HEAD_DIM]
        return lax.dot_general(k, qg, (((1,), (1,)), ((), ())), preferred_element_type=F32)

    def softmax_pv(st, vt, g, first):
        m_c = jnp.max(st, axis=0, keepdims=True)
        if first:
            m_new = m_c
        else:
            m_old = m_s[g]
            m_new = jnp.maximum(m_old, m_c)
            alpha = jnp.exp2(m_old - m_new)
        p = jnp.exp2(st - m_new)
        p_sum = jnp.sum(p, axis=0, keepdims=True)
        pv = jnp.dot(vt, p.astype(BF16), preferred_element_type=F32)
        if first:
            l_s[g] = p_sum
            acc_s[g] = pv
        else:
            l_s[g] = alpha * l_s[g] + p_sum
            acc_s[g] = alpha * acc_s[g] + pv
        m_s[g] = m_new

    def lat_keys(c):
        return kl_ref[0, pl.ds(pl.multiple_of(c * KEY_CHUNK, KEY_CHUNK), KEY_CHUNK), :]

    def lat_vals(c):
        return vl_ref[0, :, pl.ds(pl.multiple_of(c * KEY_CHUNK, KEY_CHUNK), KEY_CHUNK)]

    def stage(k_next, st_in, st_out, vt, first=False):
        for g in range(GROUP):
            if k_next is not None:
                st_out[g] = scores(k_next, g)
            softmax_pv(st_in[g], vt, g, first)

    k0 = kl_ref[0, 0:KEY_CHUNK, :]
    for g in range(GROUP):
        sc = scores(kc_ref[0], g)
        sa_s[g] = scores(k0, g)
        softmax_pv(sc, vc_ref[0], g, True)

    def body(i, carry):
        stage(lat_keys(2 * i + 1), sa_s, sb_s, lat_vals(2 * i))
        stage(lat_keys(2 * i + 2), sb_s, sa_s, lat_vals(2 * i + 1))
        return carry

    lax.fori_loop(0, n_chunks // 2 - 1, body, 0)
    stage(lat_keys(n_chunks - 1), sa_s, sb_s, lat_vals(n_chunks - 2))
    stage(None, sb_s, None, lat_vals(n_chunks - 1))
    for g in range(GROUP):
        o = (acc_s[g] / l_s[g]).T
        o_ref[0, :, g * HEAD_DIM:(g + 1) * HEAD_DIM] = o.astype(BF16)


def _attn_call(q, k_ctx, vt_ctx, k_lat, vt_lat):
    bsz, n_tok, _ = q.shape
    n_ctx = k_ctx.shape[1]
    tq = Q_TILE
    gw = GROUP * HEAD_DIM
    keys = lambda n: pl.BlockSpec((1, n, HEAD_DIM), lambda b, j, i: (b, 0, j))
    vals = lambda n: pl.BlockSpec((1, HEAD_DIM, n), lambda b, j, i: (b, j, 0))
    return pl.pallas_call(
        _attn_kernel,
        grid=(bsz, N_KV_HEADS, n_tok // tq),
        in_specs=[pl.BlockSpec((1, tq, gw), lambda b, j, i: (b, i, j)),
                  keys(n_ctx), vals(n_ctx), keys(n_tok), vals(n_tok)],
        out_specs=pl.BlockSpec((1, tq, gw), lambda b, j, i: (b, i, j)),
        out_shape=jax.ShapeDtypeStruct((bsz, n_tok, ATTN_WIDTH), BF16),
        scratch_shapes=[pltpu.VMEM((GROUP, KEY_CHUNK, tq), F32),
                        pltpu.VMEM((GROUP, KEY_CHUNK, tq), F32),
                        pltpu.VMEM((GROUP, HEAD_DIM, tq), F32),
                        pltpu.VMEM((GROUP, 1, tq), F32),
                        pltpu.VMEM((GROUP, 1, tq), F32)],
        compiler_params=pltpu.CompilerParams(
            dimension_semantics=("arbitrary", "arbitrary", "arbitrary"),
            vmem_limit_bytes=VMEM_LIMIT),
        name="attn",
    )(q, k_ctx, vt_ctx, k_lat, vt_lat)


def _attn_fixed_shift_kernel(shift_ref, q_ref, kc_ref, vc_ref, kl_ref, vl_ref, o_ref,
                             sa_s, sb_s, acc_s, l_s):
    n_chunks = kl_ref.shape[1] // KEY_CHUNK
    tq = q_ref.shape[1]
    shift = shift_ref[0]
    q_all = jnp.concatenate([q_ref[0, :, g * HEAD_DIM:(g + 1) * HEAD_DIM] for g in range(GROUP)],
                            axis=0)

    def scores(k):
        return lax.dot_general(k, q_all, (((1,), (1,)), ((), ())), preferred_element_type=F32)

    def consume(st, vt, first=False):
        p = jnp.exp2(st - shift)
        p_sum = jnp.sum(p, axis=0, keepdims=True)
        pv = jnp.dot(vt, p.astype(BF16), preferred_element_type=F32)
        if first:
            l_s[...] = p_sum
            acc_s[...] = pv
        else:
            l_s[...] = l_s[...] + p_sum
            acc_s[...] = acc_s[...] + pv

    sc = scores(kc_ref[0])
    sa_s[...] = scores(kl_ref[0, 0:KEY_CHUNK, :])
    consume(sc, vc_ref[0], True)
    bufs = (sa_s, sb_s)
    for c in range(n_chunks):
        if c + 1 < n_chunks:
            bufs[(c + 1) % 2][...] = scores(kl_ref[0, (c + 1) * KEY_CHUNK:(c + 2) * KEY_CHUNK, :])
        consume(bufs[c % 2][...], vl_ref[0, :, c * KEY_CHUNK:(c + 1) * KEY_CHUNK])
    o_t = acc_s[...] / l_s[...]
    for g in range(GROUP):
        o_ref[0, :, g * HEAD_DIM:(g + 1) * HEAD_DIM] = o_t[:, g * tq:(g + 1) * tq].T.astype(BF16)


def _attn_fixed_shift_call(shift, q, k_ctx, vt_ctx, k_lat, vt_lat):
    bsz, n_tok, _ = q.shape
    n_ctx = k_ctx.shape[1]
    tq = Q_TILE
    gw = GROUP * HEAD_DIM
    keys = lambda n: pl.BlockSpec((1, n, HEAD_DIM), lambda b, j, i: (b, 0, j))
    vals = lambda n: pl.BlockSpec((1, HEAD_DIM, n), lambda b, j, i: (b, j, 0))
    return pl.pallas_call(
        _attn_fixed_shift_kernel,
        grid=(bsz, N_KV_HEADS, n_tok // tq),
        in_specs=[pl.BlockSpec(memory_space=pltpu.SMEM),
                  pl.BlockSpec((1, tq, gw), lambda b, j, i: (b, i, j)),
                  keys(n_ctx), vals(n_ctx), keys(n_tok), vals(n_tok)],
        out_specs=pl.BlockSpec((1, tq, gw), lambda b, j, i: (b, i, j)),
        out_shape=jax.ShapeDtypeStruct((bsz, n_tok, ATTN_WIDTH), BF16),
        scratch_shapes=[pltpu.VMEM((KEY_CHUNK, GROUP * tq), F32),
                        pltpu.VMEM((KEY_CHUNK, GROUP * tq), F32),
                        pltpu.VMEM((HEAD_DIM, GROUP * tq), F32),
                        pltpu.VMEM((1, GROUP * tq), F32)],
        compiler_params=pltpu.CompilerParams(
            dimension_semantics=("arbitrary", "arbitrary", "arbitrary"),
            vmem_limit_bytes=VMEM_LIMIT),
        name="attn_fixed_shift",
    )(shift, q, k_ctx, vt_ctx, k_lat, vt_lat)


def _attention(q, k_ctx, vt_ctx, k_lat, vt_lat, q_gain, k_gain):
    bound = (SCORE_BOUND_SLACK * np.float32(np.sqrt(HEAD_DIM) * np.log2(np.e))
             * jnp.max(jnp.abs(q_gain)) * jnp.max(jnp.abs(k_gain)))
    operands = (q, k_ctx, vt_ctx, k_lat, vt_lat)
    return lax.cond(bound <= MAX_FIXED_SHIFT,
                    lambda ops: _attn_fixed_shift_call(bound.reshape(1), *ops),
                    lambda ops: _attn_call(*ops), operands)


def _mix_out_kernel(hf_ref, hb_ref, g_ref, at_ref, ga_ref, gb_ref, x_ref, mod_ref,
                    wl_ref, wa_ref, wo_ref, o_ref):
    lru = hf_ref[0].astype(F32) + hb_ref[0].astype(F32)
    za = (lru * g_ref[0].astype(F32)).astype(BF16)
    ya = jnp.dot(za, wl_ref[...], preferred_element_type=F32)
    yb = jnp.dot(at_ref[0], wa_ref[...], preferred_element_type=F32)
    mix = (ga_ref[0].astype(F32) * ya + gb_ref[0].astype(F32) * yb).astype(BF16)
    y = jnp.dot(mix, wo_ref[...], preferred_element_type=F32)
    o_ref[0] = x_ref[0] + mod_ref[0, :, 2 * D_MODEL:3 * D_MODEL] * y


def _mix_out_call(h_fwd, h_bwd, gg, attn, sga, sgb, x, mod, w_lru_bf, w_attn_bf, w_out_bf):
    bsz, n_tok, _ = x.shape
    tl = TOKEN_TILE
    tok = lambda width: pl.BlockSpec((1, tl, width), lambda b, l: (b, l, 0))
    return pl.pallas_call(
        _mix_out_kernel,
        grid=(bsz, n_tok // tl),
        in_specs=[tok(LRU_WIDTH), tok(LRU_WIDTH), tok(LRU_WIDTH), tok(ATTN_WIDTH), tok(D_MODEL),
                  tok(D_MODEL), tok(D_MODEL),
                  pl.BlockSpec((1, 1, mod.shape[-1]), lambda b, l: (b, 0, 0)),
                  _resident((LRU_WIDTH, D_MODEL)), _resident((ATTN_WIDTH, D_MODEL)),
                  _resident((D_MODEL, D_MODEL))],
        out_specs=tok(D_MODEL),
        out_shape=jax.ShapeDtypeStruct((bsz, n_tok, D_MODEL), F32),
        compiler_params=pltpu.CompilerParams(dimension_semantics=("arbitrary", "arbitrary"),
                                             vmem_limit_bytes=VMEM_LIMIT),
        name="mix_out",
    )(h_fwd, h_bwd, gg, attn, sga, sgb, x, mod, w_lru_bf, w_attn_bf, w_out_bf)


def _ffn_kernel(x_ref, mod_ref, nrm_ref, wi_ref, wo_ref, o_ref):
    x = x_ref[0]
    shift = mod_ref[0, :, 3 * D_MODEL:4 * D_MODEL]
    scale = mod_ref[0, :, 4 * D_MODEL:5 * D_MODEL]
    gate_out = mod_ref[0, :, 5 * D_MODEL:6 * D_MODEL]
    h = ((x * _rms_scale(x) * nrm_ref[...]) * (1.0 + scale) + shift).astype(BF16)
    acc = jnp.zeros(x.shape, F32)
    for lo, hi in zip(FFN_SPLITS[:-1], FFN_SPLITS[1:]):
        gate = jnp.dot(h, wi_ref[:, lo:hi], preferred_element_type=F32)
        up = jnp.dot(h, wi_ref[:, FFN_HIDDEN + lo:FFN_HIDDEN + hi], preferred_element_type=F32)
        act = (gate * _sigmoid(gate) * up).astype(BF16)
        acc = acc + jnp.dot(act, wo_ref[lo:hi, :], preferred_element_type=F32)
    o_ref[0] = x + gate_out * acc


def _ffn_call(x, mod, norm_g, w_in_bf, w_out_bf):
    bsz, n_tok, _ = x.shape
    tl = TOKEN_TILE
    tok = pl.BlockSpec((1, tl, D_MODEL), lambda b, l: (b, l, 0))
    return pl.pallas_call(
        _ffn_kernel,
        grid=(bsz, n_tok // tl),
        in_specs=[tok, pl.BlockSpec((1, 1, mod.shape[-1]), lambda b, l: (b, 0, 0)),
                  _resident((1, D_MODEL)),
                  _resident((D_MODEL, 2 * FFN_HIDDEN)), _resident((FFN_HIDDEN, D_MODEL))],
        out_specs=tok,
        out_shape=jax.ShapeDtypeStruct((bsz, n_tok, D_MODEL), F32),
        compiler_params=pltpu.CompilerParams(dimension_semantics=("arbitrary", "arbitrary"),
                                             vmem_limit_bytes=VMEM_LIMIT),
        name="ffn",
    )(x, mod, norm_g.reshape(1, D_MODEL), w_in_bf, w_out_bf)


def _rope_tables(n_tok):
    pos = np.arange(n_tok)
    inv_freq = ROPE_THETA ** (-np.arange(0, ROPE_AXIS_DIM, 2, dtype=np.float64) / ROPE_AXIS_DIM)
    ang_r = (pos // GRID_W)[:, None] * inv_freq[None, :]
    ang_c = (pos % GRID_W)[:, None] * inv_freq[None, :]
    cr, sr, cc, sc = (f(a) for a in (ang_r, ang_c) for f in (np.cos, np.sin))
    zero = np.zeros_like(sr)
    cos = np.concatenate([cr, cr, cc, cc], axis=-1)
    sin_hi = np.concatenate([-sr, zero, -sc, zero], axis=-1)
    sin_lo = np.concatenate([zero, sr, zero, sc], axis=-1)
    return tuple(jnp.asarray(t, F32) for t in (cos, sin_hi, sin_lo))


def _gate_weights(wa, wx):
    def dense(w):
        rows = [jnp.pad(w[n], ((0, 0), (n * LRU_BLOCK_DIM, LRU_WIDTH - (n + 1) * LRU_BLOCK_DIM)))
                for n in range(LRU_BLOCKS)]
        return jnp.concatenate(rows, axis=0)

    da, dx = dense(wa), dense(wx)
    tiles = []
    for j, k0 in enumerate(GATE_K0):
        cols = slice(j * GATE_COLS, (j + 1) * GATE_COLS)
        tiles.append(jnp.concatenate([da[k0:k0 + GATE_K, cols], dx[k0:k0 + GATE_K, cols]], axis=1))
    return (0.5 * jnp.stack(tiles)).astype(BF16)


def kernel(x, c, ctx, c_ctx, w_mod, b_mod, norm_mix, w_in, conv_w, conv_b, lru_wa, lru_ba, lru_wx,
           lru_bx, lru_lambda, q_norm, k_norm, w_out_lru, w_out_attn, w_out, norm_ffn, w_ffn_in,
           w_ffn_out):
    bsz, n_tok, _ = x.shape
    assert w_mod.shape[0] == 1, "single trunk layer"
    tables = _rope_tables(n_tok)

    c_all = jnp.concatenate([c, c_ctx[None, :]], axis=0)
    c_all = jnp.pad(c_all, ((0, 16 - c_all.shape[0]), (0, 0)))
    mod_all = _mod_call(c_all, w_mod[0], b_mod[0])
    mod = mod_all[:bsz].reshape(bsz, 1, -1)
    mod_c = mod_all[bsz:bsz + 1].reshape(1, 1, -1)

    w_in_bf = w_in[0].astype(BF16)
    u_c, k_c, vt_c = _in_proj_call(ctx, mod_c, norm_mix[0], w_in_bf, None, k_norm[0], None, latent=False)
    u_l, gg, q, k_l, vt_l, sga, sgb = _in_proj_call(x, mod, norm_mix[0], w_in_bf, q_norm[0], k_norm[0],
                                                    tables, latent=True)

    b_gate = 0.5 * jnp.stack([lru_ba[0], lru_bx[0]], axis=1)
    zeros = jnp.zeros((bsz, LRU_WIDTH), F32)
    h_dirs = []
    for d, reverse in enumerate((False, True)):
        lru = functools.partial(_lru_call, conv_w=conv_w[0], conv_b=conv_b[0],
                                w_gate=_gate_weights(lru_wa[0, d], lru_wx[0, d]),
                                b_gate=b_gate[d], lam=lru_lambda[0, d], reverse=reverse)
        _, h_seed = lru(u_c, zeros)
        h_dir, _ = lru(u_l, h_seed)
        h_dirs.append(h_dir)

    attn = _attention(q, k_c, vt_c, k_l, vt_l, q_norm[0], k_norm[0])

    x1 = _mix_out_call(h_dirs[0], h_dirs[1], gg, attn, sga, sgb, x, mod,
                       w_out_lru[0].astype(BF16), w_out_attn[0].astype(BF16), w_out[0].astype(BF16))
    return _ffn_call(x1, mod, norm_ffn[0], w_ffn_in[0].astype(BF16), w_ffn_out[0].astype(BF16))
```

```python
import functools

import jax
import jax.numpy as jnp
import numpy as np
from jax import lax
from jax.experimental import pallas as pl
from jax.experimental.pallas import tpu as pltpu

D_MODEL = 1024
GRID_W = 64
EPS = 1e-6
LRU_WIDTH = 1280
LRU_BLOCKS = 8
LRU_BLOCK_DIM = LRU_WIDTH // LRU_BLOCKS
LRU_C = 8.0
CONV_WIDTH = 4
CONV_LEFT = 2
HEAD_DIM = 128
LANES = 128
N_HEADS = 8
N_KV_HEADS = 2
GROUP = N_HEADS // N_KV_HEADS
ATTN_WIDTH = N_HEADS * HEAD_DIM
KV_WIDTH = N_KV_HEADS * HEAD_DIM
ROPE_AXIS_DIM = HEAD_DIM // 2
ROPE_THETA = 10000.0
FFN_HIDDEN = 2816

OFF_U = 0
OFF_G = OFF_U + LRU_WIDTH
OFF_Q = OFF_G + LRU_WIDTH
OFF_K = OFF_Q + ATTN_WIDTH
OFF_V = OFF_K + KV_WIDTH
OFF_GA = OFF_V + KV_WIDTH
OFF_GB = OFF_GA + D_MODEL
IN_COLS = OFF_GB + D_MODEL

TOKEN_TILE = 512
SCAN_TILE = 64
PERM_STEPS = 32
HALO = 16
GATE_COLS = 256
GATE_K = 512
Q_TILE = 256
KEY_CHUNK = 512
MXU_TILE = 256
FFN_SPLITS = (0, 6 * MXU_TILE, FFN_HIDDEN)
SCORE_BOUND_SLACK = 1.02
MAX_FIXED_SHIFT = 60.0
VMEM_LIMIT = 56 * 1024 * 1024

F32 = jnp.float32
BF16 = jnp.bfloat16


def _gate_k_offsets():
    offs = []
    for j in range(LRU_WIDTH // GATE_COLS):
        first_row = (j * GATE_COLS // LRU_BLOCK_DIM) * LRU_BLOCK_DIM
        last_row = ((j + 1) * GATE_COLS - 1) // LRU_BLOCK_DIM * LRU_BLOCK_DIM + LRU_BLOCK_DIM
        k0 = min(first_row // LANES * LANES, LRU_WIDTH - GATE_K)
        assert k0 <= first_row and last_row <= k0 + GATE_K
        offs.append(k0)
    return tuple(offs)


GATE_K0 = _gate_k_offsets()


def _sigmoid(x):
    return 0.5 * jnp.tanh(0.5 * x) + 0.5


def _gelu_tanh(x):
    c = np.float32(np.sqrt(2.0 / np.pi))
    inner = x * (c + np.float32(0.044715) * c * (x * x))
    return (0.5 * x) * (1.0 + jnp.tanh(inner))


def _rms_scale(x):
    return lax.rsqrt(jnp.mean(x * x, axis=-1, keepdims=True) + EPS)


def _resident(shape):
    nd = len(shape)
    return pl.BlockSpec(shape, lambda *_: (0,) * nd, pipeline_mode=pl.Buffered(1))


def _mod_kernel(c_ref, w_ref, b_ref, o_ref):
    c = c_ref[...]
    s = c * _sigmoid(c)
    o_ref[...] = jnp.dot(s, w_ref[...], preferred_element_type=F32,
                         precision=lax.Precision.HIGHEST) + b_ref[...]


def _mod_call(c_all, w_mod, b_mod):
    rows = c_all.shape[0]
    n = w_mod.shape[1]
    tn = 1024
    return pl.pallas_call(
        _mod_kernel,
        grid=(n // tn,),
        in_specs=[pl.BlockSpec((rows, D_MODEL), lambda j: (0, 0)),
                  pl.BlockSpec((D_MODEL, tn), lambda j: (0, j)),
                  pl.BlockSpec((1, tn), lambda j: (0, j))],
        out_specs=pl.BlockSpec((rows, tn), lambda j: (0, j)),
        out_shape=jax.ShapeDtypeStruct((rows, n), F32),
        compiler_params=pltpu.CompilerParams(dimension_semantics=("arbitrary",),
                                             vmem_limit_bytes=VMEM_LIMIT),
        name="mod",
    )(c_all, w_mod, b_mod.reshape(1, n))


def _gained_tables(gain, tables, scale):
    half = ROPE_AXIS_DIM // 2
    if tables is None:
        return (gain * scale,)
    cos, sin_hi, sin_lo = tables
    return (cos * (gain * scale), sin_hi * (pltpu.roll(gain, HEAD_DIM - half, 1) * scale),
            sin_lo * (pltpu.roll(gain, half, 1) * scale))


def _head_norm_rope(t, gained):
    half = ROPE_AXIS_DIM // 2
    n = t * _rms_scale(t)
    if len(gained) == 1:
        return n * gained[0]
    cos, sin_hi, sin_lo = gained
    return (n * cos + pltpu.roll(n, HEAD_DIM - half, 1) * sin_hi + pltpu.roll(n, half, 1) * sin_lo)


def _in_proj_kernel(*refs, latent, n_row_tiles, n_steps):
    if latent:
        (x0_ref, xn_ref, mod_ref, nrm_ref, w_ref, qn_ref, kn_ref, rc_ref, rh_ref, rl_ref,
         u_ref, g_ref, q_ref, k_ref, vt_ref, ga_ref, gb_ref, h_s) = refs
        tables = (rc_ref[...], rh_ref[...], rl_ref[...])
    else:
        x0_ref, xn_ref, mod_ref, nrm_ref, w_ref, kn_ref, u_ref, k_ref, vt_ref, h_s = refs
        tables = None
    step = pl.program_id(0) * n_row_tiles + pl.program_id(1)

    def normed(x, b):
        shift = mod_ref[b, :, 0:D_MODEL]
        scale = mod_ref[b, :, D_MODEL:2 * D_MODEL]
        return ((x * _rms_scale(x) * nrm_ref[...]) * (1.0 + scale) + shift).astype(BF16)

    @pl.when(step == 0)
    def _():
        h_s[...] = normed(x0_ref[0], 0)

    h = h_s[...]

    def proj(off, width):
        return jnp.dot(h, w_ref[:, off:off + width], preferred_element_type=F32)

    if latent:
        qq = proj(OFF_Q, ATTN_WIDTH)
        q_tables = _gained_tables(qn_ref[...], tables, np.float32(np.log2(np.e) / np.sqrt(HEAD_DIM)))
        for j in range(N_HEADS):
            sl = slice(j * HEAD_DIM, (j + 1) * HEAD_DIM)
            q_ref[0, :, sl] = _head_norm_rope(qq[:, sl], q_tables).astype(BF16)
    kk = proj(OFF_K, KV_WIDTH)
    k_tables = _gained_tables(kn_ref[...], tables, np.float32(1.0))
    for j in range(N_KV_HEADS):
        sl = slice(j * HEAD_DIM, (j + 1) * HEAD_DIM)
        k_ref[0, :, sl] = _head_norm_rope(kk[:, sl], k_tables).astype(BF16)
    if latent:
        g_ref[0] = _gelu_tanh(proj(OFF_G, LRU_WIDTH)).astype(BF16)
        ga_ref[0] = (0.5 * jnp.tanh(proj(OFF_GA, D_MODEL)) + 0.5).astype(BF16)
        gb_ref[0] = (0.5 * jnp.tanh(proj(OFF_GB, D_MODEL)) + 0.5).astype(BF16)
    vt_ref[0] = proj(OFF_V, KV_WIDTH).T.astype(BF16)
    u_ref[0] = proj(OFF_U, LRU_WIDTH).astype(BF16)

    nxt = jnp.minimum(step + 1, n_steps - 1)
    b_next = nxt // n_row_tiles if mod_ref.shape[0] > 1 else 0
    h_s[...] = normed(xn_ref[0], b_next)


def _in_proj_call(x, mod, norm_g, w_in_bf, q_gain, k_gain, tables, latent):
    bsz, n_tok, _ = x.shape
    tl = min(TOKEN_TILE, n_tok)
    n_row_tiles = n_tok // tl
    grid = (bsz, n_row_tiles)
    n_steps = bsz * n_row_tiles

    def next_tile(b, l):
        nxt = jnp.minimum(b * n_row_tiles + l + 1, n_steps - 1)
        return (nxt // n_row_tiles, nxt % n_row_tiles, 0)

    tok = lambda width: pl.BlockSpec((1, tl, width), lambda b, l: (b, l, 0))
    in_specs = [pl.BlockSpec((1, tl, D_MODEL), lambda b, l: (0, 0, 0)),
                pl.BlockSpec((1, tl, D_MODEL), next_tile),
                _resident(mod.shape),
                _resident((1, D_MODEL)),
                _resident((D_MODEL, IN_COLS))]
    args = [x, x, mod, norm_g.reshape(1, D_MODEL), w_in_bf]
    vt_spec = pl.BlockSpec((1, KV_WIDTH, tl), lambda b, l: (b, 0, l))
    vt_shape = jax.ShapeDtypeStruct((bsz, KV_WIDTH, n_tok), BF16)
    bf = lambda width: jax.ShapeDtypeStruct((bsz, n_tok, width), BF16)
    if latent:
        in_specs += [_resident((1, HEAD_DIM)), _resident((1, HEAD_DIM))]
        in_specs += [pl.BlockSpec((tl, HEAD_DIM), lambda b, l: (l, 0))] * 3
        args += [q_gain.reshape(1, HEAD_DIM), k_gain.reshape(1, HEAD_DIM), *tables]
        out_specs = [tok(LRU_WIDTH), tok(LRU_WIDTH), tok(ATTN_WIDTH), tok(KV_WIDTH), vt_spec,
                     tok(D_MODEL), tok(D_MODEL)]
        out_shape = [bf(LRU_WIDTH), bf(LRU_WIDTH), bf(ATTN_WIDTH), bf(KV_WIDTH), vt_shape,
                     bf(D_MODEL), bf(D_MODEL)]
    else:
        in_specs += [_resident((1, HEAD_DIM))]
        args += [k_gain.reshape(1, HEAD_DIM)]
        out_specs = [tok(LRU_WIDTH), tok(KV_WIDTH), vt_spec]
        out_shape = [bf(LRU_WIDTH), bf(KV_WIDTH), vt_shape]
    return pl.pallas_call(
        functools.partial(_in_proj_kernel, latent=latent, n_row_tiles=n_row_tiles, n_steps=n_steps),
        grid=grid, in_specs=in_specs, out_specs=out_specs, out_shape=out_shape,
        scratch_shapes=[pltpu.VMEM((tl, D_MODEL), BF16)],
        compiler_params=pltpu.CompilerParams(dimension_semantics=("arbitrary", "arbitrary"),
                                             vmem_limit_bytes=VMEM_LIMIT),
        name="in_proj_latent" if latent else "in_proj_ctx",
    )(*args)


def _lru_kernel(uf_ref, pf_ref, nf_ref, ub_ref, pb_ref, nb_ref, h0_ref, cw_ref, cb_ref, wg_ref, bg_ref,
                lam_ref, perm_ref, permt_ref, halo_ref, hf_ref, hb_ref, hlast_ref,
                ext_s, a_s, b_s, carry_s, *, n_tiles):
    t = pl.program_id(0)
    bsz, ts, _ = uf_ref.shape
    dirs = ((uf_ref, pf_ref, nf_ref, hf_ref, t), (ub_ref, pb_ref, nb_ref, hb_ref, n_tiles - 1 - t))

    @pl.when(t == 0)
    def _():
        carry_s[...] = h0_ref[...]

    def load_time_major(d):
        u_ref, prev_ref, next_ref, _, tile = dirs[d]
        for p in range(ts // PERM_STEPS):
            steps = slice(p * PERM_STEPS, (p + 1) * PERM_STEPS)
            u_bm = u_ref[:, steps, :].reshape(bsz * PERM_STEPS, LRU_WIDTH)
            u_tm = jnp.dot(perm_ref[...], u_bm, preferred_element_type=F32)
            ext_s[d, CONV_LEFT + p * PERM_STEPS:CONV_LEFT + (p + 1) * PERM_STEPS] = (
                u_tm.reshape(PERM_STEPS, bsz, LRU_WIDTH))
        edge = jnp.concatenate([prev_ref[:, HALO // 2:, :].reshape(bsz * HALO // 2, LRU_WIDTH),
                                next_ref[:, :HALO // 2, :].reshape(bsz * HALO // 2, LRU_WIDTH)], axis=0)
        halo = jnp.dot(halo_ref[...], edge, preferred_element_type=F32)
        ext_s[d, 0:CONV_LEFT] = (halo[0:CONV_LEFT * bsz]
                                 * jnp.where(tile > 0, 1.0, 0.0)).reshape(CONV_LEFT, bsz, LRU_WIDTH)
        ext_s[d, CONV_LEFT + ts:] = (halo[CONV_LEFT * bsz:(CONV_LEFT + 1) * bsz]
                                     * jnp.where(tile < n_tiles - 1, 1.0, 0.0)).reshape(1, bsz, LRU_WIDTH)

    def coefficients(d):
        xc = cb_ref[...][None]
        for j in range(CONV_WIDTH):
            xc = xc + ext_s[d, j:j + ts] * cw_ref[j:j + 1, :][None]
        xh = xc.reshape(ts * bsz, LRU_WIDTH)
        xb = xh.astype(BF16)
        lam = lam_ref[d]
        sp = jnp.maximum(-lam, 0.0) + jnp.log1p(jnp.exp(-jnp.abs(lam)))
        decay = sp * np.float32(-0.5 * LRU_C * np.log2(np.e))
        for j, k0 in enumerate(GATE_K0):
            cols = slice(j * GATE_COLS, (j + 1) * GATE_COLS)
            gates = jnp.dot(xb[:, k0:k0 + GATE_K], wg_ref[d, j], preferred_element_type=F32)
            t_r = jnp.tanh(gates[:, :GATE_COLS] + bg_ref[d, 0:1, cols])
            t_i = jnp.tanh(gates[:, GATE_COLS:] + bg_ref[d, 1:2, cols])
            a = jnp.exp2(decay[:, cols] * t_r + decay[:, cols])
            om = 1.0 - a * a
            coef = om * lax.rsqrt(jnp.maximum(om, 1e-30))
            a_s[d, :, :, cols] = a.reshape(ts, bsz, GATE_COLS)
            b_s[d, :, :, cols] = (coef * ((t_i + 1.0) * xh[:, cols])).reshape(ts, bsz, GATE_COLS)

    def scan(d):
        h = carry_s[d]
        for s in range(ts):
            idx = (ts - 1 - s) if d == 1 else s
            h = a_s[d, idx] * h + b_s[d, idx]
            b_s[d, idx] = h
        carry_s[d] = h

    def store_batch_major(d):
        h_ref = dirs[d][3]
        for p in range(ts // PERM_STEPS):
            steps = slice(p * PERM_STEPS, (p + 1) * PERM_STEPS)
            h_tm = b_s[d, steps].reshape(PERM_STEPS * bsz, LRU_WIDTH).astype(BF16)
            h_bm = jnp.dot(permt_ref[...], h_tm, preferred_element_type=F32)
            h_ref[:, steps, :] = h_bm.reshape(bsz, PERM_STEPS, LRU_WIDTH).astype(BF16)

    load_time_major(0)
    load_time_major(1)
    coefficients(0)
    scan(0)
    coefficients(1)
    store_batch_major(0)
    scan(1)
    store_batch_major(1)

    @pl.when(t == n_tiles - 1)
    def _():
        hlast_ref[...] = carry_s[...]


def _perm_matrices(bsz):
    n = bsz * PERM_STEPS
    r_out = np.arange(n)
    s, b = r_out // bsz, r_out % bsz
    perm = np.zeros((n, n), np.float32)
    perm[r_out, b * PERM_STEPS + s] = 1.0
    half = HALO // 2
    halo = np.zeros((4 * bsz, 2 * bsz * half), np.float32)
    for b_ in range(bsz):
        for k in range(CONV_LEFT):
            halo[k * bsz + b_, b_ * half + half - CONV_LEFT + k] = 1.0
        halo[CONV_LEFT * bsz + b_, bsz * half + b_ * half] = 1.0
    return jnp.asarray(perm, BF16), jnp.asarray(perm.T, BF16), jnp.asarray(halo, BF16)


def _lru_call(u, h0, conv_w, conv_b, w_gate, b_gate, lam):
    bsz, n_tok, _ = u.shape
    ts = SCAN_TILE
    n_tiles = n_tok // ts
    per = ts // HALO
    perm, perm_t, halo = _perm_matrices(bsz)

    def tile_specs(pos):
        return [pl.BlockSpec((bsz, ts, LRU_WIDTH), lambda t: (0, pos(t), 0)),
                pl.BlockSpec((bsz, HALO, LRU_WIDTH), lambda t: (0, jnp.maximum(pos(t) * per - 1, 0), 0)),
                pl.BlockSpec((bsz, HALO, LRU_WIDTH),
                             lambda t: (0, jnp.minimum((pos(t) + 1) * per, n_tok // HALO - 1), 0))]

    fwd = lambda t: t
    bwd = lambda t: n_tiles - 1 - t
    in_specs = tile_specs(fwd) + tile_specs(bwd) + [
        _resident((2, bsz, LRU_WIDTH)),
        _resident((CONV_WIDTH, LRU_WIDTH)),
        _resident((1, LRU_WIDTH)),
        _resident((2, len(GATE_K0), GATE_K, 2 * GATE_COLS)),
        _resident((2, 2, LRU_WIDTH)),
        _resident((2, 1, LRU_WIDTH)),
        _resident(perm.shape), _resident(perm_t.shape), _resident(halo.shape),
    ]
    out_specs = [pl.BlockSpec((bsz, ts, LRU_WIDTH), lambda t: (0, fwd(t), 0)),
                 pl.BlockSpec((bsz, ts, LRU_WIDTH), lambda t: (0, bwd(t), 0)),
                 pl.BlockSpec((2, bsz, LRU_WIDTH), lambda t: (0, 0, 0))]
    out_shape = [jax.ShapeDtypeStruct((bsz, n_tok, LRU_WIDTH), BF16),
                 jax.ShapeDtypeStruct((bsz, n_tok, LRU_WIDTH), BF16),
                 jax.ShapeDtypeStruct((2, bsz, LRU_WIDTH), F32)]
    return pl.pallas_call(
        functools.partial(_lru_kernel, n_tiles=n_tiles),
        grid=(n_tiles,), in_specs=in_specs, out_specs=out_specs, out_shape=out_shape,
        scratch_shapes=[pltpu.VMEM((2, ts + CONV_WIDTH - 1, bsz, LRU_WIDTH), F32),
                        pltpu.VMEM((2, ts, bsz, LRU_WIDTH), F32),
                        pltpu.VMEM((2, ts, bsz, LRU_WIDTH), F32),
                        pltpu.VMEM((2, bsz, LRU_WIDTH), F32)],
        compiler_params=pltpu.CompilerParams(dimension_semantics=("arbitrary",),
                                             vmem_limit_bytes=VMEM_LIMIT),
        name="lru",
    )(u, u, u, u, u, u, h0, 0.5 * conv_w, 0.5 * conv_b.reshape(1, LRU_WIDTH), w_gate, b_gate,
      lam.reshape(2, 1, LRU_WIDTH), perm, perm_t, halo)


def _attn_kernel(q_ref, kc_ref, vc_ref, kl_ref, vl_ref, o_ref, sa_s, sb_s, acc_s, m_s, l_s):
    n_chunks = kl_ref.shape[1] // KEY_CHUNK

    def scores(k, g):
        qg = q_ref[0, :, g * HEAD_DIM:(g + 1) * HEAD_DIM]
        return lax.dot_general(k, qg, (((1,), (1,)), ((), ())), preferred_element_type=F32)

    def softmax_pv(st, vt, g, first):
        m_c = jnp.max(st, axis=0, keepdims=True)
        if first:
            m_new = m_c
        else:
            m_old = m_s[g]
            m_new = jnp.maximum(m_old, m_c)
            alpha = jnp.exp2(m_old - m_new)
        p = jnp.exp2(st - m_new)
        p_sum = jnp.sum(p, axis=0, keepdims=True)
        pv = jnp.dot(vt, p.astype(BF16), preferred_element_type=F32)
        if first:
            l_s[g] = p_sum
            acc_s[g] = pv
        else:
            l_s[g] = alpha * l_s[g] + p_sum
            acc_s[g] = alpha * acc_s[g] + pv
        m_s[g] = m_new

    def lat_keys(c):
        return kl_ref[0, pl.ds(pl.multiple_of(c * KEY_CHUNK, KEY_CHUNK), KEY_CHUNK), :]

    def lat_vals(c):
        return vl_ref[0, :, pl.ds(pl.multiple_of(c * KEY_CHUNK, KEY_CHUNK), KEY_CHUNK)]

    def stage(k_next, st_in, st_out, vt, first=False):
        for g in range(GROUP):
            if k_next is not None:
                st_out[g] = scores(k_next, g)
            softmax_pv(st_in[g], vt, g, first)

    k0 = kl_ref[0, 0:KEY_CHUNK, :]
    for g in range(GROUP):
        sc = scores(kc_ref[0], g)
        sa_s[g] = scores(k0, g)
        softmax_pv(sc, vc_ref[0], g, True)

    def body(i, carry):
        stage(lat_keys(2 * i + 1), sa_s, sb_s, lat_vals(2 * i))
        stage(lat_keys(2 * i + 2), sb_s, sa_s, lat_vals(2 * i + 1))
        return carry

    lax.fori_loop(0, n_chunks // 2 - 1, body, 0)
    stage(lat_keys(n_chunks - 1), sa_s, sb_s, lat_vals(n_chunks - 2))
    stage(None, sb_s, None, lat_vals(n_chunks - 1))
    for g in range(GROUP):
        o = (acc_s[g] / l_s[g]).T
        o_ref[0, :, g * HEAD_DIM:(g + 1) * HEAD_DIM] = o.astype(BF16)


def _attn_call(q, k_ctx, vt_ctx, k_lat, vt_lat):
    bsz, n_tok, _ = q.shape
    n_ctx = k_ctx.shape[1]
    tq = Q_TILE
    gw = GROUP * HEAD_DIM
    keys = lambda n: pl.BlockSpec((1, n, HEAD_DIM), lambda b, j, i: (b, 0, j))
    vals = lambda n: pl.BlockSpec((1, HEAD_DIM, n), lambda b, j, i: (b, j, 0))
    return pl.pallas_call(
        _attn_kernel,
        grid=(bsz, N_KV_HEADS, n_tok // tq),
        in_specs=[pl.BlockSpec((1, tq, gw), lambda b, j, i: (b, i, j)),
                  keys(n_ctx), vals(n_ctx), keys(n_tok), vals(n_tok)],
        out_specs=pl.BlockSpec((1, tq, gw), lambda b, j, i: (b, i, j)),
        out_shape=jax.ShapeDtypeStruct((bsz, n_tok, ATTN_WIDTH), BF16),
        scratch_shapes=[pltpu.VMEM((GROUP, KEY_CHUNK, tq), F32),
                        pltpu.VMEM((GROUP, KEY_CHUNK, tq), F32),
                        pltpu.VMEM((GROUP, HEAD_DIM, tq), F32),
                        pltpu.VMEM((GROUP, 1, tq), F32),
                        pltpu.VMEM((GROUP, 1, tq), F32)],
        compiler_params=pltpu.CompilerParams(
            dimension_semantics=("arbitrary", "arbitrary", "arbitrary"),
            vmem_limit_bytes=VMEM_LIMIT),
        name="attn",
    )(q, k_ctx, vt_ctx, k_lat, vt_lat)


def _attn_fixed_shift_kernel(shift_ref, q_ref, kc_ref, vc_ref, kl_ref, vl_ref, o_ref,
                             sa_s, sb_s, acc_s, l_s):
    n_chunks = kl_ref.shape[1] // KEY_CHUNK
    tq = q_ref.shape[1]
    shift = shift_ref[0]
    q_all = jnp.concatenate([q_ref[0, :, g * HEAD_DIM:(g + 1) * HEAD_DIM] for g in range(GROUP)],
                            axis=0)

    def scores(k):
        return lax.dot_general(k, q_all, (((1,), (1,)), ((), ())), preferred_element_type=F32)

    def consume(st, vt, first=False):
        p = jnp.exp2(st - shift)
        p_sum = jnp.sum(p, axis=0, keepdims=True)
        pv = jnp.dot(vt, p.astype(BF16), preferred_element_type=F32)
        if first:
            l_s[...] = p_sum
            acc_s[...] = pv
        else:
            l_s[...] = l_s[...] + p_sum
            acc_s[...] = acc_s[...] + pv

    sc = scores(kc_ref[0])
    sa_s[...] = scores(kl_ref[0, 0:KEY_CHUNK, :])
    consume(sc, vc_ref[0], True)
    bufs = (sa_s, sb_s)
    for c in range(n_chunks):
        if c + 1 < n_chunks:
            bufs[(c + 1) % 2][...] = scores(kl_ref[0, (c + 1) * KEY_CHUNK:(c + 2) * KEY_CHUNK, :])
        consume(bufs[c % 2][...], vl_ref[0, :, c * KEY_CHUNK:(c + 1) * KEY_CHUNK])
    o_t = acc_s[...] / l_s[...]
    for g in range(GROUP):
        o_ref[0, :, g * HEAD_DIM:(g + 1) * HEAD_DIM] = o_t[:, g * tq:(g + 1) * tq].T.astype(BF16)


def _attn_fixed_shift_call(shift, q, k_ctx, vt_ctx, k_lat, vt_lat):
    bsz, n_tok, _ = q.shape
    n_ctx = k_ctx.shape[1]
    tq = Q_TILE
    gw = GROUP * HEAD_DIM
    keys = lambda n: pl.BlockSpec((1, n, HEAD_DIM), lambda b, j, i: (b, 0, j))
    vals = lambda n: pl.BlockSpec((1, HEAD_DIM, n), lambda b, j, i: (b, j, 0))
    return pl.pallas_call(
        _attn_fixed_shift_kernel,
        grid=(bsz, N_KV_HEADS, n_tok // tq),
        in_specs=[pl.BlockSpec(memory_space=pltpu.SMEM),
                  pl.BlockSpec((1, tq, gw), lambda b, j, i: (b, i, j)),
                  keys(n_ctx), vals(n_ctx), keys(n_tok), vals(n_tok)],
        out_specs=pl.BlockSpec((1, tq, gw), lambda b, j, i: (b, i, j)),
        out_shape=jax.ShapeDtypeStruct((bsz, n_tok, ATTN_WIDTH), BF16),
        scratch_shapes=[pltpu.VMEM((KEY_CHUNK, GROUP * tq), F32),
                        pltpu.VMEM((KEY_CHUNK, GROUP * tq), F32),
                        pltpu.VMEM((HEAD_DIM, GROUP * tq), F32),
                        pltpu.VMEM((1, GROUP * tq), F32)],
        compiler_params=pltpu.CompilerParams(
            dimension_semantics=("arbitrary", "arbitrary", "arbitrary"),
            vmem_limit_bytes=VMEM_LIMIT),
        name="attn_fixed_shift",
    )(shift, q, k_ctx, vt_ctx, k_lat, vt_lat)


def _attention(q, k_ctx, vt_ctx, k_lat, vt_lat, q_gain, k_gain):
    bound = (SCORE_BOUND_SLACK * np.float32(np.sqrt(HEAD_DIM) * np.log2(np.e))
             * jnp.max(jnp.abs(q_gain)) * jnp.max(jnp.abs(k_gain)))
    operands = (q, k_ctx, vt_ctx, k_lat, vt_lat)
    return lax.cond(bound <= MAX_FIXED_SHIFT,
                    lambda ops: _attn_fixed_shift_call(bound.reshape(1), *ops),
                    lambda ops: _attn_call(*ops), operands)


def _mix_out_kernel(hf_ref, hb_ref, g_ref, at_ref, ga_ref, gb_ref, x_ref, mod_ref,
                    wl_ref, wa_ref, wo_ref, o_ref):
    lru = hf_ref[0].astype(F32) + hb_ref[0].astype(F32)
    za = (lru * g_ref[0].astype(F32)).astype(BF16)
    ya = jnp.dot(za, wl_ref[...], preferred_element_type=F32)
    yb = jnp.dot(at_ref[0], wa_ref[...], preferred_element_type=F32)
    mix = (ga_ref[0].astype(F32) * ya + gb_ref[0].astype(F32) * yb).astype(BF16)
    y = jnp.dot(mix, wo_ref[...], preferred_element_type=F32)
    o_ref[0] = x_ref[0] + mod_ref[0, :, 2 * D_MODEL:3 * D_MODEL] * y


def _mix_out_call(h_fwd, h_bwd, gg, attn, sga, sgb, x, mod, w_lru_bf, w_attn_bf, w_out_bf):
    bsz, n_tok, _ = x.shape
    tl = TOKEN_TILE
    tok = lambda width: pl.BlockSpec((1, tl, width), lambda b, l: (b, l, 0))
    return pl.pallas_call(
        _mix_out_kernel,
        grid=(bsz, n_tok // tl),
        in_specs=[tok(LRU_WIDTH), tok(LRU_WIDTH), tok(LRU_WIDTH), tok(ATTN_WIDTH), tok(D_MODEL),
                  tok(D_MODEL), tok(D_MODEL),
                  pl.BlockSpec((1, 1, mod.shape[-1]), lambda b, l: (b, 0, 0)),
                  _resident((LRU_WIDTH, D_MODEL)), _resident((ATTN_WIDTH, D_MODEL)),
                  _resident((D_MODEL, D_MODEL))],
        out_specs=tok(D_MODEL),
        out_shape=jax.ShapeDtypeStruct((bsz, n_tok, D_MODEL), F32),
        compiler_params=pltpu.CompilerParams(dimension_semantics=("arbitrary", "arbitrary"),
                                             vmem_limit_bytes=VMEM_LIMIT),
        name="mix_out",
    )(h_fwd, h_bwd, gg, attn, sga, sgb, x, mod, w_lru_bf, w_attn_bf, w_out_bf)


def _ffn_kernel(x_ref, mod_ref, nrm_ref, wi_ref, wo_ref, o_ref):
    x = x_ref[0]
    shift = mod_ref[0, :, 3 * D_MODEL:4 * D_MODEL]
    scale = mod_ref[0, :, 4 * D_MODEL:5 * D_MODEL]
    gate_out = mod_ref[0, :, 5 * D_MODEL:6 * D_MODEL]
    h = ((x * _rms_scale(x) * nrm_ref[...]) * (1.0 + scale) + shift).astype(BF16)
    acc = jnp.zeros(x.shape, F32)
    for lo, hi in zip(FFN_SPLITS[:-1], FFN_SPLITS[1:]):
        gate = jnp.dot(h, wi_ref[:, lo:hi], preferred_element_type=F32)
        up = jnp.dot(h, wi_ref[:, FFN_HIDDEN + lo:FFN_HIDDEN + hi], preferred_element_type=F32)
        act = (gate * _sigmoid(gate) * up).astype(BF16)
        acc = acc + jnp.dot(act, wo_ref[lo:hi, :], preferred_element_type=F32)
    o_ref[0] = x + gate_out * acc


def _ffn_call(x, mod, norm_g, w_in_bf, w_out_bf):
    bsz, n_tok, _ = x.shape
    tl = TOKEN_TILE
    tok = pl.BlockSpec((1, tl, D_MODEL), lambda b, l: (b, l, 0))
    return pl.pallas_call(
        _ffn_kernel,
        grid=(bsz, n_tok // tl),
        in_specs=[tok, pl.BlockSpec((1, 1, mod.shape[-1]), lambda b, l: (b, 0, 0)),
                  _resident((1, D_MODEL)),
                  _resident((D_MODEL, 2 * FFN_HIDDEN)), _resident((FFN_HIDDEN, D_MODEL))],
        out_specs=tok,
        out_shape=jax.ShapeDtypeStruct((bsz, n_tok, D_MODEL), F32),
        compiler_params=pltpu.CompilerParams(dimension_semantics=("arbitrary", "arbitrary"),
                                             vmem_limit_bytes=VMEM_LIMIT),
        name="ffn",
    )(x, mod, norm_g.reshape(1, D_MODEL), w_in_bf, w_out_bf)


def _rope_tables(n_tok):
    pos = np.arange(n_tok)
    inv_freq = ROPE_THETA ** (-np.arange(0, ROPE_AXIS_DIM, 2, dtype=np.float64) / ROPE_AXIS_DIM)
    ang_r = (pos // GRID_W)[:, None] * inv_freq[None, :]
    ang_c = (pos % GRID_W)[:, None] * inv_freq[None, :]
    cr, sr, cc, sc = (f(a) for a in (ang_r, ang_c) for f in (np.cos, np.sin))
    zero = np.zeros_like(sr)
    cos = np.concatenate([cr, cr, cc, cc], axis=-1)
    sin_hi = np.concatenate([-sr, zero, -sc, zero], axis=-1)
    sin_lo = np.concatenate([zero, sr, zero, sc], axis=-1)
    return tuple(jnp.asarray(t, F32) for t in (cos, sin_hi, sin_lo))


def _gate_weights(wa, wx):
    def dense(w):
        rows = [jnp.pad(w[n], ((0, 0), (n * LRU_BLOCK_DIM, LRU_WIDTH - (n + 1) * LRU_BLOCK_DIM)))
                for n in range(LRU_BLOCKS)]
        return jnp.concatenate(rows, axis=0)

    da, dx = dense(wa), dense(wx)
    tiles = []
    for j, k0 in enumerate(GATE_K0):
        cols = slice(j * GATE_COLS, (j + 1) * GATE_COLS)
        tiles.append(jnp.concatenate([da[k0:k0 + GATE_K, cols], dx[k0:k0 + GATE_K, cols]], axis=1))
    return jnp.stack(tiles).astype(BF16)


def kernel(x, c, ctx, c_ctx, w_mod, b_mod, norm_mix, w_in, conv_w, conv_b, lru_wa, lru_ba, lru_wx,
           lru_bx, lru_lambda, q_norm, k_norm, w_out_lru, w_out_attn, w_out, norm_ffn, w_ffn_in,
           w_ffn_out):
    bsz, n_tok, _ = x.shape
    assert w_mod.shape[0] == 1, "single trunk layer"
    tables = _rope_tables(n_tok)

    c_all = jnp.concatenate([c, c_ctx[None, :]], axis=0)
    c_all = jnp.pad(c_all, ((0, 16 - c_all.shape[0]), (0, 0)))
    mod_all = _mod_call(c_all, w_mod[0], b_mod[0])
    mod = mod_all[:bsz].reshape(bsz, 1, -1)
    mod_c = mod_all[bsz:bsz + 1].reshape(1, 1, -1)

    col_scale = np.where(np.arange(IN_COLS) >= OFF_GA, 0.5, 1.0).astype(np.float32)
    w_in_bf = (w_in[0] * col_scale).astype(BF16)
    u_c, k_c, vt_c = _in_proj_call(ctx, mod_c, norm_mix[0], w_in_bf, None, k_norm[0], None, latent=False)
    u_l, gg, q, k_l, vt_l, sga, sgb = _in_proj_call(x, mod, norm_mix[0], w_in_bf, q_norm[0], k_norm[0],
                                                    tables, latent=True)

    w_gate = jnp.stack([_gate_weights(lru_wa[0, d], lru_wx[0, d]) for d in range(2)])
    b_gate = 0.5 * jnp.stack([lru_ba[0], lru_bx[0]], axis=1)
    lru = functools.partial(_lru_call, conv_w=conv_w[0], conv_b=conv_b[0], w_gate=w_gate, b_gate=b_gate,
                            lam=lru_lambda[0])
    _, _, h_seed = lru(u_c, jnp.zeros((2, bsz, LRU_WIDTH), F32))
    h_fwd, h_bwd, _ = lru(u_l, h_seed)

    attn = _attention(q, k_c, vt_c, k_l, vt_l, q_norm[0], k_norm[0])

    x1 = _mix_out_call(h_fwd, h_bwd, gg, attn, sga, sgb, x, mod,
                       w_out_lru[0].astype(BF16), w_out_attn[0].astype(BF16), w_out[0].astype(BF16))
    return _ffn_call(x1, mod, norm_ffn[0], w_ffn_in[0].astype(BF16), w_ffn_out[0].astype(BF16))
```

```python
import functools

import jax
import jax.numpy as jnp
import numpy as np
from jax import lax
from jax.experimental import pallas as pl
from jax.experimental.pallas import tpu as pltpu

D_MODEL = 1024
GRID_W = 64
EPS = 1e-6
LRU_WIDTH = 1280
LRU_BLOCKS = 8
LRU_BLOCK_DIM = LRU_WIDTH // LRU_BLOCKS
LRU_C = 8.0
CONV_WIDTH = 4
CONV_LEFT = 2
HEAD_DIM = 128
LANES = 128
N_HEADS = 8
N_KV_HEADS = 2
GROUP = N_HEADS // N_KV_HEADS
ATTN_WIDTH = N_HEADS * HEAD_DIM
KV_WIDTH = N_KV_HEADS * HEAD_DIM
ROPE_AXIS_DIM = HEAD_DIM // 2
ROPE_THETA = 10000.0
FFN_HIDDEN = 2816

OFF_U = 0
OFF_G = OFF_U + LRU_WIDTH
OFF_Q = OFF_G + LRU_WIDTH
OFF_K = OFF_Q + ATTN_WIDTH
OFF_V = OFF_K + KV_WIDTH
OFF_GA = OFF_V + KV_WIDTH
OFF_GB = OFF_GA + D_MODEL
IN_COLS = OFF_GB + D_MODEL

TOKEN_TILE = 512
SCAN_TILE = 64
PERM_STEPS = 32
HALO = 16
GATE_COLS = 256
GATE_K = 512
Q_TILE = 256
CAST_ROWS = 16
KEY_CHUNK = 512
FIXED_SHIFT_KEY_CHUNK = 1024
MXU_TILE = 256
FFN_SPLITS = (0, 6 * MXU_TILE, FFN_HIDDEN)
SCORE_BOUND_SLACK = 1.02
MAX_FIXED_SHIFT = 60.0
VMEM_LIMIT = 56 * 1024 * 1024

F32 = jnp.float32
BF16 = jnp.bfloat16


def _gate_k_offsets():
    offs = []
    for j in range(LRU_WIDTH // GATE_COLS):
        first_row = (j * GATE_COLS // LRU_BLOCK_DIM) * LRU_BLOCK_DIM
        last_row = ((j + 1) * GATE_COLS - 1) // LRU_BLOCK_DIM * LRU_BLOCK_DIM + LRU_BLOCK_DIM
        k0 = min(first_row // LANES * LANES, LRU_WIDTH - GATE_K)
        assert k0 <= first_row and last_row <= k0 + GATE_K
        offs.append(k0)
    return tuple(offs)


GATE_K0 = _gate_k_offsets()


def _sigmoid(x):
    return 0.5 * jnp.tanh(0.5 * x) + 0.5


def _gelu_tanh(x):
    c = np.float32(np.sqrt(2.0 / np.pi))
    inner = x * (c + np.float32(0.044715) * c * (x * x))
    return (0.5 * x) * (1.0 + jnp.tanh(inner))


def _rms_scale(x):
    return lax.rsqrt(jnp.mean(x * x, axis=-1, keepdims=True) + EPS)


def _resident(shape):
    nd = len(shape)
    return pl.BlockSpec(shape, lambda *_: (0,) * nd, pipeline_mode=pl.Buffered(1))


def _mod_kernel(c_ref, w_ref, b_ref, o_ref):
    c = c_ref[...]
    s = c * _sigmoid(c)
    o_ref[...] = jnp.dot(s, w_ref[...], preferred_element_type=F32,
                         precision=lax.Precision.HIGHEST) + b_ref[...]


def _mod_call(c_all, w_mod, b_mod):
    rows = c_all.shape[0]
    n = w_mod.shape[1]
    tn = 1024
    return pl.pallas_call(
        _mod_kernel,
        grid=(n // tn,),
        in_specs=[pl.BlockSpec((rows, D_MODEL), lambda j: (0, 0)),
                  pl.BlockSpec((D_MODEL, tn), lambda j: (0, j)),
                  pl.BlockSpec((1, tn), lambda j: (0, j))],
        out_specs=pl.BlockSpec((rows, tn), lambda j: (0, j)),
        out_shape=jax.ShapeDtypeStruct((rows, n), F32),
        compiler_params=pltpu.CompilerParams(dimension_semantics=("arbitrary",),
                                             vmem_limit_bytes=VMEM_LIMIT),
        name="mod",
    )(c_all, w_mod, b_mod.reshape(1, n))


def _gained_tables(gain, tables, scale):
    half = ROPE_AXIS_DIM // 2
    if tables is None:
        return (gain * scale,)
    cos, sin_hi, sin_lo = tables
    return (cos * (gain * scale), sin_hi * (pltpu.roll(gain, HEAD_DIM - half, 1) * scale),
            sin_lo * (pltpu.roll(gain, half, 1) * scale))


def _head_norm_rope(t, gained):
    half = ROPE_AXIS_DIM // 2
    n = t * _rms_scale(t)
    if len(gained) == 1:
        return n * gained[0]
    cos, sin_hi, sin_lo = gained
    return (n * cos + pltpu.roll(n, HEAD_DIM - half, 1) * sin_hi + pltpu.roll(n, half, 1) * sin_lo)


def _in_proj_kernel(*refs, latent, n_row_tiles, n_steps):
    if latent:
        (x0_ref, xn_ref, mod_ref, nrm_ref, w_ref, qn_ref, kn_ref, rc_ref, rh_ref, rl_ref,
         u_ref, g_ref, q_ref, k_ref, vt_ref, ga_ref, gb_ref, h_s) = refs
        tables = (rc_ref[...], rh_ref[...], rl_ref[...])
    else:
        x0_ref, xn_ref, mod_ref, nrm_ref, w_ref, kn_ref, u_ref, k_ref, vt_ref, h_s = refs
        tables = None
    step = pl.program_id(0) * n_row_tiles + pl.program_id(1)

    def normed(x, b):
        shift = mod_ref[b, :, 0:D_MODEL]
        scale = mod_ref[b, :, D_MODEL:2 * D_MODEL]
        return ((x * _rms_scale(x) * nrm_ref[...]) * (1.0 + scale) + shift).astype(BF16)

    @pl.when(step == 0)
    def _():
        h_s[...] = normed(x0_ref[0], 0)

    h = h_s[...]

    def proj(off, width):
        return jnp.dot(h, w_ref[:, off:off + width], preferred_element_type=F32)

    if latent:
        qq = proj(OFF_Q, ATTN_WIDTH)
        q_tables = _gained_tables(qn_ref[...], tables, np.float32(np.log2(np.e) / np.sqrt(HEAD_DIM)))
        for j in range(N_HEADS):
            sl = slice(j * HEAD_DIM, (j + 1) * HEAD_DIM)
            q_ref[0, :, sl] = _head_norm_rope(qq[:, sl], q_tables).astype(BF16)
    kk = proj(OFF_K, KV_WIDTH)
    k_tables = _gained_tables(kn_ref[...], tables, np.float32(1.0))
    for j in range(N_KV_HEADS):
        sl = slice(j * HEAD_DIM, (j + 1) * HEAD_DIM)
        k_ref[0, :, sl] = _head_norm_rope(kk[:, sl], k_tables).astype(BF16)
    if latent:
        g_ref[0] = _gelu_tanh(proj(OFF_G, LRU_WIDTH)).astype(BF16)
        ga_ref[0] = (0.5 * jnp.tanh(proj(OFF_GA, D_MODEL)) + 0.5).astype(BF16)
        gb_ref[0] = (0.5 * jnp.tanh(proj(OFF_GB, D_MODEL)) + 0.5).astype(BF16)
    vt_ref[0] = proj(OFF_V, KV_WIDTH).T.astype(BF16)
    u_ref[0] = proj(OFF_U, LRU_WIDTH).astype(BF16)

    nxt = jnp.minimum(step + 1, n_steps - 1)
    b_next = nxt // n_row_tiles if mod_ref.shape[0] > 1 else 0
    h_s[...] = normed(xn_ref[0], b_next)


def _next_tile_map(n_row_tiles, n_steps):
    def index_map(b, l):
        nxt = jnp.minimum(b * n_row_tiles + l + 1, n_steps - 1)
        return (nxt // n_row_tiles, nxt % n_row_tiles, 0)
    return index_map


def _in_proj_call(x, mod, norm_g, w_in_bf, q_gain, k_gain, tables, latent):
    bsz, n_tok, _ = x.shape
    tl = min(TOKEN_TILE, n_tok)
    n_row_tiles = n_tok // tl
    grid = (bsz, n_row_tiles)
    n_steps = bsz * n_row_tiles

    tok = lambda width: pl.BlockSpec((1, tl, width), lambda b, l: (b, l, 0))
    in_specs = [pl.BlockSpec((1, tl, D_MODEL), lambda b, l: (0, 0, 0)),
                pl.BlockSpec((1, tl, D_MODEL), _next_tile_map(n_row_tiles, n_steps)),
                _resident(mod.shape),
                _resident((1, D_MODEL)),
                _resident((D_MODEL, IN_COLS))]
    args = [x, x, mod, norm_g.reshape(1, D_MODEL), w_in_bf]
    vt_spec = pl.BlockSpec((1, KV_WIDTH, tl), lambda b, l: (b, 0, l))
    vt_shape = jax.ShapeDtypeStruct((bsz, KV_WIDTH, n_tok), BF16)
    bf = lambda width: jax.ShapeDtypeStruct((bsz, n_tok, width), BF16)
    if latent:
        in_specs += [_resident((1, HEAD_DIM)), _resident((1, HEAD_DIM))]
        in_specs += [pl.BlockSpec((tl, HEAD_DIM), lambda b, l: (l, 0))] * 3
        args += [q_gain.reshape(1, HEAD_DIM), k_gain.reshape(1, HEAD_DIM), *tables]
        out_specs = [tok(LRU_WIDTH), tok(LRU_WIDTH), tok(ATTN_WIDTH), tok(KV_WIDTH), vt_spec,
                     tok(D_MODEL), tok(D_MODEL)]
        out_shape = [bf(LRU_WIDTH), bf(LRU_WIDTH), bf(ATTN_WIDTH), bf(KV_WIDTH), vt_shape,
                     bf(D_MODEL), bf(D_MODEL)]
    else:
        in_specs += [_resident((1, HEAD_DIM))]
        args += [k_gain.reshape(1, HEAD_DIM)]
        out_specs = [tok(LRU_WIDTH), tok(KV_WIDTH), vt_spec]
        out_shape = [bf(LRU_WIDTH), bf(KV_WIDTH), vt_shape]
    return pl.pallas_call(
        functools.partial(_in_proj_kernel, latent=latent, n_row_tiles=n_row_tiles, n_steps=n_steps),
        grid=grid, in_specs=in_specs, out_specs=out_specs, out_shape=out_shape,
        scratch_shapes=[pltpu.VMEM((tl, D_MODEL), BF16)],
        compiler_params=pltpu.CompilerParams(dimension_semantics=("arbitrary", "arbitrary"),
                                             vmem_limit_bytes=VMEM_LIMIT),
        name="in_proj_latent" if latent else "in_proj_ctx",
    )(*args)


def _lru_kernel(uf_ref, pf_ref, nf_ref, ub_ref, pb_ref, nb_ref, h0_ref, cw_ref, cb_ref, wg_ref, bg_ref,
                lam_ref, perm_ref, permt_ref, halo_ref, hf_ref, hb_ref, hlast_ref,
                ext_s, a_s, b_s, carry_s, *, n_tiles):
    t = pl.program_id(0)
    bsz, ts, _ = uf_ref.shape
    dirs = ((uf_ref, pf_ref, nf_ref, hf_ref, t), (ub_ref, pb_ref, nb_ref, hb_ref, n_tiles - 1 - t))

    @pl.when(t == 0)
    def _():
        carry_s[...] = h0_ref[...]

    def load_time_major(d):
        u_ref, prev_ref, next_ref, _, tile = dirs[d]
        for p in range(ts // PERM_STEPS):
            steps = slice(p * PERM_STEPS, (p + 1) * PERM_STEPS)
            u_bm = u_ref[:, steps, :].reshape(bsz * PERM_STEPS, LRU_WIDTH)
            u_tm = jnp.dot(perm_ref[...], u_bm, preferred_element_type=F32)
            ext_s[d, CONV_LEFT + p * PERM_STEPS:CONV_LEFT + (p + 1) * PERM_STEPS] = (
                u_tm.reshape(PERM_STEPS, bsz, LRU_WIDTH))
        edge = jnp.concatenate([prev_ref[:, HALO // 2:, :].reshape(bsz * HALO // 2, LRU_WIDTH),
                                next_ref[:, :HALO // 2, :].reshape(bsz * HALO // 2, LRU_WIDTH)], axis=0)
        halo = jnp.dot(halo_ref[...], edge, preferred_element_type=F32)
        ext_s[d, 0:CONV_LEFT] = (halo[0:CONV_LEFT * bsz]
                                 * jnp.where(tile > 0, 1.0, 0.0)).reshape(CONV_LEFT, bsz, LRU_WIDTH)
        ext_s[d, CONV_LEFT + ts:] = (halo[CONV_LEFT * bsz:(CONV_LEFT + 1) * bsz]
                                     * jnp.where(tile < n_tiles - 1, 1.0, 0.0)).reshape(1, bsz, LRU_WIDTH)

    def coefficients(d):
        xc = cb_ref[...][None]
        for j in range(CONV_WIDTH):
            xc = xc + ext_s[d, j:j + ts] * cw_ref[j:j + 1, :][None]
        xh = xc.reshape(ts * bsz, LRU_WIDTH)
        xb = xh.astype(BF16)
        lam = lam_ref[d]
        sp = jnp.maximum(-lam, 0.0) + jnp.log1p(jnp.exp(-jnp.abs(lam)))
        decay = sp * np.float32(-0.5 * LRU_C * np.log2(np.e))
        for j, k0 in enumerate(GATE_K0):
            cols = slice(j * GATE_COLS, (j + 1) * GATE_COLS)
            gates = jnp.dot(xb[:, k0:k0 + GATE_K], wg_ref[d, j], preferred_element_type=F32)
            t_r = jnp.tanh(gates[:, :GATE_COLS] + bg_ref[d, 0:1, cols])
            t_i = jnp.tanh(gates[:, GATE_COLS:] + bg_ref[d, 1:2, cols])
            a = jnp.exp2(decay[:, cols] * t_r + decay[:, cols])
            om = 1.0 - a * a
            coef = om * lax.rsqrt(jnp.maximum(om, 1e-30))
            a_s[d, :, :, cols] = a.reshape(ts, bsz, GATE_COLS)
            b_s[d, :, :, cols] = (coef * ((t_i + 1.0) * xh[:, cols])).reshape(ts, bsz, GATE_COLS)

    def scan(d):
        h = carry_s[d]
        for s in range(ts):
            idx = (ts - 1 - s) if d == 1 else s
            h = a_s[d, idx] * h + b_s[d, idx]
            b_s[d, idx] = h
        carry_s[d] = h

    def store_batch_major(d):
        h_ref = dirs[d][3]
        for p in range(ts // PERM_STEPS):
            steps = slice(p * PERM_STEPS, (p + 1) * PERM_STEPS)
            h_tm = b_s[d, steps].reshape(PERM_STEPS * bsz, LRU_WIDTH).astype(BF16)
            h_bm = jnp.dot(permt_ref[...], h_tm, preferred_element_type=F32)
            h_ref[:, steps, :] = h_bm.reshape(bsz, PERM_STEPS, LRU_WIDTH).astype(BF16)

    load_time_major(0)
    load_time_major(1)
    coefficients(0)
    scan(0)
    coefficients(1)
    store_batch_major(0)
    scan(1)
    store_batch_major(1)

    @pl.when(t == n_tiles - 1)
    def _():
        hlast_ref[...] = carry_s[...]


def _perm_matrices(bsz):
    n = bsz * PERM_STEPS
    r_out = np.arange(n)
    s, b = r_out // bsz, r_out % bsz
    perm = np.zeros((n, n), np.float32)
    perm[r_out, b * PERM_STEPS + s] = 1.0
    half = HALO // 2
    halo = np.zeros((4 * bsz, 2 * bsz * half), np.float32)
    for b_ in range(bsz):
        for k in range(CONV_LEFT):
            halo[k * bsz + b_, b_ * half + half - CONV_LEFT + k] = 1.0
        halo[CONV_LEFT * bsz + b_, bsz * half + b_ * half] = 1.0
    return jnp.asarray(perm, BF16), jnp.asarray(perm.T, BF16), jnp.asarray(halo, BF16)


def _lru_call(u, h0, conv_w, conv_b, w_gate, b_gate, lam):
    bsz, n_tok, _ = u.shape
    ts = SCAN_TILE
    n_tiles = n_tok // ts
    per = ts // HALO
    perm, perm_t, halo = _perm_matrices(bsz)

    def tile_specs(pos):
        return [pl.BlockSpec((bsz, ts, LRU_WIDTH), lambda t: (0, pos(t), 0)),
                pl.BlockSpec((bsz, HALO, LRU_WIDTH), lambda t: (0, jnp.maximum(pos(t) * per - 1, 0), 0)),
                pl.BlockSpec((bsz, HALO, LRU_WIDTH),
                             lambda t: (0, jnp.minimum((pos(t) + 1) * per, n_tok // HALO - 1), 0))]

    fwd = lambda t: t
    bwd = lambda t: n_tiles - 1 - t
    in_specs = tile_specs(fwd) + tile_specs(bwd) + [
        _resident((2, bsz, LRU_WIDTH)),
        _resident((CONV_WIDTH, LRU_WIDTH)),
        _resident((1, LRU_WIDTH)),
        _resident((2, len(GATE_K0), GATE_K, 2 * GATE_COLS)),
        _resident((2, 2, LRU_WIDTH)),
        _resident((2, 1, LRU_WIDTH)),
        _resident(perm.shape), _resident(perm_t.shape), _resident(halo.shape),
    ]
    out_specs = [pl.BlockSpec((bsz, ts, LRU_WIDTH), lambda t: (0, fwd(t), 0)),
                 pl.BlockSpec((bsz, ts, LRU_WIDTH), lambda t: (0, bwd(t), 0)),
                 pl.BlockSpec((2, bsz, LRU_WIDTH), lambda t: (0, 0, 0))]
    out_shape = [jax.ShapeDtypeStruct((bsz, n_tok, LRU_WIDTH), BF16),
                 jax.ShapeDtypeStruct((bsz, n_tok, LRU_WIDTH), BF16),
                 jax.ShapeDtypeStruct((2, bsz, LRU_WIDTH), F32)]
    return pl.pallas_call(
        functools.partial(_lru_kernel, n_tiles=n_tiles),
        grid=(n_tiles,), in_specs=in_specs, out_specs=out_specs, out_shape=out_shape,
        scratch_shapes=[pltpu.VMEM((2, ts + CONV_WIDTH - 1, bsz, LRU_WIDTH), F32),
                        pltpu.VMEM((2, ts, bsz, LRU_WIDTH), F32),
                        pltpu.VMEM((2, ts, bsz, LRU_WIDTH), F32),
                        pltpu.VMEM((2, bsz, LRU_WIDTH), F32)],
        compiler_params=pltpu.CompilerParams(dimension_semantics=("arbitrary",),
                                             vmem_limit_bytes=VMEM_LIMIT),
        name="lru",
    )(u, u, u, u, u, u, h0, 0.5 * conv_w, 0.5 * conv_b.reshape(1, LRU_WIDTH), w_gate, b_gate,
      lam.reshape(2, 1, LRU_WIDTH), perm, perm_t, halo)


def _cast_specs(weights, n_steps, step_of):
    in_specs, out_specs, out_shape = [], [], []
    for w in weights:
        rows, cols = w.shape
        n_blocks = rows // CAST_ROWS
        assert rows % CAST_ROWS == 0 and n_blocks <= n_steps
        index_map = lambda *idx, n_blocks=n_blocks: (jnp.minimum(step_of(*idx), n_blocks - 1), 0)
        in_specs.append(pl.BlockSpec((CAST_ROWS, cols), index_map))
        out_specs.append(pl.BlockSpec((CAST_ROWS, cols), index_map))
        out_shape.append(jax.ShapeDtypeStruct((rows, cols), BF16))
    return in_specs, out_specs, out_shape


def _cast_blocks(w_refs, o_refs):
    for w_ref, o_ref in zip(w_refs, o_refs):
        o_ref[...] = w_ref[...].astype(BF16)


def _attn_kernel(q_ref, kc_ref, vc_ref, kl_ref, vl_ref, *rest):
    n_cast = (len(rest) - 6) // 2
    w_refs, o_ref, wo_refs = rest[:n_cast], rest[n_cast], rest[n_cast + 1:2 * n_cast + 1]
    sa_s, sb_s, acc_s, m_s, l_s = rest[2 * n_cast + 1:]
    _cast_blocks(w_refs, wo_refs)
    n_chunks = kl_ref.shape[1] // KEY_CHUNK

    def scores(k, g):
        qg = q_ref[0, :, g * HEAD_DIM:(g + 1) * HEAD_DIM]
        return lax.dot_general(k, qg, (((1,), (1,)), ((), ())), preferred_element_type=F32)

    def softmax_pv(st, vt, g, first):
        m_c = jnp.max(st, axis=0, keepdims=True)
        if first:
            m_new = m_c
        else:
            m_old = m_s[g]
            m_new = jnp.maximum(m_old, m_c)
            alpha = jnp.exp2(m_old - m_new)
        p = jnp.exp2(st - m_new)
        p_sum = jnp.sum(p, axis=0, keepdims=True)
        pv = jnp.dot(vt, p.astype(BF16), preferred_element_type=F32)
        if first:
            l_s[g] = p_sum
            acc_s[g] = pv
        else:
            l_s[g] = alpha * l_s[g] + p_sum
            acc_s[g] = alpha * acc_s[g] + pv
        m_s[g] = m_new

    def lat_keys(c):
        return kl_ref[0, pl.ds(pl.multiple_of(c * KEY_CHUNK, KEY_CHUNK), KEY_CHUNK), :]

    def lat_vals(c):
        return vl_ref[0, :, pl.ds(pl.multiple_of(c * KEY_CHUNK, KEY_CHUNK), KEY_CHUNK)]

    def stage(k_next, st_in, st_out, vt, first=False):
        for g in range(GROUP):
            if k_next is not None:
                st_out[g] = scores(k_next, g)
            softmax_pv(st_in[g], vt, g, first)

    k0 = kl_ref[0, 0:KEY_CHUNK, :]
    for g in range(GROUP):
        sc = scores(kc_ref[0], g)
        sa_s[g] = scores(k0, g)
        softmax_pv(sc, vc_ref[0], g, True)

    def body(i, carry):
        stage(lat_keys(2 * i + 1), sa_s, sb_s, lat_vals(2 * i))
        stage(lat_keys(2 * i + 2), sb_s, sa_s, lat_vals(2 * i + 1))
        return carry

    lax.fori_loop(0, n_chunks // 2 - 1, body, 0)
    stage(lat_keys(n_chunks - 1), sa_s, sb_s, lat_vals(n_chunks - 2))
    stage(None, sb_s, None, lat_vals(n_chunks - 1))
    for g in range(GROUP):
        o = (acc_s[g] / l_s[g]).T
        o_ref[0, :, g * HEAD_DIM:(g + 1) * HEAD_DIM] = o.astype(BF16)


def _attn_call(q, k_ctx, vt_ctx, k_lat, vt_lat, *weights):
    bsz, n_tok, _ = q.shape
    n_ctx = k_ctx.shape[1]
    tq = Q_TILE
    gw = GROUP * HEAD_DIM
    n_qt = n_tok // tq
    keys = lambda n: pl.BlockSpec((1, n, HEAD_DIM), lambda b, j, i: (b, 0, j))
    vals = lambda n: pl.BlockSpec((1, HEAD_DIM, n), lambda b, j, i: (b, j, 0))
    w_in, w_out, w_shape = _cast_specs(weights, bsz * N_KV_HEADS * n_qt,
                                       lambda b, j, i: (b * N_KV_HEADS + j) * n_qt + i)
    return pl.pallas_call(
        _attn_kernel,
        grid=(bsz, N_KV_HEADS, n_qt),
        in_specs=[pl.BlockSpec((1, tq, gw), lambda b, j, i: (b, i, j)),
                  keys(n_ctx), vals(n_ctx), keys(n_tok), vals(n_tok)] + w_in,
        out_specs=[pl.BlockSpec((1, tq, gw), lambda b, j, i: (b, i, j))] + w_out,
        out_shape=[jax.ShapeDtypeStruct((bsz, n_tok, ATTN_WIDTH), BF16)] + w_shape,
        scratch_shapes=[pltpu.VMEM((GROUP, KEY_CHUNK, tq), F32),
                        pltpu.VMEM((GROUP, KEY_CHUNK, tq), F32),
                        pltpu.VMEM((GROUP, HEAD_DIM, tq), F32),
                        pltpu.VMEM((GROUP, 1, tq), F32),
                        pltpu.VMEM((GROUP, 1, tq), F32)],
        compiler_params=pltpu.CompilerParams(
            dimension_semantics=("arbitrary", "arbitrary", "arbitrary"),
            vmem_limit_bytes=VMEM_LIMIT),
        name="attn",
    )(q, k_ctx, vt_ctx, k_lat, vt_lat, *weights)


def _attn_fixed_shift_kernel(shift_ref, q_ref, kc_ref, vc_ref, kl_ref, vl_ref, *rest):
    n_cast = (len(rest) - 5) // 2
    w_refs, o_ref, wo_refs = rest[:n_cast], rest[n_cast], rest[n_cast + 1:2 * n_cast + 1]
    sa_s, sb_s, acc_s, l_s = rest[2 * n_cast + 1:]
    _cast_blocks(w_refs, wo_refs)
    kc = sa_s.shape[0]
    n_chunks = kl_ref.shape[1] // kc
    tq = q_ref.shape[1]
    shift = shift_ref[0]
    q_all = jnp.concatenate([q_ref[0, :, g * HEAD_DIM:(g + 1) * HEAD_DIM] for g in range(GROUP)],
                            axis=0)

    def scores(k):
        return lax.dot_general(k, q_all, (((1,), (1,)), ((), ())), preferred_element_type=F32)

    def consume(st, vt, first=False):
        p = jnp.exp2(st - shift)
        p_sum = jnp.sum(p, axis=0, keepdims=True)
        pv = jnp.dot(vt, p.astype(BF16), preferred_element_type=F32)
        if first:
            l_s[...] = p_sum
            acc_s[...] = pv
        else:
            l_s[...] = l_s[...] + p_sum
            acc_s[...] = acc_s[...] + pv

    sc = scores(kc_ref[0])
    sa_s[...] = scores(kl_ref[0, 0:kc, :])
    consume(sc, vc_ref[0], True)
    bufs = (sa_s, sb_s)
    for c in range(n_chunks):
        if c + 1 < n_chunks:
            bufs[(c + 1) % 2][...] = scores(kl_ref[0, (c + 1) * kc:(c + 2) * kc, :])
        consume(bufs[c % 2][...], vl_ref[0, :, c * kc:(c + 1) * kc])
    o_t = acc_s[...] / l_s[...]
    for g in range(GROUP):
        o_ref[0, :, g * HEAD_DIM:(g + 1) * HEAD_DIM] = o_t[:, g * tq:(g + 1) * tq].T.astype(BF16)


def _attn_fixed_shift_call(shift, q, k_ctx, vt_ctx, k_lat, vt_lat, *weights):
    bsz, n_tok, _ = q.shape
    n_ctx = k_ctx.shape[1]
    tq = Q_TILE
    gw = GROUP * HEAD_DIM
    n_qt = n_tok // tq
    keys = lambda n: pl.BlockSpec((1, n, HEAD_DIM), lambda b, j, i: (b, 0, j))
    vals = lambda n: pl.BlockSpec((1, HEAD_DIM, n), lambda b, j, i: (b, j, 0))
    w_in, w_out, w_shape = _cast_specs(weights, bsz * N_KV_HEADS * n_qt,
                                       lambda b, j, i: (b * N_KV_HEADS + j) * n_qt + i)
    return pl.pallas_call(
        _attn_fixed_shift_kernel,
        grid=(bsz, N_KV_HEADS, n_qt),
        in_specs=[pl.BlockSpec(memory_space=pltpu.SMEM),
                  pl.BlockSpec((1, tq, gw), lambda b, j, i: (b, i, j)),
                  keys(n_ctx), vals(n_ctx), keys(n_tok), vals(n_tok)] + w_in,
        out_specs=[pl.BlockSpec((1, tq, gw), lambda b, j, i: (b, i, j))] + w_out,
        out_shape=[jax.ShapeDtypeStruct((bsz, n_tok, ATTN_WIDTH), BF16)] + w_shape,
        scratch_shapes=[pltpu.VMEM((FIXED_SHIFT_KEY_CHUNK, GROUP * tq), F32),
                        pltpu.VMEM((FIXED_SHIFT_KEY_CHUNK, GROUP * tq), F32),
                        pltpu.VMEM((HEAD_DIM, GROUP * tq), F32),
                        pltpu.VMEM((1, GROUP * tq), F32)],
        compiler_params=pltpu.CompilerParams(
            dimension_semantics=("arbitrary", "arbitrary", "arbitrary"),
            vmem_limit_bytes=VMEM_LIMIT),
        name="attn_fixed_shift",
    )(shift, q, k_ctx, vt_ctx, k_lat, vt_lat, *weights)


def _attention(q, k_ctx, vt_ctx, k_lat, vt_lat, q_gain, k_gain, weights):
    bound = (SCORE_BOUND_SLACK * np.float32(np.sqrt(HEAD_DIM) * np.log2(np.e))
             * jnp.max(jnp.abs(q_gain)) * jnp.max(jnp.abs(k_gain)))
    operands = (q, k_ctx, vt_ctx, k_lat, vt_lat, *weights)
    return lax.cond(bound <= MAX_FIXED_SHIFT,
                    lambda ops: _attn_fixed_shift_call(bound.reshape(1), *ops),
                    lambda ops: _attn_call(*ops), operands)


def _mix_out_kernel(hf_ref, hb_ref, g_ref, at_ref, ga_ref, gb_ref, x_ref, mod_ref,
                    wl_ref, wa_ref, wo_ref, o_ref):
    lru = hf_ref[0].astype(F32) + hb_ref[0].astype(F32)
    za = (lru * g_ref[0].astype(F32)).astype(BF16)
    ya = jnp.dot(za, wl_ref[...], preferred_element_type=F32)
    yb = jnp.dot(at_ref[0], wa_ref[...], preferred_element_type=F32)
    mix = (ga_ref[0].astype(F32) * ya + gb_ref[0].astype(F32) * yb).astype(BF16)
    y = jnp.dot(mix, wo_ref[...], preferred_element_type=F32)
    o_ref[0] = x_ref[0] + mod_ref[0, :, 2 * D_MODEL:3 * D_MODEL] * y


def _mix_out_call(h_fwd, h_bwd, gg, attn, sga, sgb, x, mod, w_lru_bf, w_attn_bf, w_out_bf):
    bsz, n_tok, _ = x.shape
    tl = TOKEN_TILE
    tok = lambda width: pl.BlockSpec((1, tl, width), lambda b, l: (b, l, 0))
    return pl.pallas_call(
        _mix_out_kernel,
        grid=(bsz, n_tok // tl),
        in_specs=[tok(LRU_WIDTH), tok(LRU_WIDTH), tok(LRU_WIDTH), tok(ATTN_WIDTH), tok(D_MODEL),
                  tok(D_MODEL), tok(D_MODEL),
                  pl.BlockSpec((1, 1, mod.shape[-1]), lambda b, l: (b, 0, 0)),
                  _resident((LRU_WIDTH, D_MODEL)), _resident((ATTN_WIDTH, D_MODEL)),
                  _resident((D_MODEL, D_MODEL))],
        out_specs=tok(D_MODEL),
        out_shape=jax.ShapeDtypeStruct((bsz, n_tok, D_MODEL), F32),
        compiler_params=pltpu.CompilerParams(dimension_semantics=("arbitrary", "arbitrary"),
                                             vmem_limit_bytes=VMEM_LIMIT),
        name="mix_out",
    )(h_fwd, h_bwd, gg, attn, sga, sgb, x, mod, w_lru_bf, w_attn_bf, w_out_bf)


def _ffn_kernel(x_ref, mod_ref, nrm_ref, wi_ref, wo_ref, o_ref):
    x = x_ref[0]
    shift = mod_ref[0, :, 3 * D_MODEL:4 * D_MODEL]
    scale = mod_ref[0, :, 4 * D_MODEL:5 * D_MODEL]
    gate_out = mod_ref[0, :, 5 * D_MODEL:6 * D_MODEL]
    h = ((x * _rms_scale(x) * nrm_ref[...]) * (1.0 + scale) + shift).astype(BF16)
    acc = jnp.zeros(x.shape, F32)
    for lo, hi in zip(FFN_SPLITS[:-1], FFN_SPLITS[1:]):
        gate = jnp.dot(h, wi_ref[:, lo:hi], preferred_element_type=F32)
        up = jnp.dot(h, wi_ref[:, FFN_HIDDEN + lo:FFN_HIDDEN + hi], preferred_element_type=F32)
        act = (gate * _sigmoid(gate) * up).astype(BF16)
        acc = acc + jnp.dot(act, wo_ref[lo:hi, :], preferred_element_type=F32)
    o_ref[0] = x + gate_out * acc


def _ffn_call(x, mod, norm_g, w_in_bf, w_out_bf):
    bsz, n_tok, _ = x.shape
    tl = TOKEN_TILE
    tok = pl.BlockSpec((1, tl, D_MODEL), lambda b, l: (b, l, 0))
    return pl.pallas_call(
        _ffn_kernel,
        grid=(bsz, n_tok // tl),
        in_specs=[tok, pl.BlockSpec((1, 1, mod.shape[-1]), lambda b, l: (b, 0, 0)),
                  _resident((1, D_MODEL)),
                  _resident((D_MODEL, 2 * FFN_HIDDEN)), _resident((FFN_HIDDEN, D_MODEL))],
        out_specs=tok,
        out_shape=jax.ShapeDtypeStruct((bsz, n_tok, D_MODEL), F32),
        compiler_params=pltpu.CompilerParams(dimension_semantics=("arbitrary", "arbitrary"),
                                             vmem_limit_bytes=VMEM_LIMIT),
        name="ffn",
    )(x, mod, norm_g.reshape(1, D_MODEL), w_in_bf, w_out_bf)


def _rope_tables(n_tok):
    pos = np.arange(n_tok)
    inv_freq = ROPE_THETA ** (-np.arange(0, ROPE_AXIS_DIM, 2, dtype=np.float64) / ROPE_AXIS_DIM)
    ang_r = (pos // GRID_W)[:, None] * inv_freq[None, :]
    ang_c = (pos % GRID_W)[:, None] * inv_freq[None, :]
    cr, sr, cc, sc = (f(a) for a in (ang_r, ang_c) for f in (np.cos, np.sin))
    zero = np.zeros_like(sr)
    cos = np.concatenate([cr, cr, cc, cc], axis=-1)
    sin_hi = np.concatenate([-sr, zero, -sc, zero], axis=-1)
    sin_lo = np.concatenate([zero, sr, zero, sc], axis=-1)
    return tuple(jnp.asarray(t, F32) for t in (cos, sin_hi, sin_lo))


def _gate_weights(wa, wx):
    def dense(w):
        rows = [jnp.pad(w[n], ((0, 0), (n * LRU_BLOCK_DIM, LRU_WIDTH - (n + 1) * LRU_BLOCK_DIM)))
                for n in range(LRU_BLOCKS)]
        return jnp.concatenate(rows, axis=0)

    da, dx = dense(wa), dense(wx)
    tiles = []
    for j, k0 in enumerate(GATE_K0):
        cols = slice(j * GATE_COLS, (j + 1) * GATE_COLS)
        tiles.append(jnp.concatenate([da[k0:k0 + GATE_K, cols], dx[k0:k0 + GATE_K, cols]], axis=1))
    return jnp.stack(tiles).astype(BF16)


def kernel(x, c, ctx, c_ctx, w_mod, b_mod, norm_mix, w_in, conv_w, conv_b, lru_wa, lru_ba, lru_wx,
           lru_bx, lru_lambda, q_norm, k_norm, w_out_lru, w_out_attn, w_out, norm_ffn, w_ffn_in,
           w_ffn_out):
    bsz, n_tok, _ = x.shape
    assert w_mod.shape[0] == 1, "single trunk layer"
    tables = _rope_tables(n_tok)

    c_all = jnp.concatenate([c, c_ctx[None, :]], axis=0)
    c_all = jnp.pad(c_all, ((0, 16 - c_all.shape[0]), (0, 0)))
    mod_all = _mod_call(c_all, w_mod[0], b_mod[0])
    mod = mod_all[:bsz].reshape(bsz, 1, -1)
    mod_c = mod_all[bsz:bsz + 1].reshape(1, 1, -1)

    col_scale = np.where(np.arange(IN_COLS) >= OFF_GA, 0.5, 1.0).astype(np.float32)
    w_in_bf = (w_in[0] * col_scale).astype(BF16)
    u_c, k_c, vt_c = _in_proj_call(ctx, mod_c, norm_mix[0], w_in_bf, None, k_norm[0], None, latent=False)
    u_l, gg, q, k_l, vt_l, sga, sgb = _in_proj_call(x, mod, norm_mix[0], w_in_bf, q_norm[0], k_norm[0],
                                                    tables, latent=True)

    w_gate = jnp.stack([_gate_weights(lru_wa[0, d], lru_wx[0, d]) for d in range(2)])
    b_gate = 0.5 * jnp.stack([lru_ba[0], lru_bx[0]], axis=1)
    lru = functools.partial(_lru_call, conv_w=conv_w[0], conv_b=conv_b[0], w_gate=w_gate, b_gate=b_gate,
                            lam=lru_lambda[0])
    _, _, h_seed = lru(u_c, jnp.zeros((2, bsz, LRU_WIDTH), F32))
    h_fwd, h_bwd, _ = lru(u_l, h_seed)

    later_weights = (w_out_lru[0], w_out_attn[0], w_out[0], w_ffn_in[0], w_ffn_out[0])
    attn, w_lru_bf, w_attn_bf, w_out_bf, w_ffn_in_bf, w_ffn_out_bf = _attention(
        q, k_c, vt_c, k_l, vt_l, q_norm[0], k_norm[0], later_weights)

    x1 = _mix_out_call(h_fwd, h_bwd, gg, attn, sga, sgb, x, mod, w_lru_bf, w_attn_bf, w_out_bf)
    return _ffn_call(x1, mod, norm_ffn[0], w_ffn_in_bf, w_ffn_out_bf)
```

```python
import functools

import jax
import jax.numpy as jnp
import numpy as np
from jax import lax
from jax.experimental import pallas as pl
from jax.experimental.pallas import tpu as pltpu

D_MODEL = 1024
GRID_W = 64
EPS = 1e-6
LRU_WIDTH = 1280
LRU_BLOCKS = 8
LRU_BLOCK_DIM = LRU_WIDTH // LRU_BLOCKS
LRU_C = 8.0
CONV_WIDTH = 4
CONV_LEFT = 2
HEAD_DIM = 128
LANES = 128
SUBLANES = 8
N_HEADS = 8
N_KV_HEADS = 2
GROUP = N_HEADS // N_KV_HEADS
ATTN_WIDTH = N_HEADS * HEAD_DIM
KV_WIDTH = N_KV_HEADS * HEAD_DIM
ROPE_AXIS_DIM = HEAD_DIM // 2
ROPE_THETA = 10000.0
FFN_HIDDEN = 2816

OFF_U = 0
OFF_G = OFF_U + LRU_WIDTH
OFF_Q = OFF_G + LRU_WIDTH
OFF_K = OFF_Q + ATTN_WIDTH
OFF_V = OFF_K + KV_WIDTH
OFF_GA = OFF_V + KV_WIDTH
OFF_GB = OFF_GA + D_MODEL
IN_COLS = OFF_GB + D_MODEL

MOD_COLS_TILE = 1024
TOKEN_TILE = 512
SCAN_TILE = 64
PERM_STEPS = 32
HALO = 16
GATE_COLS = 256
GATE_K = 512
Q_TILE = 256
CAST_ROWS = 16
KEY_CHUNK = 512
MXU_TILE = 256
FFN_SPLITS = (0, 6 * MXU_TILE, FFN_HIDDEN)
SCORE_BOUND_SLACK = 1.02
MAX_FIXED_SHIFT = 60.0
VMEM_LIMIT = 56 * 1024 * 1024

F32 = jnp.float32
BF16 = jnp.bfloat16


def _gate_k_offsets():
    offs = []
    for j in range(LRU_WIDTH // GATE_COLS):
        first_row = (j * GATE_COLS // LRU_BLOCK_DIM) * LRU_BLOCK_DIM
        last_row = ((j + 1) * GATE_COLS - 1) // LRU_BLOCK_DIM * LRU_BLOCK_DIM + LRU_BLOCK_DIM
        k0 = min(first_row // LANES * LANES, LRU_WIDTH - GATE_K)
        assert k0 <= first_row and last_row <= k0 + GATE_K
        offs.append(k0)
    return tuple(offs)


GATE_K0 = _gate_k_offsets()


def _sigmoid(x):
    return 0.5 * jnp.tanh(0.5 * x) + 0.5


def _gelu_tanh(x):
    c = np.float32(np.sqrt(2.0 / np.pi))
    inner = x * (c + np.float32(0.044715) * c * (x * x))
    return (0.5 * x) * (1.0 + jnp.tanh(inner))


def _rms_scale(x):
    return lax.rsqrt(jnp.mean(x * x, axis=-1, keepdims=True) + EPS)


def _resident(shape):
    nd = len(shape)
    return pl.BlockSpec(shape, lambda *_: (0,) * nd, pipeline_mode=pl.Buffered(1))


def _mod_kernel(c_ref, w_ref, b_ref, o_ref):
    c = c_ref[...]
    s = c * _sigmoid(c)
    o_ref[...] = jnp.dot(s, w_ref[...], preferred_element_type=F32,
                         precision=lax.Precision.HIGHEST) + b_ref[...]


def _mod_call(c_all, w_mod, b_mod):
    rows = c_all.shape[0]
    n = w_mod.shape[1]
    tn = MOD_COLS_TILE
    return pl.pallas_call(
        _mod_kernel,
        grid=(n // tn,),
        in_specs=[pl.BlockSpec((rows, D_MODEL), lambda j: (0, 0)),
                  pl.BlockSpec((D_MODEL, tn), lambda j: (0, j)),
                  pl.BlockSpec((1, tn), lambda j: (0, j))],
        out_specs=pl.BlockSpec((rows, tn), lambda j: (0, j)),
        out_shape=jax.ShapeDtypeStruct((rows, n), F32),
        compiler_params=pltpu.CompilerParams(dimension_semantics=("arbitrary",),
                                             vmem_limit_bytes=VMEM_LIMIT),
        name="mod",
    )(c_all, w_mod, b_mod.reshape(1, n))


def _gained_tables(gain, tables, scale):
    half = ROPE_AXIS_DIM // 2
    if tables is None:
        return (gain * scale,)
    cos, sin_hi, sin_lo = tables
    return (cos * (gain * scale), sin_hi * (pltpu.roll(gain, HEAD_DIM - half, 1) * scale),
            sin_lo * (pltpu.roll(gain, half, 1) * scale))


def _head_norm_rope(t, gained):
    half = ROPE_AXIS_DIM // 2
    n = t * _rms_scale(t)
    if len(gained) == 1:
        return n * gained[0]
    cos, sin_hi, sin_lo = gained
    return (n * cos + pltpu.roll(n, HEAD_DIM - half, 1) * sin_hi + pltpu.roll(n, half, 1) * sin_lo)


def _in_proj_kernel(*refs, latent, n_row_tiles, n_steps):
    if latent:
        (x0_ref, xn_ref, mod_ref, nrm_ref, w_ref, qn_ref, kn_ref, rc_ref, rh_ref, rl_ref,
         u_ref, g_ref, q_ref, k_ref, vt_ref, ga_ref, gb_ref, h_s) = refs
        tables = (rc_ref[...], rh_ref[...], rl_ref[...])
    else:
        x0_ref, xn_ref, mod_ref, nrm_ref, w_ref, kn_ref, u_ref, k_ref, vt_ref, h_s = refs
        tables = None
    step = pl.program_id(0) * n_row_tiles + pl.program_id(1)

    def normed(x, b):
        shift = mod_ref[b, :, 0:D_MODEL]
        scale = mod_ref[b, :, D_MODEL:2 * D_MODEL]
        return ((x * _rms_scale(x) * nrm_ref[...]) * (1.0 + scale) + shift).astype(BF16)

    @pl.when(step == 0)
    def _():
        h_s[...] = normed(x0_ref[0], 0)

    h = h_s[...]

    def proj(off, width):
        return jnp.dot(h, w_ref[:, off:off + width], preferred_element_type=F32)

    if latent:
        qq = proj(OFF_Q, ATTN_WIDTH)
        q_tables = _gained_tables(qn_ref[...], tables, np.float32(np.log2(np.e) / np.sqrt(HEAD_DIM)))
        for j in range(N_HEADS):
            sl = slice(j * HEAD_DIM, (j + 1) * HEAD_DIM)
            q_ref[0, :, sl] = _head_norm_rope(qq[:, sl], q_tables).astype(BF16)
    kk = proj(OFF_K, KV_WIDTH)
    k_tables = _gained_tables(kn_ref[...], tables, np.float32(1.0))
    for j in range(N_KV_HEADS):
        sl = slice(j * HEAD_DIM, (j + 1) * HEAD_DIM)
        k_ref[0, :, sl] = _head_norm_rope(kk[:, sl], k_tables).astype(BF16)
    if latent:
        g_ref[0] = _gelu_tanh(proj(OFF_G, LRU_WIDTH)).astype(BF16)
        ga_ref[0] = (0.5 * jnp.tanh(proj(OFF_GA, D_MODEL)) + 0.5).astype(BF16)
        gb_ref[0] = (0.5 * jnp.tanh(proj(OFF_GB, D_MODEL)) + 0.5).astype(BF16)
    vt_ref[0] = proj(OFF_V, KV_WIDTH).T.astype(BF16)
    u_ref[0] = proj(OFF_U, LRU_WIDTH).astype(BF16)

    nxt = jnp.minimum(step + 1, n_steps - 1)
    b_next = nxt // n_row_tiles if mod_ref.shape[0] > 1 else 0
    h_s[...] = normed(xn_ref[0], b_next)


def _next_tile_map(n_row_tiles, n_steps):
    def index_map(b, l):
        nxt = jnp.minimum(b * n_row_tiles + l + 1, n_steps - 1)
        return (nxt // n_row_tiles, nxt % n_row_tiles, 0)
    return index_map


def _in_proj_call(x, mod, norm_g, w_in_bf, q_gain, k_gain, tables, latent):
    bsz, n_tok, _ = x.shape
    tl = min(TOKEN_TILE, n_tok)
    n_row_tiles = n_tok // tl
    grid = (bsz, n_row_tiles)
    n_steps = bsz * n_row_tiles

    tok = lambda width: pl.BlockSpec((1, tl, width), lambda b, l: (b, l, 0))
    in_specs = [pl.BlockSpec((1, tl, D_MODEL), lambda b, l: (0, 0, 0)),
                pl.BlockSpec((1, tl, D_MODEL), _next_tile_map(n_row_tiles, n_steps)),
                _resident(mod.shape),
                _resident((1, D_MODEL)),
                _resident((D_MODEL, IN_COLS))]
    args = [x, x, mod, norm_g.reshape(1, D_MODEL), w_in_bf]
    vt_spec = pl.BlockSpec((1, KV_WIDTH, tl), lambda b, l: (b, 0, l))
    vt_shape = jax.ShapeDtypeStruct((bsz, KV_WIDTH, n_tok), BF16)
    bf = lambda width: jax.ShapeDtypeStruct((bsz, n_tok, width), BF16)
    if latent:
        in_specs += [_resident((1, HEAD_DIM)), _resident((1, HEAD_DIM))]
        in_specs += [pl.BlockSpec((tl, HEAD_DIM), lambda b, l: (l, 0))] * 3
        args += [q_gain.reshape(1, HEAD_DIM), k_gain.reshape(1, HEAD_DIM), *tables]
        out_specs = [tok(LRU_WIDTH), tok(LRU_WIDTH), tok(ATTN_WIDTH), tok(KV_WIDTH), vt_spec,
                     tok(D_MODEL), tok(D_MODEL)]
        out_shape = [bf(LRU_WIDTH), bf(LRU_WIDTH), bf(ATTN_WIDTH), bf(KV_WIDTH), vt_shape,
                     bf(D_MODEL), bf(D_MODEL)]
    else:
        in_specs += [_resident((1, HEAD_DIM))]
        args += [k_gain.reshape(1, HEAD_DIM)]
        out_specs = [tok(LRU_WIDTH), tok(KV_WIDTH), vt_spec]
        out_shape = [bf(LRU_WIDTH), bf(KV_WIDTH), vt_shape]
    return pl.pallas_call(
        functools.partial(_in_proj_kernel, latent=latent, n_row_tiles=n_row_tiles, n_steps=n_steps),
        grid=grid, in_specs=in_specs, out_specs=out_specs, out_shape=out_shape,
        scratch_shapes=[pltpu.VMEM((tl, D_MODEL), BF16)],
        compiler_params=pltpu.CompilerParams(dimension_semantics=("arbitrary", "arbitrary"),
                                             vmem_limit_bytes=VMEM_LIMIT),
        name="in_proj_latent" if latent else "in_proj_ctx",
    )(*args)


def _lru_kernel(uf_ref, pf_ref, nf_ref, ub_ref, pb_ref, nb_ref, h0_ref, cw_ref, cb_ref, wg_ref, bg_ref,
                lam_ref, perm_ref, permt_ref, halo_ref, hf_ref, hb_ref, hlast_ref,
                ext_s, a_s, b_s, carry_s, *, n_tiles):
    t = pl.program_id(0)
    bsz, ts, _ = uf_ref.shape
    dirs = ((uf_ref, pf_ref, nf_ref, hf_ref, t), (ub_ref, pb_ref, nb_ref, hb_ref, n_tiles - 1 - t))

    @pl.when(t == 0)
    def _():
        carry_s[...] = h0_ref[...]

    def load_time_major(d):
        u_ref, prev_ref, next_ref, _, tile = dirs[d]
        for p in range(ts // PERM_STEPS):
            steps = slice(p * PERM_STEPS, (p + 1) * PERM_STEPS)
            u_bm = u_ref[:, steps, :].reshape(bsz * PERM_STEPS, LRU_WIDTH)
            u_tm = jnp.dot(perm_ref[...], u_bm, preferred_element_type=F32)
            ext_s[d, CONV_LEFT + p * PERM_STEPS:CONV_LEFT + (p + 1) * PERM_STEPS] = (
                u_tm.reshape(PERM_STEPS, bsz, LRU_WIDTH))
        edge = jnp.concatenate([prev_ref[:, HALO // 2:, :].reshape(bsz * HALO // 2, LRU_WIDTH),
                                next_ref[:, :HALO // 2, :].reshape(bsz * HALO // 2, LRU_WIDTH)], axis=0)
        halo = jnp.dot(halo_ref[...], edge, preferred_element_type=F32)
        ext_s[d, 0:CONV_LEFT] = (halo[0:CONV_LEFT * bsz]
                                 * jnp.where(tile > 0, 1.0, 0.0)).reshape(CONV_LEFT, bsz, LRU_WIDTH)
        ext_s[d, CONV_LEFT + ts:] = (halo[CONV_LEFT * bsz:(CONV_LEFT + 1) * bsz]
                                     * jnp.where(tile < n_tiles - 1, 1.0, 0.0)).reshape(1, bsz, LRU_WIDTH)

    def coefficients(d):
        xc = cb_ref[...][None]
        for j in range(CONV_WIDTH):
            xc = xc + ext_s[d, j:j + ts] * cw_ref[j:j + 1, :][None]
        xh = xc.reshape(ts * bsz, LRU_WIDTH)
        xb = xh.astype(BF16)
        lam = lam_ref[d]
        sp = jnp.maximum(-lam, 0.0) + jnp.log1p(jnp.exp(-jnp.abs(lam)))
        decay = sp * np.float32(-0.5 * LRU_C * np.log2(np.e))
        for j, k0 in enumerate(GATE_K0):
            cols = slice(j * GATE_COLS, (j + 1) * GATE_COLS)
            gates = jnp.dot(xb[:, k0:k0 + GATE_K], wg_ref[d, j], preferred_element_type=F32)
            t_r = jnp.tanh(gates[:, :GATE_COLS] + bg_ref[d, 0:1, cols])
            t_i = jnp.tanh(gates[:, GATE_COLS:] + bg_ref[d, 1:2, cols])
            a = jnp.exp2(decay[:, cols] * t_r + decay[:, cols])
            om = 1.0 - a * a
            coef = om * lax.rsqrt(jnp.maximum(om, 1e-30))
            a_s[d, :, :, cols] = a.reshape(ts, bsz, GATE_COLS)
            b_s[d, :, :, cols] = (coef * ((t_i + 1.0) * xh[:, cols])).reshape(ts, bsz, GATE_COLS)

    def scan(d):
        h = carry_s[d]
        for s in range(ts):
            idx = (ts - 1 - s) if d == 1 else s
            h = a_s[d, idx] * h + b_s[d, idx]
            b_s[d, idx] = h
        carry_s[d] = h

    def store_batch_major(d):
        h_ref = dirs[d][3]
        for p in range(ts // PERM_STEPS):
            steps = slice(p * PERM_STEPS, (p + 1) * PERM_STEPS)
            h_tm = b_s[d, steps].reshape(PERM_STEPS * bsz, LRU_WIDTH).astype(BF16)
            h_bm = jnp.dot(permt_ref[...], h_tm, preferred_element_type=F32)
            h_ref[:, steps, :] = h_bm.reshape(bsz, PERM_STEPS, LRU_WIDTH).astype(BF16)

    load_time_major(0)
    load_time_major(1)
    coefficients(0)
    scan(0)
    coefficients(1)
    store_batch_major(0)
    scan(1)
    store_batch_major(1)

    @pl.when(t == n_tiles - 1)
    def _():
        hlast_ref[...] = carry_s[...]


def _perm_matrices(bsz):
    n = bsz * PERM_STEPS
    r_out = np.arange(n)
    s, b = r_out // bsz, r_out % bsz
    perm = np.zeros((n, n), np.float32)
    perm[r_out, b * PERM_STEPS + s] = 1.0
    half = HALO // 2
    halo = np.zeros((CONV_WIDTH * bsz, 2 * bsz * half), np.float32)
    for b_ in range(bsz):
        for k in range(CONV_LEFT):
            halo[k * bsz + b_, b_ * half + half - CONV_LEFT + k] = 1.0
        halo[CONV_LEFT * bsz + b_, bsz * half + b_ * half] = 1.0
    return jnp.asarray(perm, BF16), jnp.asarray(perm.T, BF16), jnp.asarray(halo, BF16)


def _lru_call(u, h0, conv_w, conv_b, w_gate, b_gate, lam):
    bsz, n_tok, _ = u.shape
    ts = SCAN_TILE
    n_tiles = n_tok // ts
    per = ts // HALO
    perm, perm_t, halo = _perm_matrices(bsz)

    def tile_specs(pos):
        return [pl.BlockSpec((bsz, ts, LRU_WIDTH), lambda t: (0, pos(t), 0)),
                pl.BlockSpec((bsz, HALO, LRU_WIDTH), lambda t: (0, jnp.maximum(pos(t) * per - 1, 0), 0)),
                pl.BlockSpec((bsz, HALO, LRU_WIDTH),
                             lambda t: (0, jnp.minimum((pos(t) + 1) * per, n_tok // HALO - 1), 0))]

    fwd = lambda t: t
    bwd = lambda t: n_tiles - 1 - t
    in_specs = tile_specs(fwd) + tile_specs(bwd) + [
        _resident((2, bsz, LRU_WIDTH)),
        _resident((CONV_WIDTH, LRU_WIDTH)),
        _resident((1, LRU_WIDTH)),
        _resident((2, len(GATE_K0), GATE_K, 2 * GATE_COLS)),
        _resident((2, 2, LRU_WIDTH)),
        _resident((2, 1, LRU_WIDTH)),
        _resident(perm.shape), _resident(perm_t.shape), _resident(halo.shape),
    ]
    out_specs = [pl.BlockSpec((bsz, ts, LRU_WIDTH), lambda t: (0, fwd(t), 0)),
                 pl.BlockSpec((bsz, ts, LRU_WIDTH), lambda t: (0, bwd(t), 0)),
                 pl.BlockSpec((2, bsz, LRU_WIDTH), lambda t: (0, 0, 0))]
    out_shape = [jax.ShapeDtypeStruct((bsz, n_tok, LRU_WIDTH), BF16),
                 jax.ShapeDtypeStruct((bsz, n_tok, LRU_WIDTH), BF16),
                 jax.ShapeDtypeStruct((2, bsz, LRU_WIDTH), F32)]
    return pl.pallas_call(
        functools.partial(_lru_kernel, n_tiles=n_tiles),
        grid=(n_tiles,), in_specs=in_specs, out_specs=out_specs, out_shape=out_shape,
        scratch_shapes=[pltpu.VMEM((2, ts + CONV_WIDTH - 1, bsz, LRU_WIDTH), F32),
                        pltpu.VMEM((2, ts, bsz, LRU_WIDTH), F32),
                        pltpu.VMEM((2, ts, bsz, LRU_WIDTH), F32),
                        pltpu.VMEM((2, bsz, LRU_WIDTH), F32)],
        compiler_params=pltpu.CompilerParams(dimension_semantics=("arbitrary",),
                                             vmem_limit_bytes=VMEM_LIMIT),
        name="lru",
    )(u, u, u, u, u, u, h0, 0.5 * conv_w, 0.5 * conv_b.reshape(1, LRU_WIDTH), w_gate, b_gate,
      lam.reshape(2, 1, LRU_WIDTH), perm, perm_t, halo)


def _cast_specs(weights, n_steps, step_of):
    in_specs, out_specs, out_shape = [], [], []
    for w in weights:
        rows, cols = w.shape
        n_blocks = rows // CAST_ROWS
        assert rows % CAST_ROWS == 0 and n_blocks <= n_steps
        index_map = lambda *idx, n_blocks=n_blocks: (jnp.minimum(step_of(*idx), n_blocks - 1), 0)
        in_specs.append(pl.BlockSpec((CAST_ROWS, cols), index_map))
        out_specs.append(pl.BlockSpec((CAST_ROWS, cols), index_map))
        out_shape.append(jax.ShapeDtypeStruct((rows, cols), BF16))
    return in_specs, out_specs, out_shape


def _cast_blocks(w_refs, o_refs):
    for w_ref, o_ref in zip(w_refs, o_refs):
        o_ref[...] = w_ref[...].astype(BF16)


def _attn_kernel(q_ref, kc_ref, vc_ref, kl_ref, vl_ref, *rest):
    n_cast = (len(rest) - 6) // 2
    w_refs, o_ref, wo_refs = rest[:n_cast], rest[n_cast], rest[n_cast + 1:2 * n_cast + 1]
    sa_s, sb_s, acc_s, m_s, l_s = rest[2 * n_cast + 1:]
    _cast_blocks(w_refs, wo_refs)
    n_chunks = kl_ref.shape[1] // KEY_CHUNK

    def scores(k, g):
        qg = q_ref[0, :, g * HEAD_DIM:(g + 1) * HEAD_DIM]
        return lax.dot_general(k, qg, (((1,), (1,)), ((), ())), preferred_element_type=F32)

    def softmax_pv(st, vt, g, first):
        m_c = jnp.max(st, axis=0, keepdims=True)
        if first:
            m_new = m_c
        else:
            m_old = m_s[g]
            m_new = jnp.maximum(m_old, m_c)
            alpha = jnp.exp2(m_old - m_new)
        p = jnp.exp2(st - m_new)
        p_sum = jnp.sum(p, axis=0, keepdims=True)
        pv = jnp.dot(vt, p.astype(BF16), preferred_element_type=F32)
        if first:
            l_s[g] = p_sum
            acc_s[g] = pv
        else:
            l_s[g] = alpha * l_s[g] + p_sum
            acc_s[g] = alpha * acc_s[g] + pv
        m_s[g] = m_new

    def lat_keys(c):
        return kl_ref[0, pl.ds(pl.multiple_of(c * KEY_CHUNK, KEY_CHUNK), KEY_CHUNK), :]

    def lat_vals(c):
        return vl_ref[0, :, pl.ds(pl.multiple_of(c * KEY_CHUNK, KEY_CHUNK), KEY_CHUNK)]

    def stage(k_next, st_in, st_out, vt, first=False):
        for g in range(GROUP):
            if k_next is not None:
                st_out[g] = scores(k_next, g)
            softmax_pv(st_in[g], vt, g, first)

    k0 = kl_ref[0, 0:KEY_CHUNK, :]
    for g in range(GROUP):
        sc = scores(kc_ref[0], g)
        sa_s[g] = scores(k0, g)
        softmax_pv(sc, vc_ref[0], g, True)

    def body(i, carry):
        stage(lat_keys(2 * i + 1), sa_s, sb_s, lat_vals(2 * i))
        stage(lat_keys(2 * i + 2), sb_s, sa_s, lat_vals(2 * i + 1))
        return carry

    lax.fori_loop(0, n_chunks // 2 - 1, body, 0)
    stage(lat_keys(n_chunks - 1), sa_s, sb_s, lat_vals(n_chunks - 2))
    stage(None, sb_s, None, lat_vals(n_chunks - 1))
    for g in range(GROUP):
        o = (acc_s[g] / l_s[g]).T
        o_ref[0, :, g * HEAD_DIM:(g + 1) * HEAD_DIM] = o.astype(BF16)


def _attn_call(q, k_ctx, vt_ctx, k_lat, vt_lat, *weights):
    bsz, n_tok, _ = q.shape
    n_ctx = k_ctx.shape[1]
    tq = Q_TILE
    gw = GROUP * HEAD_DIM
    n_qt = n_tok // tq
    keys = lambda n: pl.BlockSpec((1, n, HEAD_DIM), lambda b, j, i: (b, 0, j))
    vals = lambda n: pl.BlockSpec((1, HEAD_DIM, n), lambda b, j, i: (b, j, 0))
    w_in, w_out, w_shape = _cast_specs(weights, bsz * N_KV_HEADS * n_qt,
                                       lambda b, j, i: (b * N_KV_HEADS + j) * n_qt + i)
    return pl.pallas_call(
        _attn_kernel,
        grid=(bsz, N_KV_HEADS, n_qt),
        in_specs=[pl.BlockSpec((1, tq, gw), lambda b, j, i: (b, i, j)),
                  keys(n_ctx), vals(n_ctx), keys(n_tok), vals(n_tok)] + w_in,
        out_specs=[pl.BlockSpec((1, tq, gw), lambda b, j, i: (b, i, j))] + w_out,
        out_shape=[jax.ShapeDtypeStruct((bsz, n_tok, ATTN_WIDTH), BF16)] + w_shape,
        scratch_shapes=[pltpu.VMEM((GROUP, KEY_CHUNK, tq), F32),
                        pltpu.VMEM((GROUP, KEY_CHUNK, tq), F32),
                        pltpu.VMEM((GROUP, HEAD_DIM, tq), F32),
                        pltpu.VMEM((GROUP, 1, tq), F32),
                        pltpu.VMEM((GROUP, 1, tq), F32)],
        compiler_params=pltpu.CompilerParams(
            dimension_semantics=("arbitrary", "arbitrary", "arbitrary"),
            vmem_limit_bytes=VMEM_LIMIT),
        name="attn",
    )(q, k_ctx, vt_ctx, k_lat, vt_lat, *weights)


def _attn_fixed_shift_kernel(shift_ref, q_ref, kc_ref, vc_ref, kl_ref, vl_ref, *rest):
    n_cast = (len(rest) - 1) // 2
    w_refs, o_ref, wo_refs = rest[:n_cast], rest[n_cast], rest[n_cast + 1:]
    _cast_blocks(w_refs, wo_refs)
    tq = q_ref.shape[1]
    shift = shift_ref[0]
    q_all = jnp.concatenate([q_ref[0, :, g * HEAD_DIM:(g + 1) * HEAD_DIM] for g in range(GROUP)],
                            axis=0)

    def unnormalised(k, vt):
        st = lax.dot_general(k, q_all, (((1,), (1,)), ((), ())), preferred_element_type=F32)
        p = jnp.exp2(st - shift)
        pv = jnp.dot(vt, p.astype(BF16), preferred_element_type=F32)
        return jnp.sum(p, axis=0, keepdims=True), pv

    l_ctx, acc_ctx = unnormalised(kc_ref[0], vc_ref[0])
    l_lat, acc_lat = unnormalised(kl_ref[0], vl_ref[0])
    o_t = (acc_ctx + acc_lat) / (l_ctx + l_lat)
    for g in range(GROUP):
        o_ref[0, :, g * HEAD_DIM:(g + 1) * HEAD_DIM] = o_t[:, g * tq:(g + 1) * tq].T.astype(BF16)


def _attn_fixed_shift_call(shift, q, k_ctx, vt_ctx, k_lat, vt_lat, *weights):
    bsz, n_tok, _ = q.shape
    n_ctx = k_ctx.shape[1]
    tq = Q_TILE
    gw = GROUP * HEAD_DIM
    n_qt = n_tok // tq
    keys = lambda n: pl.BlockSpec((1, n, HEAD_DIM), lambda b, j, i: (b, 0, j))
    vals = lambda n: pl.BlockSpec((1, HEAD_DIM, n), lambda b, j, i: (b, j, 0))
    w_in, w_out, w_shape = _cast_specs(weights, bsz * N_KV_HEADS * n_qt,
                                       lambda b, j, i: (b * N_KV_HEADS + j) * n_qt + i)
    return pl.pallas_call(
        _attn_fixed_shift_kernel,
        grid=(bsz, N_KV_HEADS, n_qt),
        in_specs=[pl.BlockSpec(memory_space=pltpu.SMEM),
                  pl.BlockSpec((1, tq, gw), lambda b, j, i: (b, i, j)),
                  keys(n_ctx), vals(n_ctx), keys(n_tok), vals(n_tok)] + w_in,
        out_specs=[pl.BlockSpec((1, tq, gw), lambda b, j, i: (b, i, j))] + w_out,
        out_shape=[jax.ShapeDtypeStruct((bsz, n_tok, ATTN_WIDTH), BF16)] + w_shape,
        compiler_params=pltpu.CompilerParams(
            dimension_semantics=("arbitrary", "arbitrary", "arbitrary"),
            vmem_limit_bytes=VMEM_LIMIT),
        name="attn_fixed_shift",
    )(shift, q, k_ctx, vt_ctx, k_lat, vt_lat, *weights)


def _attention(q, k_ctx, vt_ctx, k_lat, vt_lat, q_gain, k_gain, weights):
    bound = (SCORE_BOUND_SLACK * np.float32(np.sqrt(HEAD_DIM) * np.log2(np.e))
             * jnp.max(jnp.abs(q_gain)) * jnp.max(jnp.abs(k_gain)))
    operands = (q, k_ctx, vt_ctx, k_lat, vt_lat, *weights)
    return lax.cond(bound <= MAX_FIXED_SHIFT,
                    lambda ops: _attn_fixed_shift_call(bound.reshape(1), *ops),
                    lambda ops: _attn_call(*ops), operands)


def _mix_out_kernel(hf_ref, hb_ref, g_ref, at_ref, ga_ref, gb_ref, x_ref, mod_ref,
                    wl_ref, wa_ref, wo_ref, o_ref):
    lru = hf_ref[0].astype(F32) + hb_ref[0].astype(F32)
    za = (lru * g_ref[0].astype(F32)).astype(BF16)
    ya = jnp.dot(za, wl_ref[...], preferred_element_type=F32)
    yb = jnp.dot(at_ref[0], wa_ref[...], preferred_element_type=F32)
    mix = (ga_ref[0].astype(F32) * ya + gb_ref[0].astype(F32) * yb).astype(BF16)
    y = jnp.dot(mix, wo_ref[...], preferred_element_type=F32)
    o_ref[0] = x_ref[0] + mod_ref[0, :, 2 * D_MODEL:3 * D_MODEL] * y


def _mix_out_call(h_fwd, h_bwd, gg, attn, sga, sgb, x, mod, w_lru_bf, w_attn_bf, w_out_bf):
    bsz, n_tok, _ = x.shape
    tl = TOKEN_TILE
    tok = lambda width: pl.BlockSpec((1, tl, width), lambda b, l: (b, l, 0))
    return pl.pallas_call(
        _mix_out_kernel,
        grid=(bsz, n_tok // tl),
        in_specs=[tok(LRU_WIDTH), tok(LRU_WIDTH), tok(LRU_WIDTH), tok(ATTN_WIDTH), tok(D_MODEL),
                  tok(D_MODEL), tok(D_MODEL),
                  pl.BlockSpec((1, 1, mod.shape[-1]), lambda b, l: (b, 0, 0)),
                  _resident((LRU_WIDTH, D_MODEL)), _resident((ATTN_WIDTH, D_MODEL)),
                  _resident((D_MODEL, D_MODEL))],
        out_specs=tok(D_MODEL),
        out_shape=jax.ShapeDtypeStruct((bsz, n_tok, D_MODEL), F32),
        compiler_params=pltpu.CompilerParams(dimension_semantics=("arbitrary", "arbitrary"),
                                             vmem_limit_bytes=VMEM_LIMIT),
        name="mix_out",
    )(h_fwd, h_bwd, gg, attn, sga, sgb, x, mod, w_lru_bf, w_attn_bf, w_out_bf)


def _ffn_kernel(x_ref, mod_ref, nrm_ref, wi_ref, wo_ref, o_ref):
    x = x_ref[0]
    shift = mod_ref[0, :, 3 * D_MODEL:4 * D_MODEL]
    scale = mod_ref[0, :, 4 * D_MODEL:5 * D_MODEL]
    gate_out = mod_ref[0, :, 5 * D_MODEL:6 * D_MODEL]
    h = ((x * _rms_scale(x) * nrm_ref[...]) * (1.0 + scale) + shift).astype(BF16)
    acc = jnp.zeros(x.shape, F32)
    for lo, hi in zip(FFN_SPLITS[:-1], FFN_SPLITS[1:]):
        gate = jnp.dot(h, wi_ref[:, lo:hi], preferred_element_type=F32)
        up = jnp.dot(h, wi_ref[:, FFN_HIDDEN + lo:FFN_HIDDEN + hi], preferred_element_type=F32)
        act = (gate * _sigmoid(gate) * up).astype(BF16)
        acc = acc + jnp.dot(act, wo_ref[lo:hi, :], preferred_element_type=F32)
    o_ref[0] = x + gate_out * acc


def _ffn_call(x, mod, norm_g, w_in_bf, w_out_bf):
    bsz, n_tok, _ = x.shape
    tl = TOKEN_TILE
    tok = pl.BlockSpec((1, tl, D_MODEL), lambda b, l: (b, l, 0))
    return pl.pallas_call(
        _ffn_kernel,
        grid=(bsz, n_tok // tl),
        in_specs=[tok, pl.BlockSpec((1, 1, mod.shape[-1]), lambda b, l: (b, 0, 0)),
                  _resident((1, D_MODEL)),
                  _resident((D_MODEL, 2 * FFN_HIDDEN)), _resident((FFN_HIDDEN, D_MODEL))],
        out_specs=tok,
        out_shape=jax.ShapeDtypeStruct((bsz, n_tok, D_MODEL), F32),
        compiler_params=pltpu.CompilerParams(dimension_semantics=("arbitrary", "arbitrary"),
                                             vmem_limit_bytes=VMEM_LIMIT),
        name="ffn",
    )(x, mod, norm_g.reshape(1, D_MODEL), w_in_bf, w_out_bf)


def _rope_tables(n_tok):
    pos = np.arange(n_tok)
    inv_freq = ROPE_THETA ** (-np.arange(0, ROPE_AXIS_DIM, 2, dtype=np.float64) / ROPE_AXIS_DIM)
    ang_r = (pos // GRID_W)[:, None] * inv_freq[None, :]
    ang_c = (pos % GRID_W)[:, None] * inv_freq[None, :]
    cr, sr, cc, sc = (f(a) for a in (ang_r, ang_c) for f in (np.cos, np.sin))
    zero = np.zeros_like(sr)
    cos = np.concatenate([cr, cr, cc, cc], axis=-1)
    sin_hi = np.concatenate([-sr, zero, -sc, zero], axis=-1)
    sin_lo = np.concatenate([zero, sr, zero, sc], axis=-1)
    return tuple(jnp.asarray(t, F32) for t in (cos, sin_hi, sin_lo))


def _gate_weights(wa, wx):
    def dense(w):
        rows = [jnp.pad(w[n], ((0, 0), (n * LRU_BLOCK_DIM, LRU_WIDTH - (n + 1) * LRU_BLOCK_DIM)))
                for n in range(LRU_BLOCKS)]
        return jnp.concatenate(rows, axis=0)

    da, dx = dense(wa), dense(wx)
    tiles = []
    for j, k0 in enumerate(GATE_K0):
        cols = slice(j * GATE_COLS, (j + 1) * GATE_COLS)
        tiles.append(jnp.concatenate([da[k0:k0 + GATE_K, cols], dx[k0:k0 + GATE_K, cols]], axis=1))
    return jnp.stack(tiles).astype(BF16)


def kernel(x, c, ctx, c_ctx, w_mod, b_mod, norm_mix, w_in, conv_w, conv_b, lru_wa, lru_ba, lru_wx,
           lru_bx, lru_lambda, q_norm, k_norm, w_out_lru, w_out_attn, w_out, norm_ffn, w_ffn_in,
           w_ffn_out):
    bsz, n_tok, _ = x.shape
    assert w_mod.shape[0] == 1, "single trunk layer"
    tables = _rope_tables(n_tok)

    c_all = jnp.concatenate([c, c_ctx[None, :]], axis=0)
    c_all = jnp.pad(c_all, ((0, -c_all.shape[0] % SUBLANES), (0, 0)))
    mod_all = _mod_call(c_all, w_mod[0], b_mod[0])
    mod = mod_all[:bsz].reshape(bsz, 1, -1)
    mod_c = mod_all[bsz:bsz + 1].reshape(1, 1, -1)

    col_scale = np.where(np.arange(IN_COLS) >= OFF_GA, 0.5, 1.0).astype(np.float32)
    w_in_bf = (w_in[0] * col_scale).astype(BF16)
    u_c, k_c, vt_c = _in_proj_call(ctx, mod_c, norm_mix[0], w_in_bf, None, k_norm[0], None, latent=False)
    u_l, gg, q, k_l, vt_l, sga, sgb = _in_proj_call(x, mod, norm_mix[0], w_in_bf, q_norm[0], k_norm[0],
                                                    tables, latent=True)

    w_gate = jnp.stack([_gate_weights(lru_wa[0, d], lru_wx[0, d]) for d in range(2)])
    b_gate = 0.5 * jnp.stack([lru_ba[0], lru_bx[0]], axis=1)
    lru = functools.partial(_lru_call, conv_w=conv_w[0], conv_b=conv_b[0], w_gate=w_gate, b_gate=b_gate,
                            lam=lru_lambda[0])
    _, _, h_seed = lru(u_c, jnp.zeros((2, bsz, LRU_WIDTH), F32))
    h_fwd, h_bwd, _ = lru(u_l, h_seed)

    later_weights = (w_out_lru[0], w_out_attn[0], w_out[0], w_ffn_in[0], w_ffn_out[0])
    attn, w_lru_bf, w_attn_bf, w_out_bf, w_ffn_in_bf, w_ffn_out_bf = _attention(
        q, k_c, vt_c, k_l, vt_l, q_norm[0], k_norm[0], later_weights)

    x1 = _mix_out_call(h_fwd, h_bwd, gg, attn, sga, sgb, x, mod, w_lru_bf, w_attn_bf, w_out_bf)
    return _ffn_call(x1, mod, norm_ffn[0], w_ffn_in_bf, w_ffn_out_bf)
```

```python
import functools

import jax
import jax.numpy as jnp
import numpy as np
from jax import lax
from jax.experimental import pallas as pl
from jax.experimental.pallas import tpu as pltpu

D_MODEL = 1024
GRID_W = 64
EPS = 1e-6
LRU_WIDTH = 1280
LRU_BLOCKS = 8
LRU_BLOCK_DIM = LRU_WIDTH // LRU_BLOCKS
LRU_C = 8.0
CONV_WIDTH = 4
CONV_LEFT = 2
HEAD_DIM = 128
LANES = 128
SUBLANES = 8
N_HEADS = 8
N_KV_HEADS = 2
GROUP = N_HEADS // N_KV_HEADS
ATTN_WIDTH = N_HEADS * HEAD_DIM
KV_WIDTH = N_KV_HEADS * HEAD_DIM
ROPE_AXIS_DIM = HEAD_DIM // 2
ROPE_THETA = 10000.0
FFN_HIDDEN = 2816

OFF_U = 0
OFF_G = OFF_U + LRU_WIDTH
OFF_Q = OFF_G + LRU_WIDTH
OFF_K = OFF_Q + ATTN_WIDTH
OFF_V = OFF_K + KV_WIDTH
OFF_GA = OFF_V + KV_WIDTH
OFF_GB = OFF_GA + D_MODEL
IN_COLS = OFF_GB + D_MODEL

MOD_COLS_TILE = 1024
TOKEN_TILE = 512
SCAN_TILE = 64
PERM_STEPS = 32
HALO = 16
GATE_COLS = 256
GATE_K = 512
Q_TILE = 256
CAST_ROWS = 16
KEY_CHUNK = 512
MXU_TILE = 256
FFN_SPLITS = (0, 6 * MXU_TILE, FFN_HIDDEN)
SCORE_BOUND_SLACK = 1.02
MAX_FIXED_SHIFT = 60.0
VMEM_LIMIT = 56 * 1024 * 1024

F32 = jnp.float32
BF16 = jnp.bfloat16


def _gate_k_offsets():
    offs = []
    for j in range(LRU_WIDTH // GATE_COLS):
        first_row = (j * GATE_COLS // LRU_BLOCK_DIM) * LRU_BLOCK_DIM
        last_row = ((j + 1) * GATE_COLS - 1) // LRU_BLOCK_DIM * LRU_BLOCK_DIM + LRU_BLOCK_DIM
        k0 = min(first_row // LANES * LANES, LRU_WIDTH - GATE_K)
        assert k0 <= first_row and last_row <= k0 + GATE_K
        offs.append(k0)
    return tuple(offs)


GATE_K0 = _gate_k_offsets()


def _sigmoid(x):
    return 0.5 * jnp.tanh(0.5 * x) + 0.5


def _gelu_tanh(x):
    c = np.float32(np.sqrt(2.0 / np.pi))
    inner = x * (c + np.float32(0.044715) * c * (x * x))
    return (0.5 * x) * (1.0 + jnp.tanh(inner))


def _rms_scale(x):
    return lax.rsqrt(jnp.mean(x * x, axis=-1, keepdims=True) + EPS)


def _resident(shape):
    nd = len(shape)
    return pl.BlockSpec(shape, lambda *_: (0,) * nd, pipeline_mode=pl.Buffered(1))


def _mod_kernel(c_ref, w_ref, b_ref, o_ref):
    c = c_ref[...]
    s = c * _sigmoid(c)
    o_ref[...] = jnp.dot(s, w_ref[...], preferred_element_type=F32,
                         precision=lax.Precision.HIGHEST) + b_ref[...]


def _mod_call(c_all, w_mod, b_mod):
    rows = c_all.shape[0]
    n = w_mod.shape[1]
    tn = MOD_COLS_TILE
    return pl.pallas_call(
        _mod_kernel,
        grid=(n // tn,),
        in_specs=[pl.BlockSpec((rows, D_MODEL), lambda j: (0, 0)),
                  pl.BlockSpec((D_MODEL, tn), lambda j: (0, j)),
                  pl.BlockSpec((1, tn), lambda j: (0, j))],
        out_specs=pl.BlockSpec((rows, tn), lambda j: (0, j)),
        out_shape=jax.ShapeDtypeStruct((rows, n), F32),
        compiler_params=pltpu.CompilerParams(dimension_semantics=("arbitrary",),
                                             vmem_limit_bytes=VMEM_LIMIT),
        name="mod",
    )(c_all, w_mod, b_mod.reshape(1, n))


def _gained_tables(gain, tables, scale):
    half = ROPE_AXIS_DIM // 2
    if tables is None:
        return (gain * scale,)
    cos, sin_hi, sin_lo = tables
    return (cos * (gain * scale), sin_hi * (pltpu.roll(gain, HEAD_DIM - half, 1) * scale),
            sin_lo * (pltpu.roll(gain, half, 1) * scale))


def _head_norm_rope(t, gained):
    half = ROPE_AXIS_DIM // 2
    n = t * _rms_scale(t)
    if len(gained) == 1:
        return n * gained[0]
    cos, sin_hi, sin_lo = gained
    return (n * cos + pltpu.roll(n, HEAD_DIM - half, 1) * sin_hi + pltpu.roll(n, half, 1) * sin_lo)


def _in_proj_kernel(*refs, latent, n_row_tiles, n_steps):
    if latent:
        (x0_ref, xn_ref, mod_ref, nrm_ref, w_ref, qn_ref, kn_ref, rc_ref, rh_ref, rl_ref,
         u_ref, g_ref, q_ref, k_ref, vt_ref, ga_ref, gb_ref, h_s) = refs
        tables = (rc_ref[...], rh_ref[...], rl_ref[...])
    else:
        x0_ref, xn_ref, mod_ref, nrm_ref, w_ref, kn_ref, u_ref, k_ref, vt_ref, h_s = refs
        tables = None
    step = pl.program_id(0) * n_row_tiles + pl.program_id(1)

    def normed(x, b):
        shift = mod_ref[b, :, 0:D_MODEL]
        scale = mod_ref[b, :, D_MODEL:2 * D_MODEL]
        return ((x * _rms_scale(x) * nrm_ref[...]) * (1.0 + scale) + shift).astype(BF16)

    @pl.when(step == 0)
    def _():
        h_s[...] = normed(x0_ref[0], 0)

    h = h_s[...]

    def proj(off, width):
        return jnp.dot(h, w_ref[:, off:off + width], preferred_element_type=F32)

    if latent:
        qq = proj(OFF_Q, ATTN_WIDTH)
        q_tables = _gained_tables(qn_ref[...], tables, np.float32(np.log2(np.e) / np.sqrt(HEAD_DIM)))
        for j in range(N_HEADS):
            sl = slice(j * HEAD_DIM, (j + 1) * HEAD_DIM)
            q_ref[0, :, sl] = _head_norm_rope(qq[:, sl], q_tables).astype(BF16)
    kk = proj(OFF_K, KV_WIDTH)
    k_tables = _gained_tables(kn_ref[...], tables, np.float32(1.0))
    for j in range(N_KV_HEADS):
        sl = slice(j * HEAD_DIM, (j + 1) * HEAD_DIM)
        k_ref[0, :, sl] = _head_norm_rope(kk[:, sl], k_tables).astype(BF16)
    if latent:
        g_ref[0] = _gelu_tanh(proj(OFF_G, LRU_WIDTH)).astype(BF16)
        ga_ref[0] = (0.5 * jnp.tanh(proj(OFF_GA, D_MODEL)) + 0.5).astype(BF16)
        gb_ref[0] = (0.5 * jnp.tanh(proj(OFF_GB, D_MODEL)) + 0.5).astype(BF16)
    vt_ref[0] = proj(OFF_V, KV_WIDTH).T.astype(BF16)
    u_ref[0] = proj(OFF_U, LRU_WIDTH).astype(BF16)

    nxt = jnp.minimum(step + 1, n_steps - 1)
    b_next = nxt // n_row_tiles if mod_ref.shape[0] > 1 else 0
    h_s[...] = normed(xn_ref[0], b_next)


def _next_tile_map(n_row_tiles, n_steps):
    def index_map(b, l):
        nxt = jnp.minimum(b * n_row_tiles + l + 1, n_steps - 1)
        return (nxt // n_row_tiles, nxt % n_row_tiles, 0)
    return index_map


def _in_proj_call(x, mod, norm_g, w_in_bf, q_gain, k_gain, tables, latent):
    bsz, n_tok, _ = x.shape
    tl = min(TOKEN_TILE, n_tok)
    n_row_tiles = n_tok // tl
    grid = (bsz, n_row_tiles)
    n_steps = bsz * n_row_tiles

    tok = lambda width: pl.BlockSpec((1, tl, width), lambda b, l: (b, l, 0))
    in_specs = [pl.BlockSpec((1, tl, D_MODEL), lambda b, l: (0, 0, 0)),
                pl.BlockSpec((1, tl, D_MODEL), _next_tile_map(n_row_tiles, n_steps)),
                _resident(mod.shape),
                _resident((1, D_MODEL)),
                _resident((D_MODEL, IN_COLS))]
    args = [x, x, mod, norm_g.reshape(1, D_MODEL), w_in_bf]
    vt_spec = pl.BlockSpec((1, KV_WIDTH, tl), lambda b, l: (b, 0, l))
    vt_shape = jax.ShapeDtypeStruct((bsz, KV_WIDTH, n_tok), BF16)
    bf = lambda width: jax.ShapeDtypeStruct((bsz, n_tok, width), BF16)
    if latent:
        in_specs += [_resident((1, HEAD_DIM)), _resident((1, HEAD_DIM))]
        in_specs += [pl.BlockSpec((tl, HEAD_DIM), lambda b, l: (l, 0))] * 3
        args += [q_gain.reshape(1, HEAD_DIM), k_gain.reshape(1, HEAD_DIM), *tables]
        out_specs = [tok(LRU_WIDTH), tok(LRU_WIDTH), tok(ATTN_WIDTH), tok(KV_WIDTH), vt_spec,
                     tok(D_MODEL), tok(D_MODEL)]
        out_shape = [bf(LRU_WIDTH), bf(LRU_WIDTH), bf(ATTN_WIDTH), bf(KV_WIDTH), vt_shape,
                     bf(D_MODEL), bf(D_MODEL)]
    else:
        in_specs += [_resident((1, HEAD_DIM))]
        args += [k_gain.reshape(1, HEAD_DIM)]
        out_specs = [tok(LRU_WIDTH), tok(KV_WIDTH), vt_spec]
        out_shape = [bf(LRU_WIDTH), bf(KV_WIDTH), vt_shape]
    return pl.pallas_call(
        functools.partial(_in_proj_kernel, latent=latent, n_row_tiles=n_row_tiles, n_steps=n_steps),
        grid=grid, in_specs=in_specs, out_specs=out_specs, out_shape=out_shape,
        scratch_shapes=[pltpu.VMEM((tl, D_MODEL), BF16)],
        compiler_params=pltpu.CompilerParams(dimension_semantics=("arbitrary", "arbitrary"),
                                             vmem_limit_bytes=VMEM_LIMIT),
        name="in_proj_latent" if latent else "in_proj_ctx",
    )(*args)


def _lru_kernel(uf_ref, pf_ref, nf_ref, ub_ref, pb_ref, nb_ref, h0_ref, cw_ref, cb_ref, wg_ref, bg_ref,
                lam_ref, perm_ref, permt_ref, halo_ref, hf_ref, hb_ref, hlast_ref,
                ext_s, a_s, b_s, carry_s, *, n_tiles):
    t = pl.program_id(0)
    bsz, ts, _ = uf_ref.shape
    dirs = ((uf_ref, pf_ref, nf_ref, hf_ref, t), (ub_ref, pb_ref, nb_ref, hb_ref, n_tiles - 1 - t))

    @pl.when(t == 0)
    def _():
        carry_s[...] = h0_ref[...]

    def load_time_major(d):
        u_ref, prev_ref, next_ref, _, tile = dirs[d]
        for p in range(ts // PERM_STEPS):
            steps = slice(p * PERM_STEPS, (p + 1) * PERM_STEPS)
            u_bm = u_ref[:, steps, :].reshape(bsz * PERM_STEPS, LRU_WIDTH)
            u_tm = jnp.dot(perm_ref[...], u_bm, preferred_element_type=F32)
            ext_s[d, CONV_LEFT + p * PERM_STEPS:CONV_LEFT + (p + 1) * PERM_STEPS] = (
                u_tm.reshape(PERM_STEPS, bsz, LRU_WIDTH))
        edge = jnp.concatenate([prev_ref[:, HALO // 2:, :].reshape(bsz * HALO // 2, LRU_WIDTH),
                                next_ref[:, :HALO // 2, :].reshape(bsz * HALO // 2, LRU_WIDTH)], axis=0)
        halo = jnp.dot(halo_ref[...], edge, preferred_element_type=F32)
        ext_s[d, 0:CONV_LEFT] = (halo[0:CONV_LEFT * bsz]
                                 * jnp.where(tile > 0, 1.0, 0.0)).reshape(CONV_LEFT, bsz, LRU_WIDTH)
        ext_s[d, CONV_LEFT + ts:] = (halo[CONV_LEFT * bsz:(CONV_LEFT + 1) * bsz]
                                     * jnp.where(tile < n_tiles - 1, 1.0, 0.0)).reshape(1, bsz, LRU_WIDTH)

    def coefficients(d):
        xc = cb_ref[...][None]
        for j in range(CONV_WIDTH):
            xc = xc + ext_s[d, j:j + ts] * cw_ref[j:j + 1, :][None]
        xh = xc.reshape(ts * bsz, LRU_WIDTH)
        xb = xh.astype(BF16)
        lam = lam_ref[d]
        sp = jnp.maximum(-lam, 0.0) + jnp.log1p(jnp.exp(-jnp.abs(lam)))
        decay = sp * np.float32(-0.5 * LRU_C * np.log2(np.e))
        for j, k0 in enumerate(GATE_K0):
            cols = slice(j * GATE_COLS, (j + 1) * GATE_COLS)
            gates = jnp.dot(xb[:, k0:k0 + GATE_K], wg_ref[d, j], preferred_element_type=F32)
            t_r = jnp.tanh(gates[:, :GATE_COLS] + bg_ref[d, 0:1, cols])
            t_i = jnp.tanh(gates[:, GATE_COLS:] + bg_ref[d, 1:2, cols])
            a = jnp.exp2(decay[:, cols] * t_r + decay[:, cols])
            om = 1.0 - a * a
            coef = om * lax.rsqrt(jnp.maximum(om, 1e-30))
            a_s[d, :, :, cols] = a.reshape(ts, bsz, GATE_COLS)
            b_s[d, :, :, cols] = (coef * ((t_i + 1.0) * xh[:, cols])).reshape(ts, bsz, GATE_COLS)

    def scan(d):
        h = carry_s[d]
        for s in range(ts):
            idx = (ts - 1 - s) if d == 1 else s
            h = a_s[d, idx] * h + b_s[d, idx]
            b_s[d, idx] = h
        carry_s[d] = h

    def store_batch_major(d):
        h_ref = dirs[d][3]
        for p in range(ts // PERM_STEPS):
            steps = slice(p * PERM_STEPS, (p + 1) * PERM_STEPS)
            h_tm = b_s[d, steps].reshape(PERM_STEPS * bsz, LRU_WIDTH).astype(BF16)
            h_bm = jnp.dot(permt_ref[...], h_tm, preferred_element_type=F32)
            h_ref[:, steps, :] = h_bm.reshape(bsz, PERM_STEPS, LRU_WIDTH).astype(BF16)

    load_time_major(0)
    load_time_major(1)
    coefficients(0)
    scan(0)
    coefficients(1)
    store_batch_major(0)
    scan(1)
    store_batch_major(1)

    @pl.when(t == n_tiles - 1)
    def _():
        hlast_ref[...] = carry_s[...]


def _perm_matrices(bsz):
    n = bsz * PERM_STEPS
    r_out = np.arange(n)
    s, b = r_out // bsz, r_out % bsz
    perm = np.zeros((n, n), np.float32)
    perm[r_out, b * PERM_STEPS + s] = 1.0
    half = HALO // 2
    halo = np.zeros((CONV_WIDTH * bsz, 2 * bsz * half), np.float32)
    for b_ in range(bsz):
        for k in range(CONV_LEFT):
            halo[k * bsz + b_, b_ * half + half - CONV_LEFT + k] = 1.0
        halo[CONV_LEFT * bsz + b_, bsz * half + b_ * half] = 1.0
    return jnp.asarray(perm, BF16), jnp.asarray(perm.T, BF16), jnp.asarray(halo, BF16)


def _lru_call(u, h0, conv_w, conv_b, w_gate, b_gate, lam):
    bsz, n_tok, _ = u.shape
    ts = SCAN_TILE
    n_tiles = n_tok // ts
    per = ts // HALO
    perm, perm_t, halo = _perm_matrices(bsz)

    def tile_specs(pos):
        return [pl.BlockSpec((bsz, ts, LRU_WIDTH), lambda t: (0, pos(t), 0)),
                pl.BlockSpec((bsz, HALO, LRU_WIDTH), lambda t: (0, jnp.maximum(pos(t) * per - 1, 0), 0)),
                pl.BlockSpec((bsz, HALO, LRU_WIDTH),
                             lambda t: (0, jnp.minimum((pos(t) + 1) * per, n_tok // HALO - 1), 0))]

    fwd = lambda t: t
    bwd = lambda t: n_tiles - 1 - t
    in_specs = tile_specs(fwd) + tile_specs(bwd) + [
        _resident((2, bsz, LRU_WIDTH)),
        _resident((CONV_WIDTH, LRU_WIDTH)),
        _resident((1, LRU_WIDTH)),
        _resident((2, len(GATE_K0), GATE_K, 2 * GATE_COLS)),
        _resident((2, 2, LRU_WIDTH)),
        _resident((2, 1, LRU_WIDTH)),
        _resident(perm.shape), _resident(perm_t.shape), _resident(halo.shape),
    ]
    out_specs = [pl.BlockSpec((bsz, ts, LRU_WIDTH), lambda t: (0, fwd(t), 0)),
                 pl.BlockSpec((bsz, ts, LRU_WIDTH), lambda t: (0, bwd(t), 0)),
                 pl.BlockSpec((2, bsz, LRU_WIDTH), lambda t: (0, 0, 0))]
    out_shape = [jax.ShapeDtypeStruct((bsz, n_tok, LRU_WIDTH), BF16),
                 jax.ShapeDtypeStruct((bsz, n_tok, LRU_WIDTH), BF16),
                 jax.ShapeDtypeStruct((2, bsz, LRU_WIDTH), F32)]
    return pl.pallas_call(
        functools.partial(_lru_kernel, n_tiles=n_tiles),
        grid=(n_tiles,), in_specs=in_specs, out_specs=out_specs, out_shape=out_shape,
        scratch_shapes=[pltpu.VMEM((2, ts + CONV_WIDTH - 1, bsz, LRU_WIDTH), F32),
                        pltpu.VMEM((2, ts, bsz, LRU_WIDTH), F32),
                        pltpu.VMEM((2, ts, bsz, LRU_WIDTH), F32),
                        pltpu.VMEM((2, bsz, LRU_WIDTH), F32)],
        compiler_params=pltpu.CompilerParams(dimension_semantics=("arbitrary",),
                                             vmem_limit_bytes=VMEM_LIMIT),
        name="lru",
    )(u, u, u, u, u, u, h0, 0.5 * conv_w, 0.5 * conv_b.reshape(1, LRU_WIDTH), w_gate, b_gate,
      lam.reshape(2, 1, LRU_WIDTH), perm, perm_t, halo)


def _cast_specs(weights, n_steps, step_of):
    in_specs, out_specs, out_shape = [], [], []
    for w in weights:
        rows, cols = w.shape
        n_blocks = rows // CAST_ROWS
        assert rows % CAST_ROWS == 0 and n_blocks <= n_steps
        index_map = lambda *idx, n_blocks=n_blocks: (jnp.minimum(step_of(*idx), n_blocks - 1), 0)
        in_specs.append(pl.BlockSpec((CAST_ROWS, cols), index_map))
        out_specs.append(pl.BlockSpec((CAST_ROWS, cols), index_map))
        out_shape.append(jax.ShapeDtypeStruct((rows, cols), BF16))
    return in_specs, out_specs, out_shape


def _cast_blocks(w_refs, o_refs):
    for w_ref, o_ref in zip(w_refs, o_refs):
        o_ref[...] = w_ref[...].astype(BF16)


def _attn_kernel(q_ref, kc_ref, vc_ref, kl_ref, vl_ref, *rest):
    n_cast = (len(rest) - 6) // 2
    w_refs, o_ref, wo_refs = rest[:n_cast], rest[n_cast], rest[n_cast + 1:2 * n_cast + 1]
    sa_s, sb_s, acc_s, m_s, l_s = rest[2 * n_cast + 1:]
    _cast_blocks(w_refs, wo_refs)
    n_chunks = kl_ref.shape[1] // KEY_CHUNK

    def scores(k, g):
        qg = q_ref[0, :, g * HEAD_DIM:(g + 1) * HEAD_DIM]
        return lax.dot_general(k, qg, (((1,), (1,)), ((), ())), preferred_element_type=F32)

    def softmax_pv(st, vt, g, first):
        m_c = jnp.max(st, axis=0, keepdims=True)
        if first:
            m_new = m_c
        else:
            m_old = m_s[g]
            m_new = jnp.maximum(m_old, m_c)
            alpha = jnp.exp2(m_old - m_new)
        p = jnp.exp2(st - m_new)
        p_sum = jnp.sum(p, axis=0, keepdims=True)
        pv = jnp.dot(vt, p.astype(BF16), preferred_element_type=F32)
        if first:
            l_s[g] = p_sum
            acc_s[g] = pv
        else:
            l_s[g] = alpha * l_s[g] + p_sum
            acc_s[g] = alpha * acc_s[g] + pv
        m_s[g] = m_new

    def lat_keys(c):
        return kl_ref[0, pl.ds(pl.multiple_of(c * KEY_CHUNK, KEY_CHUNK), KEY_CHUNK), :]

    def lat_vals(c):
        return vl_ref[0, :, pl.ds(pl.multiple_of(c * KEY_CHUNK, KEY_CHUNK), KEY_CHUNK)]

    def stage(k_next, st_in, st_out, vt, first=False):
        for g in range(GROUP):
            if k_next is not None:
                st_out[g] = scores(k_next, g)
            softmax_pv(st_in[g], vt, g, first)

    k0 = kl_ref[0, 0:KEY_CHUNK, :]
    for g in range(GROUP):
        sc = scores(kc_ref[0], g)
        sa_s[g] = scores(k0, g)
        softmax_pv(sc, vc_ref[0], g, True)

    def body(i, carry):
        stage(lat_keys(2 * i + 1), sa_s, sb_s, lat_vals(2 * i))
        stage(lat_keys(2 * i + 2), sb_s, sa_s, lat_vals(2 * i + 1))
        return carry

    lax.fori_loop(0, n_chunks // 2 - 1, body, 0)
    stage(lat_keys(n_chunks - 1), sa_s, sb_s, lat_vals(n_chunks - 2))
    stage(None, sb_s, None, lat_vals(n_chunks - 1))
    for g in range(GROUP):
        o = (acc_s[g] / l_s[g]).T
        o_ref[0, :, g * HEAD_DIM:(g + 1) * HEAD_DIM] = o.astype(BF16)


def _attn_call(q, k_ctx, vt_ctx, k_lat, vt_lat, *weights):
    bsz, n_tok, _ = q.shape
    n_ctx = k_ctx.shape[1]
    tq = Q_TILE
    gw = GROUP * HEAD_DIM
    n_qt = n_tok // tq
    keys = lambda n: pl.BlockSpec((1, n, HEAD_DIM), lambda b, j, i: (b, 0, j))
    vals = lambda n: pl.BlockSpec((1, HEAD_DIM, n), lambda b, j, i: (b, j, 0))
    w_in, w_out, w_shape = _cast_specs(weights, bsz * N_KV_HEADS * n_qt,
                                       lambda b, j, i: (b * N_KV_HEADS + j) * n_qt + i)
    return pl.pallas_call(
        _attn_kernel,
        grid=(bsz, N_KV_HEADS, n_qt),
        in_specs=[pl.BlockSpec((1, tq, gw), lambda b, j, i: (b, i, j)),
                  keys(n_ctx), vals(n_ctx), keys(n_tok), vals(n_tok)] + w_in,
        out_specs=[pl.BlockSpec((1, tq, gw), lambda b, j, i: (b, i, j))] + w_out,
        out_shape=[jax.ShapeDtypeStruct((bsz, n_tok, ATTN_WIDTH), BF16)] + w_shape,
        scratch_shapes=[pltpu.VMEM((GROUP, KEY_CHUNK, tq), F32),
                        pltpu.VMEM((GROUP, KEY_CHUNK, tq), F32),
                        pltpu.VMEM((GROUP, HEAD_DIM, tq), F32),
                        pltpu.VMEM((GROUP, 1, tq), F32),
                        pltpu.VMEM((GROUP, 1, tq), F32)],
        compiler_params=pltpu.CompilerParams(
            dimension_semantics=("arbitrary", "arbitrary", "arbitrary"),
            vmem_limit_bytes=VMEM_LIMIT),
        name="attn",
    )(q, k_ctx, vt_ctx, k_lat, vt_lat, *weights)


def _attn_fixed_shift_kernel(shift_ref, q_ref, kc_ref, vc_ref, kl_ref, vl_ref, *rest):
    n_cast = (len(rest) - 1) // 2
    w_refs, o_ref, wo_refs = rest[:n_cast], rest[n_cast], rest[n_cast + 1:]
    _cast_blocks(w_refs, wo_refs)
    tq = q_ref.shape[1]
    shift = shift_ref[0]
    q_all = jnp.concatenate([q_ref[0, :, g * HEAD_DIM:(g + 1) * HEAD_DIM] for g in range(GROUP)],
                            axis=0)

    def unnormalised(k, vt):
        st = lax.dot_general(k, q_all, (((1,), (1,)), ((), ())), preferred_element_type=F32)
        p = jnp.exp2(st - shift)
        pv = jnp.dot(vt, p.astype(BF16), preferred_element_type=F32)
        return jnp.sum(p, axis=0, keepdims=True), pv

    l_ctx, acc_ctx = unnormalised(kc_ref[0], vc_ref[0])
    l_lat, acc_lat = unnormalised(kl_ref[0], vl_ref[0])
    o_t = (acc_ctx + acc_lat) / (l_ctx + l_lat)
    for g in range(GROUP):
        o_ref[0, :, g * HEAD_DIM:(g + 1) * HEAD_DIM] = o_t[:, g * tq:(g + 1) * tq].T.astype(BF16)


def _attn_fixed_shift_call(shift, q, k_ctx, vt_ctx, k_lat, vt_lat, *weights):
    bsz, n_tok, _ = q.shape
    n_ctx = k_ctx.shape[1]
    tq = Q_TILE
    gw = GROUP * HEAD_DIM
    n_qt = n_tok // tq
    keys = lambda n: pl.BlockSpec((1, n, HEAD_DIM), lambda b, j, i: (b, 0, j))
    vals = lambda n: pl.BlockSpec((1, HEAD_DIM, n), lambda b, j, i: (b, j, 0))
    w_in, w_out, w_shape = _cast_specs(weights, bsz * N_KV_HEADS * n_qt,
                                       lambda b, j, i: (b * N_KV_HEADS + j) * n_qt + i)
    return pl.pallas_call(
        _attn_fixed_shift_kernel,
        grid=(bsz, N_KV_HEADS, n_qt),
        in_specs=[pl.BlockSpec(memory_space=pltpu.SMEM),
                  pl.BlockSpec((1, tq, gw), lambda b, j, i: (b, i, j)),
                  keys(n_ctx), vals(n_ctx), keys(n_tok), vals(n_tok)] + w_in,
        out_specs=[pl.BlockSpec((1, tq, gw), lambda b, j, i: (b, i, j))] + w_out,
        out_shape=[jax.ShapeDtypeStruct((bsz, n_tok, ATTN_WIDTH), BF16)] + w_shape,
        compiler_params=pltpu.CompilerParams(
            dimension_semantics=("arbitrary", "arbitrary", "arbitrary"),
            vmem_limit_bytes=VMEM_LIMIT),
        name="attn_fixed_shift",
    )(shift, q, k_ctx, vt_ctx, k_lat, vt_lat, *weights)


def _attention(q, k_ctx, vt_ctx, k_lat, vt_lat, q_gain, k_gain, weights):
    bound = (SCORE_BOUND_SLACK * np.float32(np.sqrt(HEAD_DIM) * np.log2(np.e))
             * jnp.max(jnp.abs(q_gain)) * jnp.max(jnp.abs(k_gain)))
    operands = (q, k_ctx, vt_ctx, k_lat, vt_lat, *weights)
    return lax.cond(bound <= MAX_FIXED_SHIFT,
                    lambda ops: _attn_fixed_shift_call(bound.reshape(1), *ops),
                    lambda ops: _attn_call(*ops), operands)


def _mix_out_kernel(hf_ref, hb_ref, g_ref, at_ref, ga_ref, gb_ref, mod_ref,
                    wl_ref, wa_ref, wo_ref, o_ref):
    lru = hf_ref[0].astype(F32) + hb_ref[0].astype(F32)
    za = (lru * g_ref[0].astype(F32)).astype(BF16)
    ya = jnp.dot(za, wl_ref[...], preferred_element_type=F32)
    yb = jnp.dot(at_ref[0], wa_ref[...], preferred_element_type=F32)
    mix = (ga_ref[0].astype(F32) * ya + gb_ref[0].astype(F32) * yb).astype(BF16)
    y = jnp.dot(mix, wo_ref[...], preferred_element_type=F32)
    o_ref[0] = (mod_ref[0, :, 2 * D_MODEL:3 * D_MODEL] * y).astype(BF16)


def _mix_out_call(h_fwd, h_bwd, gg, attn, sga, sgb, mod, w_lru_bf, w_attn_bf, w_out_bf):
    bsz, n_tok, _ = attn.shape
    tl = TOKEN_TILE
    tok = lambda width: pl.BlockSpec((1, tl, width), lambda b, l: (b, l, 0))
    return pl.pallas_call(
        _mix_out_kernel,
        grid=(bsz, n_tok // tl),
        in_specs=[tok(LRU_WIDTH), tok(LRU_WIDTH), tok(LRU_WIDTH), tok(ATTN_WIDTH), tok(D_MODEL),
                  tok(D_MODEL),
                  pl.BlockSpec((1, 1, mod.shape[-1]), lambda b, l: (b, 0, 0)),
                  _resident((LRU_WIDTH, D_MODEL)), _resident((ATTN_WIDTH, D_MODEL)),
                  _resident((D_MODEL, D_MODEL))],
        out_specs=tok(D_MODEL),
        out_shape=jax.ShapeDtypeStruct((bsz, n_tok, D_MODEL), BF16),
        compiler_params=pltpu.CompilerParams(dimension_semantics=("arbitrary", "arbitrary"),
                                             vmem_limit_bytes=VMEM_LIMIT),
        name="mix_out",
    )(h_fwd, h_bwd, gg, attn, sga, sgb, mod, w_lru_bf, w_attn_bf, w_out_bf)


def _ffn_kernel(x_ref, dy_ref, mod_ref, nrm_ref, wi_ref, wo_ref, o_ref):
    x = x_ref[0] + dy_ref[0].astype(F32)
    shift = mod_ref[0, :, 3 * D_MODEL:4 * D_MODEL]
    scale = mod_ref[0, :, 4 * D_MODEL:5 * D_MODEL]
    gate_out = mod_ref[0, :, 5 * D_MODEL:6 * D_MODEL]
    h = ((x * _rms_scale(x) * nrm_ref[...]) * (1.0 + scale) + shift).astype(BF16)
    acc = jnp.zeros(x.shape, F32)
    for lo, hi in zip(FFN_SPLITS[:-1], FFN_SPLITS[1:]):
        gate = jnp.dot(h, wi_ref[:, lo:hi], preferred_element_type=F32)
        up = jnp.dot(h, wi_ref[:, FFN_HIDDEN + lo:FFN_HIDDEN + hi], preferred_element_type=F32)
        act = (gate * _sigmoid(gate) * up).astype(BF16)
        acc = acc + jnp.dot(act, wo_ref[lo:hi, :], preferred_element_type=F32)
    o_ref[0] = x + gate_out * acc


def _ffn_call(x, dy, mod, norm_g, w_in_bf, w_out_bf):
    bsz, n_tok, _ = x.shape
    tl = TOKEN_TILE
    tok = pl.BlockSpec((1, tl, D_MODEL), lambda b, l: (b, l, 0))
    return pl.pallas_call(
        _ffn_kernel,
        grid=(bsz, n_tok // tl),
        in_specs=[tok, tok, pl.BlockSpec((1, 1, mod.shape[-1]), lambda b, l: (b, 0, 0)),
                  _resident((1, D_MODEL)),
                  _resident((D_MODEL, 2 * FFN_HIDDEN)), _resident((FFN_HIDDEN, D_MODEL))],
        out_specs=tok,
        out_shape=jax.ShapeDtypeStruct((bsz, n_tok, D_MODEL), F32),
        compiler_params=pltpu.CompilerParams(dimension_semantics=("arbitrary", "arbitrary"),
                                             vmem_limit_bytes=VMEM_LIMIT),
        name="ffn",
    )(x, dy, mod, norm_g.reshape(1, D_MODEL), w_in_bf, w_out_bf)


def _rope_tables(n_tok):
    pos = np.arange(n_tok)
    inv_freq = ROPE_THETA ** (-np.arange(0, ROPE_AXIS_DIM, 2, dtype=np.float64) / ROPE_AXIS_DIM)
    ang_r = (pos // GRID_W)[:, None] * inv_freq[None, :]
    ang_c = (pos % GRID_W)[:, None] * inv_freq[None, :]
    cr, sr, cc, sc = (f(a) for a in (ang_r, ang_c) for f in (np.cos, np.sin))
    zero = np.zeros_like(sr)
    cos = np.concatenate([cr, cr, cc, cc], axis=-1)
    sin_hi = np.concatenate([-sr, zero, -sc, zero], axis=-1)
    sin_lo = np.concatenate([zero, sr, zero, sc], axis=-1)
    return tuple(jnp.asarray(t, F32) for t in (cos, sin_hi, sin_lo))


def _gate_weights(wa, wx):
    def dense(w):
        rows = [jnp.pad(w[n], ((0, 0), (n * LRU_BLOCK_DIM, LRU_WIDTH - (n + 1) * LRU_BLOCK_DIM)))
                for n in range(LRU_BLOCKS)]
        return jnp.concatenate(rows, axis=0)

    da, dx = dense(wa), dense(wx)
    tiles = []
    for j, k0 in enumerate(GATE_K0):
        cols = slice(j * GATE_COLS, (j + 1) * GATE_COLS)
        tiles.append(jnp.concatenate([da[k0:k0 + GATE_K, cols], dx[k0:k0 + GATE_K, cols]], axis=1))
    return jnp.stack(tiles).astype(BF16)


def kernel(x, c, ctx, c_ctx, w_mod, b_mod, norm_mix, w_in, conv_w, conv_b, lru_wa, lru_ba, lru_wx,
           lru_bx, lru_lambda, q_norm, k_norm, w_out_lru, w_out_attn, w_out, norm_ffn, w_ffn_in,
           w_ffn_out):
    bsz, n_tok, _ = x.shape
    assert w_mod.shape[0] == 1, "single trunk layer"
    tables = _rope_tables(n_tok)

    c_all = jnp.concatenate([c, c_ctx[None, :]], axis=0)
    c_all = jnp.pad(c_all, ((0, -c_all.shape[0] % SUBLANES), (0, 0)))
    mod_all = _mod_call(c_all, w_mod[0], b_mod[0])
    mod = mod_all[:bsz].reshape(bsz, 1, -1)
    mod_c = mod_all[bsz:bsz + 1].reshape(1, 1, -1)

    col_scale = np.where(np.arange(IN_COLS) >= OFF_GA, 0.5, 1.0).astype(np.float32)
    w_in_bf = (w_in[0] * col_scale).astype(BF16)
    u_c, k_c, vt_c = _in_proj_call(ctx, mod_c, norm_mix[0], w_in_bf, None, k_norm[0], None, latent=False)
    u_l, gg, q, k_l, vt_l, sga, sgb = _in_proj_call(x, mod, norm_mix[0], w_in_bf, q_norm[0], k_norm[0],
                                                    tables, latent=True)

    w_gate = jnp.stack([_gate_weights(lru_wa[0, d], lru_wx[0, d]) for d in range(2)])
    b_gate = 0.5 * jnp.stack([lru_ba[0], lru_bx[0]], axis=1)
    lru = functools.partial(_lru_call, conv_w=conv_w[0], conv_b=conv_b[0], w_gate=w_gate, b_gate=b_gate,
                            lam=lru_lambda[0])
    _, _, h_seed = lru(u_c, jnp.zeros((2, bsz, LRU_WIDTH), F32))
    h_fwd, h_bwd, _ = lru(u_l, h_seed)

    later_weights = (w_out_lru[0], w_out_attn[0], w_out[0], w_ffn_in[0], w_ffn_out[0])
    attn, w_lru_bf, w_attn_bf, w_out_bf, w_ffn_in_bf, w_ffn_out_bf = _attention(
        q, k_c, vt_c, k_l, vt_l, q_norm[0], k_norm[0], later_weights)

    dy = _mix_out_call(h_fwd, h_bwd, gg, attn, sga, sgb, mod, w_lru_bf, w_attn_bf, w_out_bf)
    return _ffn_call(x, dy, mod, norm_ffn[0], w_ffn_in_bf, w_ffn_out_bf)
```

```python
import functools

import jax
import jax.numpy as jnp
import numpy as np
from jax import lax
from jax.experimental import pallas as pl
from jax.experimental.pallas import tpu as pltpu

D_MODEL = 1024
GRID_W = 64
EPS = 1e-6
LRU_WIDTH = 1280
LRU_BLOCKS = 8
LRU_BLOCK_DIM = LRU_WIDTH // LRU_BLOCKS
LRU_C = 8.0
CONV_WIDTH = 4
CONV_LEFT = 2
HEAD_DIM = 128
LANES = 128
SUBLANES = 8
N_HEADS = 8
N_KV_HEADS = 2
GROUP = N_HEADS // N_KV_HEADS
ATTN_WIDTH = N_HEADS * HEAD_DIM
KV_WIDTH = N_KV_HEADS * HEAD_DIM
ROPE_AXIS_DIM = HEAD_DIM // 2
ROPE_THETA = 10000.0
FFN_HIDDEN = 2816

OFF_U = 0
OFF_G = OFF_U + LRU_WIDTH
OFF_Q = OFF_G + LRU_WIDTH
OFF_K = OFF_Q + ATTN_WIDTH
OFF_V = OFF_K + KV_WIDTH
OFF_GA = OFF_V + KV_WIDTH
OFF_GB = OFF_GA + D_MODEL
IN_COLS = OFF_GB + D_MODEL

MOD_COLS_TILE = 1024
TOKEN_TILE = 512
SCAN_TILE = 64
PERM_STEPS = 32
HALO = 16
GATE_COLS = 256
GATE_K = 512
Q_TILE = 256
FIXED_SHIFT_Q_TILES = 2
CAST_ROWS = 64
KEY_CHUNK = 512
MXU_TILE = 256
FFN_SPLITS = (0, 6 * MXU_TILE, FFN_HIDDEN)
SCORE_BOUND_SLACK = 1.02
MAX_FIXED_SHIFT = 60.0
VMEM_LIMIT = 56 * 1024 * 1024

F32 = jnp.float32
BF16 = jnp.bfloat16


def _gate_k_offsets():
    offs = []
    for j in range(LRU_WIDTH // GATE_COLS):
        first_row = (j * GATE_COLS // LRU_BLOCK_DIM) * LRU_BLOCK_DIM
        last_row = ((j + 1) * GATE_COLS - 1) // LRU_BLOCK_DIM * LRU_BLOCK_DIM + LRU_BLOCK_DIM
        k0 = min(first_row // LANES * LANES, LRU_WIDTH - GATE_K)
        assert k0 <= first_row and last_row <= k0 + GATE_K
        offs.append(k0)
    return tuple(offs)


GATE_K0 = _gate_k_offsets()


def _sigmoid(x):
    return 0.5 * jnp.tanh(0.5 * x) + 0.5


def _gelu_tanh(x):
    c = np.float32(np.sqrt(2.0 / np.pi))
    inner = x * (c + np.float32(0.044715) * c * (x * x))
    return (0.5 * x) * (1.0 + jnp.tanh(inner))


def _rms_scale(x):
    return lax.rsqrt(jnp.mean(x * x, axis=-1, keepdims=True) + EPS)


def _resident(shape):
    nd = len(shape)
    return pl.BlockSpec(shape, lambda *_: (0,) * nd, pipeline_mode=pl.Buffered(1))


def _mod_kernel(c_ref, w_ref, b_ref, o_ref):
    c = c_ref[...]
    s = c * _sigmoid(c)
    o_ref[...] = jnp.dot(s, w_ref[...], preferred_element_type=F32,
                         precision=lax.Precision.HIGHEST) + b_ref[...]


def _mod_call(c_all, w_mod, b_mod):
    rows = c_all.shape[0]
    n = w_mod.shape[1]
    tn = MOD_COLS_TILE
    return pl.pallas_call(
        _mod_kernel,
        grid=(n // tn,),
        in_specs=[pl.BlockSpec((rows, D_MODEL), lambda j: (0, 0)),
                  pl.BlockSpec((D_MODEL, tn), lambda j: (0, j)),
                  pl.BlockSpec((1, tn), lambda j: (0, j))],
        out_specs=pl.BlockSpec((rows, tn), lambda j: (0, j)),
        out_shape=jax.ShapeDtypeStruct((rows, n), F32),
        compiler_params=pltpu.CompilerParams(dimension_semantics=("arbitrary",),
                                             vmem_limit_bytes=VMEM_LIMIT),
        name="mod",
    )(c_all, w_mod, b_mod.reshape(1, n))


def _gained_tables(gain, tables, scale):
    half = ROPE_AXIS_DIM // 2
    if tables is None:
        return (gain * scale,)
    cos, sin_hi, sin_lo = tables
    return (cos * (gain * scale), sin_hi * (pltpu.roll(gain, HEAD_DIM - half, 1) * scale),
            sin_lo * (pltpu.roll(gain, half, 1) * scale))


def _head_norm_rope(t, gained):
    half = ROPE_AXIS_DIM // 2
    n = t * _rms_scale(t)
    if len(gained) == 1:
        return n * gained[0]
    cos, sin_hi, sin_lo = gained
    return (n * cos + pltpu.roll(n, HEAD_DIM - half, 1) * sin_hi + pltpu.roll(n, half, 1) * sin_lo)


def _in_proj_kernel(*refs, latent, n_row_tiles, n_steps):
    if latent:
        (x0_ref, xn_ref, mod_ref, nrm_ref, w_ref, qn_ref, kn_ref, rc_ref, rh_ref, rl_ref,
         u_ref, g_ref, q_ref, k_ref, vt_ref, ga_ref, gb_ref, h_s) = refs
        tables = (rc_ref[...], rh_ref[...], rl_ref[...])
    else:
        x0_ref, xn_ref, mod_ref, nrm_ref, w_ref, kn_ref, u_ref, k_ref, vt_ref, h_s = refs
        tables = None
    step = pl.program_id(0) * n_row_tiles + pl.program_id(1)

    def normed(x, b):
        shift = mod_ref[b, :, 0:D_MODEL]
        scale = mod_ref[b, :, D_MODEL:2 * D_MODEL]
        return ((x * _rms_scale(x) * nrm_ref[...]) * (1.0 + scale) + shift).astype(BF16)

    @pl.when(step == 0)
    def _():
        h_s[...] = normed(x0_ref[0], 0)

    h = h_s[...]

    def proj(off, width):
        return jnp.dot(h, w_ref[:, off:off + width], preferred_element_type=F32)

    if latent:
        qq = proj(OFF_Q, ATTN_WIDTH)
        q_tables = _gained_tables(qn_ref[...], tables, np.float32(np.log2(np.e) / np.sqrt(HEAD_DIM)))
        for j in range(N_HEADS):
            sl = slice(j * HEAD_DIM, (j + 1) * HEAD_DIM)
            q_ref[0, :, sl] = _head_norm_rope(qq[:, sl], q_tables).astype(BF16)
    kk = proj(OFF_K, KV_WIDTH)
    k_tables = _gained_tables(kn_ref[...], tables, np.float32(1.0))
    for j in range(N_KV_HEADS):
        sl = slice(j * HEAD_DIM, (j + 1) * HEAD_DIM)
        k_ref[0, :, sl] = _head_norm_rope(kk[:, sl], k_tables).astype(BF16)
    if latent:
        g_ref[0] = _gelu_tanh(proj(OFF_G, LRU_WIDTH)).astype(BF16)
        ga_ref[0] = (0.5 * jnp.tanh(proj(OFF_GA, D_MODEL)) + 0.5).astype(BF16)
        gb_ref[0] = (0.5 * jnp.tanh(proj(OFF_GB, D_MODEL)) + 0.5).astype(BF16)
    vt_ref[0] = proj(OFF_V, KV_WIDTH).T.astype(BF16)
    u_ref[0] = proj(OFF_U, LRU_WIDTH).astype(BF16)

    nxt = jnp.minimum(step + 1, n_steps - 1)
    b_next = nxt // n_row_tiles if mod_ref.shape[0] > 1 else 0
    h_s[...] = normed(xn_ref[0], b_next)


def _next_tile_map(n_row_tiles, n_steps):
    def index_map(b, l):
        nxt = jnp.minimum(b * n_row_tiles + l + 1, n_steps - 1)
        return (nxt // n_row_tiles, nxt % n_row_tiles, 0)
    return index_map


def _in_proj_call(x, mod, norm_g, w_in_bf, q_gain, k_gain, tables, latent):
    bsz, n_tok, _ = x.shape
    tl = min(TOKEN_TILE, n_tok)
    n_row_tiles = n_tok // tl
    grid = (bsz, n_row_tiles)
    n_steps = bsz * n_row_tiles

    tok = lambda width: pl.BlockSpec((1, tl, width), lambda b, l: (b, l, 0))
    in_specs = [pl.BlockSpec((1, tl, D_MODEL), lambda b, l: (0, 0, 0)),
                pl.BlockSpec((1, tl, D_MODEL), _next_tile_map(n_row_tiles, n_steps)),
                _resident(mod.shape),
                _resident((1, D_MODEL)),
                _resident((D_MODEL, IN_COLS))]
    args = [x, x, mod, norm_g.reshape(1, D_MODEL), w_in_bf]
    vt_spec = pl.BlockSpec((1, KV_WIDTH, tl), lambda b, l: (b, 0, l))
    vt_shape = jax.ShapeDtypeStruct((bsz, KV_WIDTH, n_tok), BF16)
    bf = lambda width: jax.ShapeDtypeStruct((bsz, n_tok, width), BF16)
    if latent:
        in_specs += [_resident((1, HEAD_DIM)), _resident((1, HEAD_DIM))]
        in_specs += [pl.BlockSpec((tl, HEAD_DIM), lambda b, l: (l, 0))] * 3
        args += [q_gain.reshape(1, HEAD_DIM), k_gain.reshape(1, HEAD_DIM), *tables]
        out_specs = [tok(LRU_WIDTH), tok(LRU_WIDTH), tok(ATTN_WIDTH), tok(KV_WIDTH), vt_spec,
                     tok(D_MODEL), tok(D_MODEL)]
        out_shape = [bf(LRU_WIDTH), bf(LRU_WIDTH), bf(ATTN_WIDTH), bf(KV_WIDTH), vt_shape,
                     bf(D_MODEL), bf(D_MODEL)]
    else:
        in_specs += [_resident((1, HEAD_DIM))]
        args += [k_gain.reshape(1, HEAD_DIM)]
        out_specs = [tok(LRU_WIDTH), tok(KV_WIDTH), vt_spec]
        out_shape = [bf(LRU_WIDTH), bf(KV_WIDTH), vt_shape]
    return pl.pallas_call(
        functools.partial(_in_proj_kernel, latent=latent, n_row_tiles=n_row_tiles, n_steps=n_steps),
        grid=grid, in_specs=in_specs, out_specs=out_specs, out_shape=out_shape,
        scratch_shapes=[pltpu.VMEM((tl, D_MODEL), BF16)],
        compiler_params=pltpu.CompilerParams(dimension_semantics=("arbitrary", "arbitrary"),
                                             vmem_limit_bytes=VMEM_LIMIT),
        name="in_proj_latent" if latent else "in_proj_ctx",
    )(*args)


def _lru_kernel(uf_ref, pf_ref, nf_ref, ub_ref, pb_ref, nb_ref, h0_ref, cw_ref, cb_ref, wg_ref, bg_ref,
                lam_ref, perm_ref, permt_ref, halo_ref, hf_ref, hb_ref, hlast_ref,
                ext_s, a_s, b_s, carry_s, *, n_tiles):
    t = pl.program_id(0)
    bsz, ts, _ = uf_ref.shape
    dirs = ((uf_ref, pf_ref, nf_ref, hf_ref, t), (ub_ref, pb_ref, nb_ref, hb_ref, n_tiles - 1 - t))

    @pl.when(t == 0)
    def _():
        carry_s[...] = h0_ref[...]

    def load_time_major(d):
        u_ref, prev_ref, next_ref, _, tile = dirs[d]
        for p in range(ts // PERM_STEPS):
            steps = slice(p * PERM_STEPS, (p + 1) * PERM_STEPS)
            u_bm = u_ref[:, steps, :].reshape(bsz * PERM_STEPS, LRU_WIDTH)
            u_tm = jnp.dot(perm_ref[...], u_bm, preferred_element_type=F32)
            ext_s[d, CONV_LEFT + p * PERM_STEPS:CONV_LEFT + (p + 1) * PERM_STEPS] = (
                u_tm.reshape(PERM_STEPS, bsz, LRU_WIDTH))
        edge = jnp.concatenate([prev_ref[:, HALO // 2:, :].reshape(bsz * HALO // 2, LRU_WIDTH),
                                next_ref[:, :HALO // 2, :].reshape(bsz * HALO // 2, LRU_WIDTH)], axis=0)
        halo = jnp.dot(halo_ref[...], edge, preferred_element_type=F32)
        ext_s[d, 0:CONV_LEFT] = (halo[0:CONV_LEFT * bsz]
                                 * jnp.where(tile > 0, 1.0, 0.0)).reshape(CONV_LEFT, bsz, LRU_WIDTH)
        ext_s[d, CONV_LEFT + ts:] = (halo[CONV_LEFT * bsz:(CONV_LEFT + 1) * bsz]
                                     * jnp.where(tile < n_tiles - 1, 1.0, 0.0)).reshape(1, bsz, LRU_WIDTH)

    def coefficients(d):
        xc = cb_ref[...][None]
        for j in range(CONV_WIDTH):
            xc = xc + ext_s[d, j:j + ts] * cw_ref[j:j + 1, :][None]
        xh = xc.reshape(ts * bsz, LRU_WIDTH)
        xb = xh.astype(BF16)
        lam = lam_ref[d]
        sp = jnp.maximum(-lam, 0.0) + jnp.log1p(jnp.exp(-jnp.abs(lam)))
        decay = sp * np.float32(-0.5 * LRU_C * np.log2(np.e))
        for j, k0 in enumerate(GATE_K0):
            cols = slice(j * GATE_COLS, (j + 1) * GATE_COLS)
            gates = jnp.dot(xb[:, k0:k0 + GATE_K], wg_ref[d, j], preferred_element_type=F32)
            t_r = jnp.tanh(gates[:, :GATE_COLS] + bg_ref[d, 0:1, cols])
            t_i = jnp.tanh(gates[:, GATE_COLS:] + bg_ref[d, 1:2, cols])
            a = jnp.exp2(decay[:, cols] * t_r + decay[:, cols])
            om = 1.0 - a * a
            coef = om * lax.rsqrt(jnp.maximum(om, 1e-30))
            a_s[d, :, :, cols] = a.reshape(ts, bsz, GATE_COLS)
            b_s[d, :, :, cols] = (coef * ((t_i + 1.0) * xh[:, cols])).reshape(ts, bsz, GATE_COLS)

    def scan(d):
        h = carry_s[d]
        for s in range(ts):
            idx = (ts - 1 - s) if d == 1 else s
            h = a_s[d, idx] * h + b_s[d, idx]
            b_s[d, idx] = h
        carry_s[d] = h

    def store_batch_major(d):
        h_ref = dirs[d][3]
        for p in range(ts // PERM_STEPS):
            steps = slice(p * PERM_STEPS, (p + 1) * PERM_STEPS)
            h_tm = b_s[d, steps].reshape(PERM_STEPS * bsz, LRU_WIDTH).astype(BF16)
            h_bm = jnp.dot(permt_ref[...], h_tm, preferred_element_type=F32)
            h_ref[:, steps, :] = h_bm.reshape(bsz, PERM_STEPS, LRU_WIDTH).astype(BF16)

    load_time_major(0)
    load_time_major(1)
    coefficients(0)
    scan(0)
    coefficients(1)
    store_batch_major(0)
    scan(1)
    store_batch_major(1)

    @pl.when(t == n_tiles - 1)
    def _():
        hlast_ref[...] = carry_s[...]


def _perm_matrices(bsz):
    n = bsz * PERM_STEPS
    r_out = np.arange(n)
    s, b = r_out // bsz, r_out % bsz
    perm = np.zeros((n, n), np.float32)
    perm[r_out, b * PERM_STEPS + s] = 1.0
    half = HALO // 2
    halo = np.zeros((CONV_WIDTH * bsz, 2 * bsz * half), np.float32)
    for b_ in range(bsz):
        for k in range(CONV_LEFT):
            halo[k * bsz + b_, b_ * half + half - CONV_LEFT + k] = 1.0
        halo[CONV_LEFT * bsz + b_, bsz * half + b_ * half] = 1.0
    return jnp.asarray(perm, BF16), jnp.asarray(perm.T, BF16), jnp.asarray(halo, BF16)


def _lru_call(u, h0, conv_w, conv_b, w_gate, b_gate, lam):
    bsz, n_tok, _ = u.shape
    ts = SCAN_TILE
    n_tiles = n_tok // ts
    per = ts // HALO
    perm, perm_t, halo = _perm_matrices(bsz)

    def tile_specs(pos):
        return [pl.BlockSpec((bsz, ts, LRU_WIDTH), lambda t: (0, pos(t), 0)),
                pl.BlockSpec((bsz, HALO, LRU_WIDTH), lambda t: (0, jnp.maximum(pos(t) * per - 1, 0), 0)),
                pl.BlockSpec((bsz, HALO, LRU_WIDTH),
                             lambda t: (0, jnp.minimum((pos(t) + 1) * per, n_tok // HALO - 1), 0))]

    fwd = lambda t: t
    bwd = lambda t: n_tiles - 1 - t
    in_specs = tile_specs(fwd) + tile_specs(bwd) + [
        _resident((2, bsz, LRU_WIDTH)),
        _resident((CONV_WIDTH, LRU_WIDTH)),
        _resident((1, LRU_WIDTH)),
        _resident((2, len(GATE_K0), GATE_K, 2 * GATE_COLS)),
        _resident((2, 2, LRU_WIDTH)),
        _resident((2, 1, LRU_WIDTH)),
        _resident(perm.shape), _resident(perm_t.shape), _resident(halo.shape),
    ]
    out_specs = [pl.BlockSpec((bsz, ts, LRU_WIDTH), lambda t: (0, fwd(t), 0)),
                 pl.BlockSpec((bsz, ts, LRU_WIDTH), lambda t: (0, bwd(t), 0)),
                 pl.BlockSpec((2, bsz, LRU_WIDTH), lambda t: (0, 0, 0))]
    out_shape = [jax.ShapeDtypeStruct((bsz, n_tok, LRU_WIDTH), BF16),
                 jax.ShapeDtypeStruct((bsz, n_tok, LRU_WIDTH), BF16),
                 jax.ShapeDtypeStruct((2, bsz, LRU_WIDTH), F32)]
    return pl.pallas_call(
        functools.partial(_lru_kernel, n_tiles=n_tiles),
        grid=(n_tiles,), in_specs=in_specs, out_specs=out_specs, out_shape=out_shape,
        scratch_shapes=[pltpu.VMEM((2, ts + CONV_WIDTH - 1, bsz, LRU_WIDTH), F32),
                        pltpu.VMEM((2, ts, bsz, LRU_WIDTH), F32),
                        pltpu.VMEM((2, ts, bsz, LRU_WIDTH), F32),
                        pltpu.VMEM((2, bsz, LRU_WIDTH), F32)],
        compiler_params=pltpu.CompilerParams(dimension_semantics=("arbitrary",),
                                             vmem_limit_bytes=VMEM_LIMIT),
        name="lru",
    )(u, u, u, u, u, u, h0, 0.5 * conv_w, 0.5 * conv_b.reshape(1, LRU_WIDTH), w_gate, b_gate,
      lam.reshape(2, 1, LRU_WIDTH), perm, perm_t, halo)


def _cast_specs(weights, n_steps, step_of):
    in_specs, out_specs, out_shape = [], [], []
    for w in weights:
        rows, cols = w.shape
        n_blocks = rows // CAST_ROWS
        assert rows % CAST_ROWS == 0 and n_blocks <= n_steps
        index_map = lambda *idx, n_blocks=n_blocks: (jnp.minimum(step_of(*idx), n_blocks - 1), 0)
        in_specs.append(pl.BlockSpec((CAST_ROWS, cols), index_map))
        out_specs.append(pl.BlockSpec((CAST_ROWS, cols), index_map))
        out_shape.append(jax.ShapeDtypeStruct((rows, cols), BF16))
    return in_specs, out_specs, out_shape


def _cast_blocks(w_refs, o_refs):
    for w_ref, o_ref in zip(w_refs, o_refs):
        o_ref[...] = w_ref[...].astype(BF16)


def _attn_kernel(q_ref, kc_ref, vc_ref, kl_ref, vl_ref, *rest):
    n_cast = (len(rest) - 6) // 2
    w_refs, o_ref, wo_refs = rest[:n_cast], rest[n_cast], rest[n_cast + 1:2 * n_cast + 1]
    sa_s, sb_s, acc_s, m_s, l_s = rest[2 * n_cast + 1:]
    _cast_blocks(w_refs, wo_refs)
    n_chunks = kl_ref.shape[1] // KEY_CHUNK

    def scores(k, g):
        qg = q_ref[0, :, g * HEAD_DIM:(g + 1) * HEAD_DIM]
        return lax.dot_general(k, qg, (((1,), (1,)), ((), ())), preferred_element_type=F32)

    def softmax_pv(st, vt, g, first):
        m_c = jnp.max(st, axis=0, keepdims=True)
        if first:
            m_new = m_c
        else:
            m_old = m_s[g]
            m_new = jnp.maximum(m_old, m_c)
            alpha = jnp.exp2(m_old - m_new)
        p = jnp.exp2(st - m_new)
        p_sum = jnp.sum(p, axis=0, keepdims=True)
        pv = jnp.dot(vt, p.astype(BF16), preferred_element_type=F32)
        if first:
            l_s[g] = p_sum
            acc_s[g] = pv
        else:
            l_s[g] = alpha * l_s[g] + p_sum
            acc_s[g] = alpha * acc_s[g] + pv
        m_s[g] = m_new

    def lat_keys(c):
        return kl_ref[0, pl.ds(pl.multiple_of(c * KEY_CHUNK, KEY_CHUNK), KEY_CHUNK), :]

    def lat_vals(c):
        return vl_ref[0, :, pl.ds(pl.multiple_of(c * KEY_CHUNK, KEY_CHUNK), KEY_CHUNK)]

    def stage(k_next, st_in, st_out, vt, first=False):
        for g in range(GROUP):
            if k_next is not None:
                st_out[g] = scores(k_next, g)
            softmax_pv(st_in[g], vt, g, first)

    k0 = kl_ref[0, 0:KEY_CHUNK, :]
    for g in range(GROUP):
        sc = scores(kc_ref[0], g)
        sa_s[g] = scores(k0, g)
        softmax_pv(sc, vc_ref[0], g, True)

    def body(i, carry):
        stage(lat_keys(2 * i + 1), sa_s, sb_s, lat_vals(2 * i))
        stage(lat_keys(2 * i + 2), sb_s, sa_s, lat_vals(2 * i + 1))
        return carry

    lax.fori_loop(0, n_chunks // 2 - 1, body, 0)
    stage(lat_keys(n_chunks - 1), sa_s, sb_s, lat_vals(n_chunks - 2))
    stage(None, sb_s, None, lat_vals(n_chunks - 1))
    for g in range(GROUP):
        o = (acc_s[g] / l_s[g]).T
        o_ref[0, :, g * HEAD_DIM:(g + 1) * HEAD_DIM] = o.astype(BF16)


def _attn_call(q, k_ctx, vt_ctx, k_lat, vt_lat, *weights):
    bsz, n_tok, _ = q.shape
    n_ctx = k_ctx.shape[1]
    tq = Q_TILE
    gw = GROUP * HEAD_DIM
    n_qt = n_tok // tq
    keys = lambda n: pl.BlockSpec((1, n, HEAD_DIM), lambda b, j, i: (b, 0, j))
    vals = lambda n: pl.BlockSpec((1, HEAD_DIM, n), lambda b, j, i: (b, j, 0))
    w_in, w_out, w_shape = _cast_specs(weights, bsz * N_KV_HEADS * n_qt,
                                       lambda b, j, i: (b * N_KV_HEADS + j) * n_qt + i)
    return pl.pallas_call(
        _attn_kernel,
        grid=(bsz, N_KV_HEADS, n_qt),
        in_specs=[pl.BlockSpec((1, tq, gw), lambda b, j, i: (b, i, j)),
                  keys(n_ctx), vals(n_ctx), keys(n_tok), vals(n_tok)] + w_in,
        out_specs=[pl.BlockSpec((1, tq, gw), lambda b, j, i: (b, i, j))] + w_out,
        out_shape=[jax.ShapeDtypeStruct((bsz, n_tok, ATTN_WIDTH), BF16)] + w_shape,
        scratch_shapes=[pltpu.VMEM((GROUP, KEY_CHUNK, tq), F32),
                        pltpu.VMEM((GROUP, KEY_CHUNK, tq), F32),
                        pltpu.VMEM((GROUP, HEAD_DIM, tq), F32),
                        pltpu.VMEM((GROUP, 1, tq), F32),
                        pltpu.VMEM((GROUP, 1, tq), F32)],
        compiler_params=pltpu.CompilerParams(
            dimension_semantics=("arbitrary", "arbitrary", "arbitrary"),
            vmem_limit_bytes=VMEM_LIMIT),
        name="attn",
    )(q, k_ctx, vt_ctx, k_lat, vt_lat, *weights)


def _attn_fixed_shift_kernel(shift_ref, q_ref, kc_ref, vc_ref, kl_ref, vl_ref, *rest):
    n_cast = (len(rest) - 1) // 2
    w_refs, o_ref, wo_refs = rest[:n_cast], rest[n_cast], rest[n_cast + 1:]
    _cast_blocks(w_refs, wo_refs)
    tq = Q_TILE
    shift = shift_ref[0]
    for t in range(q_ref.shape[1] // tq):
        rows = slice(t * tq, (t + 1) * tq)
        for j in range(N_KV_HEADS):
            base = j * GROUP * HEAD_DIM
            kv = slice(j * HEAD_DIM, (j + 1) * HEAD_DIM)
            q_all = jnp.concatenate([q_ref[0, rows, base + g * HEAD_DIM:base + (g + 1) * HEAD_DIM]
                                     for g in range(GROUP)], axis=0)

            def unnormalised(k, vt):
                st = lax.dot_general(k, q_all, (((1,), (1,)), ((), ())), preferred_element_type=F32)
                p = jnp.exp2(st - shift)
                pv = jnp.dot(vt, p.astype(BF16), preferred_element_type=F32)
                return jnp.sum(p, axis=0, keepdims=True), pv

            l_ctx, acc_ctx = unnormalised(kc_ref[0, :, kv], vc_ref[0, kv, :])
            l_lat, acc_lat = unnormalised(kl_ref[0, :, kv], vl_ref[0, kv, :])
            o_t = (acc_ctx + acc_lat) / (l_ctx + l_lat)
            for g in range(GROUP):
                o_ref[0, rows, base + g * HEAD_DIM:base + (g + 1) * HEAD_DIM] = (
                    o_t[:, g * tq:(g + 1) * tq].T.astype(BF16))


def _attn_fixed_shift_call(shift, q, k_ctx, vt_ctx, k_lat, vt_lat, *weights):
    bsz, n_tok, _ = q.shape
    n_ctx = k_ctx.shape[1]
    rows = FIXED_SHIFT_Q_TILES * Q_TILE
    n_qt = n_tok // rows
    keys = lambda n: pl.BlockSpec((1, n, KV_WIDTH), lambda b, i: (b, 0, 0))
    vals = lambda n: pl.BlockSpec((1, KV_WIDTH, n), lambda b, i: (b, 0, 0))
    w_in, w_out, w_shape = _cast_specs(weights, bsz * n_qt, lambda b, i: b * n_qt + i)
    return pl.pallas_call(
        _attn_fixed_shift_kernel,
        grid=(bsz, n_qt),
        in_specs=[pl.BlockSpec(memory_space=pltpu.SMEM),
                  pl.BlockSpec((1, rows, ATTN_WIDTH), lambda b, i: (b, i, 0)),
                  keys(n_ctx), vals(n_ctx), keys(n_tok), vals(n_tok)] + w_in,
        out_specs=[pl.BlockSpec((1, rows, ATTN_WIDTH), lambda b, i: (b, i, 0))] + w_out,
        out_shape=[jax.ShapeDtypeStruct((bsz, n_tok, ATTN_WIDTH), BF16)] + w_shape,
        compiler_params=pltpu.CompilerParams(
            dimension_semantics=("arbitrary", "arbitrary"),
            vmem_limit_bytes=VMEM_LIMIT),
        name="attn_fixed_shift",
    )(shift, q, k_ctx, vt_ctx, k_lat, vt_lat, *weights)


def _attention(q, k_ctx, vt_ctx, k_lat, vt_lat, q_gain, k_gain, weights):
    bound = (SCORE_BOUND_SLACK * np.float32(np.sqrt(HEAD_DIM) * np.log2(np.e))
             * jnp.max(jnp.abs(q_gain)) * jnp.max(jnp.abs(k_gain)))
    operands = (q, k_ctx, vt_ctx, k_lat, vt_lat, *weights)
    return lax.cond(bound <= MAX_FIXED_SHIFT,
                    lambda ops: _attn_fixed_shift_call(bound.reshape(1), *ops),
                    lambda ops: _attn_call(*ops), operands)


def _mix_out_kernel(hf_ref, hb_ref, g_ref, at_ref, ga_ref, gb_ref, x_ref, mod_ref,
                    wl_ref, wa_ref, wo_ref, o_ref):
    lru = hf_ref[0].astype(F32) + hb_ref[0].astype(F32)
    za = (lru * g_ref[0].astype(F32)).astype(BF16)
    ya = jnp.dot(za, wl_ref[...], preferred_element_type=F32)
    yb = jnp.dot(at_ref[0], wa_ref[...], preferred_element_type=F32)
    mix = (ga_ref[0].astype(F32) * ya + gb_ref[0].astype(F32) * yb).astype(BF16)
    y = jnp.dot(mix, wo_ref[...], preferred_element_type=F32)
    o_ref[0] = x_ref[0] + mod_ref[0, :, 2 * D_MODEL:3 * D_MODEL] * y


def _mix_out_call(h_fwd, h_bwd, gg, attn, sga, sgb, x, mod, w_lru_bf, w_attn_bf, w_out_bf):
    bsz, n_tok, _ = x.shape
    tl = TOKEN_TILE
    tok = lambda width: pl.BlockSpec((1, tl, width), lambda b, l: (b, l, 0))
    return pl.pallas_call(
        _mix_out_kernel,
        grid=(bsz, n_tok // tl),
        in_specs=[tok(LRU_WIDTH), tok(LRU_WIDTH), tok(LRU_WIDTH), tok(ATTN_WIDTH), tok(D_MODEL),
                  tok(D_MODEL), tok(D_MODEL),
                  pl.BlockSpec((1, 1, mod.shape[-1]), lambda b, l: (b, 0, 0)),
                  _resident((LRU_WIDTH, D_MODEL)), _resident((ATTN_WIDTH, D_MODEL)),
                  _resident((D_MODEL, D_MODEL))],
        out_specs=tok(D_MODEL),
        out_shape=jax.ShapeDtypeStruct((bsz, n_tok, D_MODEL), F32),
        compiler_params=pltpu.CompilerParams(dimension_semantics=("arbitrary", "arbitrary"),
                                             vmem_limit_bytes=VMEM_LIMIT),
        name="mix_out",
    )(h_fwd, h_bwd, gg, attn, sga, sgb, x, mod, w_lru_bf, w_attn_bf, w_out_bf)


def _ffn_kernel(x_ref, mod_ref, nrm_ref, wi_ref, wo_ref, o_ref):
    x = x_ref[0]
    shift = mod_ref[0, :, 3 * D_MODEL:4 * D_MODEL]
    scale = mod_ref[0, :, 4 * D_MODEL:5 * D_MODEL]
    gate_out = mod_ref[0, :, 5 * D_MODEL:6 * D_MODEL]
    h = ((x * _rms_scale(x) * nrm_ref[...]) * (1.0 + scale) + shift).astype(BF16)
    acc = jnp.zeros(x.shape, F32)
    for lo, hi in zip(FFN_SPLITS[:-1], FFN_SPLITS[1:]):
        gate = jnp.dot(h, wi_ref[:, lo:hi], preferred_element_type=F32)
        up = jnp.dot(h, wi_ref[:, FFN_HIDDEN + lo:FFN_HIDDEN + hi], preferred_element_type=F32)
        act = (gate * _sigmoid(gate) * up).astype(BF16)
        acc = acc + jnp.dot(act, wo_ref[lo:hi, :], preferred_element_type=F32)
    o_ref[0] = x + gate_out * acc


def _ffn_call(x, mod, norm_g, w_in_bf, w_out_bf):
    bsz, n_tok, _ = x.shape
    tl = TOKEN_TILE
    tok = pl.BlockSpec((1, tl, D_MODEL), lambda b, l: (b, l, 0))
    return pl.pallas_call(
        _ffn_kernel,
        grid=(bsz, n_tok // tl),
        in_specs=[tok, pl.BlockSpec((1, 1, mod.shape[-1]), lambda b, l: (b, 0, 0)),
                  _resident((1, D_MODEL)),
                  _resident((D_MODEL, 2 * FFN_HIDDEN)), _resident((FFN_HIDDEN, D_MODEL))],
        out_specs=tok,
        out_shape=jax.ShapeDtypeStruct((bsz, n_tok, D_MODEL), F32),
        compiler_params=pltpu.CompilerParams(dimension_semantics=("arbitrary", "arbitrary"),
                                             vmem_limit_bytes=VMEM_LIMIT),
        name="ffn",
    )(x, mod, norm_g.reshape(1, D_MODEL), w_in_bf, w_out_bf)


def _rope_tables(n_tok):
    pos = np.arange(n_tok)
    inv_freq = ROPE_THETA ** (-np.arange(0, ROPE_AXIS_DIM, 2, dtype=np.float64) / ROPE_AXIS_DIM)
    ang_r = (pos // GRID_W)[:, None] * inv_freq[None, :]
    ang_c = (pos % GRID_W)[:, None] * inv_freq[None, :]
    cr, sr, cc, sc = (f(a) for a in (ang_r, ang_c) for f in (np.cos, np.sin))
    zero = np.zeros_like(sr)
    cos = np.concatenate([cr, cr, cc, cc], axis=-1)
    sin_hi = np.concatenate([-sr, zero, -sc, zero], axis=-1)
    sin_lo = np.concatenate([zero, sr, zero, sc], axis=-1)
    return tuple(jnp.asarray(t, F32) for t in (cos, sin_hi, sin_lo))


def _gate_weights(wa, wx):
    def dense(w):
        rows = [jnp.pad(w[n], ((0, 0), (n * LRU_BLOCK_DIM, LRU_WIDTH - (n + 1) * LRU_BLOCK_DIM)))
                for n in range(LRU_BLOCKS)]
        return jnp.concatenate(rows, axis=0)

    da, dx = dense(wa), dense(wx)
    tiles = []
    for j, k0 in enumerate(GATE_K0):
        cols = slice(j * GATE_COLS, (j + 1) * GATE_COLS)
        tiles.append(jnp.concatenate([da[k0:k0 + GATE_K, cols], dx[k0:k0 + GATE_K, cols]], axis=1))
    return jnp.stack(tiles).astype(BF16)


def kernel(x, c, ctx, c_ctx, w_mod, b_mod, norm_mix, w_in, conv_w, conv_b, lru_wa, lru_ba, lru_wx,
           lru_bx, lru_lambda, q_norm, k_norm, w_out_lru, w_out_attn, w_out, norm_ffn, w_ffn_in,
           w_ffn_out):
    bsz, n_tok, _ = x.shape
    assert w_mod.shape[0] == 1, "single trunk layer"
    tables = _rope_tables(n_tok)

    c_all = jnp.concatenate([c, c_ctx[None, :]], axis=0)
    c_all = jnp.pad(c_all, ((0, -c_all.shape[0] % SUBLANES), (0, 0)))
    mod_all = _mod_call(c_all, w_mod[0], b_mod[0])
    mod = mod_all[:bsz].reshape(bsz, 1, -1)
    mod_c = mod_all[bsz:bsz + 1].reshape(1, 1, -1)

    col_scale = np.where(np.arange(IN_COLS) >= OFF_GA, 0.5, 1.0).astype(np.float32)
    w_in_bf = (w_in[0] * col_scale).astype(BF16)
    u_c, k_c, vt_c = _in_proj_call(ctx, mod_c, norm_mix[0], w_in_bf, None, k_norm[0], None, latent=False)
    u_l, gg, q, k_l, vt_l, sga, sgb = _in_proj_call(x, mod, norm_mix[0], w_in_bf, q_norm[0], k_norm[0],
                                                    tables, latent=True)

    w_gate = jnp.stack([_gate_weights(lru_wa[0, d], lru_wx[0, d]) for d in range(2)])
    b_gate = 0.5 * jnp.stack([lru_ba[0], lru_bx[0]], axis=1)
    lru = functools.partial(_lru_call, conv_w=conv_w[0], conv_b=conv_b[0], w_gate=w_gate, b_gate=b_gate,
                            lam=lru_lambda[0])
    _, _, h_seed = lru(u_c, jnp.zeros((2, bsz, LRU_WIDTH), F32))
    h_fwd, h_bwd, _ = lru(u_l, h_seed)

    later_weights = (w_out_lru[0], w_out_attn[0], w_out[0], w_ffn_in[0], w_ffn_out[0])
    attn, w_lru_bf, w_attn_bf, w_out_bf, w_ffn_in_bf, w_ffn_out_bf = _attention(
        q, k_c, vt_c, k_l, vt_l, q_norm[0], k_norm[0], later_weights)

    x1 = _mix_out_call(h_fwd, h_bwd, gg, attn, sga, sgb, x, mod, w_lru_bf, w_attn_bf, w_out_bf)
    return _ffn_call(x1, mod, norm_ffn[0], w_ffn_in_bf, w_ffn_out_bf)
```

```python
import functools

import jax
import jax.numpy as jnp
import numpy as np
from jax import lax
from jax.experimental import pallas as pl
from jax.experimental.pallas import tpu as pltpu

D_MODEL = 1024
GRID_W = 64
EPS = 1e-6
LRU_WIDTH = 1280
LRU_BLOCKS = 8
LRU_BLOCK_DIM = LRU_WIDTH // LRU_BLOCKS
LRU_C = 8.0
CONV_WIDTH = 4
CONV_LEFT = 2
HEAD_DIM = 128
LANES = 128
SUBLANES = 8
N_HEADS = 8
N_KV_HEADS = 2
GROUP = N_HEADS // N_KV_HEADS
ATTN_WIDTH = N_HEADS * HEAD_DIM
KV_WIDTH = N_KV_HEADS * HEAD_DIM
ROPE_AXIS_DIM = HEAD_DIM // 2
ROPE_THETA = 10000.0
FFN_HIDDEN = 2816

OFF_U = 0
OFF_G = OFF_U + LRU_WIDTH
OFF_Q = OFF_G + LRU_WIDTH
OFF_K = OFF_Q + ATTN_WIDTH
OFF_V = OFF_K + KV_WIDTH
OFF_GA = OFF_V + KV_WIDTH
OFF_GB = OFF_GA + D_MODEL
IN_COLS = OFF_GB + D_MODEL

MOD_COLS_TILE = 1024
TOKEN_TILE = 512
SCAN_TILE = 64
PERM_STEPS = 32
HALO = 16
GATE_COLS = 256
GATE_K = 512
Q_TILE = 256
FIXED_SHIFT_Q_TILES = 2
CAST_ROWS = 64
KEY_CHUNK = 512
MXU_TILE = 256
FFN_SPLITS = (0, 6 * MXU_TILE, FFN_HIDDEN)
SCORE_BOUND_SLACK = 1.02
MAX_FIXED_SHIFT = 60.0
VMEM_LIMIT = 56 * 1024 * 1024

F32 = jnp.float32
BF16 = jnp.bfloat16


def _gate_k_offsets():
    offs = []
    for j in range(LRU_WIDTH // GATE_COLS):
        first_row = (j * GATE_COLS // LRU_BLOCK_DIM) * LRU_BLOCK_DIM
        last_row = ((j + 1) * GATE_COLS - 1) // LRU_BLOCK_DIM * LRU_BLOCK_DIM + LRU_BLOCK_DIM
        k0 = min(first_row // LANES * LANES, LRU_WIDTH - GATE_K)
        assert k0 <= first_row and last_row <= k0 + GATE_K
        offs.append(k0)
    return tuple(offs)


GATE_K0 = _gate_k_offsets()


def _sigmoid(x):
    return 0.5 * jnp.tanh(0.5 * x) + 0.5


def _gelu_tanh(x):
    c = np.float32(np.sqrt(2.0 / np.pi))
    inner = x * (c + np.float32(0.044715) * c * (x * x))
    return (0.5 * x) * (1.0 + jnp.tanh(inner))


def _rms_scale(x):
    return lax.rsqrt(jnp.mean(x * x, axis=-1, keepdims=True) + EPS)


def _resident(shape):
    nd = len(shape)
    return pl.BlockSpec(shape, lambda *_: (0,) * nd, pipeline_mode=pl.Buffered(1))


def _mod_kernel(c_ref, w_ref, b_ref, o_ref):
    c = c_ref[...]
    s = c * _sigmoid(c)
    o_ref[...] = jnp.dot(s, w_ref[...], preferred_element_type=F32,
                         precision=lax.Precision.HIGHEST) + b_ref[...]


def _mod_call(c_all, w_mod, b_mod):
    rows = c_all.shape[0]
    n = w_mod.shape[1]
    tn = MOD_COLS_TILE
    return pl.pallas_call(
        _mod_kernel,
        grid=(n // tn,),
        in_specs=[pl.BlockSpec((rows, D_MODEL), lambda j: (0, 0)),
                  pl.BlockSpec((D_MODEL, tn), lambda j: (0, j)),
                  pl.BlockSpec((1, tn), lambda j: (0, j))],
        out_specs=pl.BlockSpec((rows, tn), lambda j: (0, j)),
        out_shape=jax.ShapeDtypeStruct((rows, n), F32),
        compiler_params=pltpu.CompilerParams(dimension_semantics=("arbitrary",),
                                             vmem_limit_bytes=VMEM_LIMIT),
        name="mod",
    )(c_all, w_mod, b_mod.reshape(1, n))


def _gained_tables(gain, tables, scale):
    half = ROPE_AXIS_DIM // 2
    if tables is None:
        return (gain * scale,)
    cos, sin_hi, sin_lo = tables
    return (cos * (gain * scale), sin_hi * (pltpu.roll(gain, HEAD_DIM - half, 1) * scale),
            sin_lo * (pltpu.roll(gain, half, 1) * scale))


def _head_norm_rope(t, gained):
    half = ROPE_AXIS_DIM // 2
    n = t * _rms_scale(t)
    if len(gained) == 1:
        return n * gained[0]
    cos, sin_hi, sin_lo = gained
    return (n * cos + pltpu.roll(n, HEAD_DIM - half, 1) * sin_hi + pltpu.roll(n, half, 1) * sin_lo)


def _in_proj_kernel(*refs, latent, n_row_tiles, n_steps):
    if latent:
        (x0_ref, xn_ref, mod_ref, nrm_ref, w_ref, qn_ref, kn_ref, rc_ref, rh_ref, rl_ref,
         u_ref, g_ref, q_ref, k_ref, vt_ref, ga_ref, gb_ref, h_s) = refs
        tables = (rc_ref[...], rh_ref[...], rl_ref[...])
    else:
        x0_ref, xn_ref, mod_ref, nrm_ref, w_ref, kn_ref, u_ref, k_ref, vt_ref, h_s = refs
        tables = None
    step = pl.program_id(0) * n_row_tiles + pl.program_id(1)

    def normed(x, b):
        shift = mod_ref[b, :, 0:D_MODEL]
        scale = mod_ref[b, :, D_MODEL:2 * D_MODEL]
        return ((x * _rms_scale(x) * nrm_ref[...]) * (1.0 + scale) + shift).astype(BF16)

    @pl.when(step == 0)
    def _():
        h_s[...] = normed(x0_ref[0], 0)

    h = h_s[...]

    def proj(off, width):
        return jnp.dot(h, w_ref[:, off:off + width], preferred_element_type=F32)

    if latent:
        qq = proj(OFF_Q, ATTN_WIDTH)
        q_tables = _gained_tables(qn_ref[...], tables, np.float32(np.log2(np.e) / np.sqrt(HEAD_DIM)))
        for j in range(N_HEADS):
            sl = slice(j * HEAD_DIM, (j + 1) * HEAD_DIM)
            q_ref[0, :, sl] = _head_norm_rope(qq[:, sl], q_tables).astype(BF16)
    kk = proj(OFF_K, KV_WIDTH)
    k_tables = _gained_tables(kn_ref[...], tables, np.float32(1.0))
    for j in range(N_KV_HEADS):
        sl = slice(j * HEAD_DIM, (j + 1) * HEAD_DIM)
        k_ref[0, :, sl] = _head_norm_rope(kk[:, sl], k_tables).astype(BF16)
    if latent:
        g_ref[0] = _gelu_tanh(proj(OFF_G, LRU_WIDTH)).astype(BF16)
        ga_ref[0] = (0.5 * jnp.tanh(proj(OFF_GA, D_MODEL)) + 0.5).astype(BF16)
        gb_ref[0] = (0.5 * jnp.tanh(proj(OFF_GB, D_MODEL)) + 0.5).astype(BF16)
    vt_ref[0] = proj(OFF_V, KV_WIDTH).T.astype(BF16)
    u_ref[0] = proj(OFF_U, LRU_WIDTH).astype(BF16)

    nxt = jnp.minimum(step + 1, n_steps - 1)
    b_next = nxt // n_row_tiles if mod_ref.shape[0] > 1 else 0
    h_s[...] = normed(xn_ref[0], b_next)


def _next_tile_map(n_row_tiles, n_steps):
    def index_map(b, l):
        nxt = jnp.minimum(b * n_row_tiles + l + 1, n_steps - 1)
        return (nxt // n_row_tiles, nxt % n_row_tiles, 0)
    return index_map


def _in_proj_call(x, mod, norm_g, w_in_bf, q_gain, k_gain, tables, latent):
    bsz, n_tok, _ = x.shape
    tl = min(TOKEN_TILE, n_tok)
    n_row_tiles = n_tok // tl
    grid = (bsz, n_row_tiles)
    n_steps = bsz * n_row_tiles

    tok = lambda width: pl.BlockSpec((1, tl, width), lambda b, l: (b, l, 0))
    in_specs = [pl.BlockSpec((1, tl, D_MODEL), lambda b, l: (0, 0, 0)),
                pl.BlockSpec((1, tl, D_MODEL), _next_tile_map(n_row_tiles, n_steps)),
                _resident(mod.shape),
                _resident((1, D_MODEL)),
                _resident((D_MODEL, IN_COLS))]
    args = [x, x, mod, norm_g.reshape(1, D_MODEL), w_in_bf]
    vt_spec = pl.BlockSpec((1, KV_WIDTH, tl), lambda b, l: (b, 0, l))
    vt_shape = jax.ShapeDtypeStruct((bsz, KV_WIDTH, n_tok), BF16)
    bf = lambda width: jax.ShapeDtypeStruct((bsz, n_tok, width), BF16)
    if latent:
        in_specs += [_resident((1, HEAD_DIM)), _resident((1, HEAD_DIM))]
        in_specs += [pl.BlockSpec((tl, HEAD_DIM), lambda b, l: (l, 0))] * 3
        args += [q_gain.reshape(1, HEAD_DIM), k_gain.reshape(1, HEAD_DIM), *tables]
        out_specs = [tok(LRU_WIDTH), tok(LRU_WIDTH), tok(ATTN_WIDTH), tok(KV_WIDTH), vt_spec,
                     tok(D_MODEL), tok(D_MODEL)]
        out_shape = [bf(LRU_WIDTH), bf(LRU_WIDTH), bf(ATTN_WIDTH), bf(KV_WIDTH), vt_shape,
                     bf(D_MODEL), bf(D_MODEL)]
    else:
        in_specs += [_resident((1, HEAD_DIM))]
        args += [k_gain.reshape(1, HEAD_DIM)]
        out_specs = [tok(LRU_WIDTH), tok(KV_WIDTH), vt_spec]
        out_shape = [bf(LRU_WIDTH), bf(KV_WIDTH), vt_shape]
    return pl.pallas_call(
        functools.partial(_in_proj_kernel, latent=latent, n_row_tiles=n_row_tiles, n_steps=n_steps),
        grid=grid, in_specs=in_specs, out_specs=out_specs, out_shape=out_shape,
        scratch_shapes=[pltpu.VMEM((tl, D_MODEL), BF16)],
        compiler_params=pltpu.CompilerParams(dimension_semantics=("arbitrary", "arbitrary"),
                                             vmem_limit_bytes=VMEM_LIMIT),
        name="in_proj_latent" if latent else "in_proj_ctx",
    )(*args)


def _lru_kernel(uf_ref, pf_ref, nf_ref, ub_ref, pb_ref, nb_ref, h0_ref, cw_ref, cb_ref, wg_ref, bg_ref,
                lam_ref, perm_ref, permt_ref, halo_ref, *rest, n_tiles, emit_h):
    hf_ref, hb_ref = rest[:2] if emit_h else (None, None)
    hlast_ref, ext_s, a_s, b_s, carry_s = rest[2 if emit_h else 0:]
    t = pl.program_id(0)
    bsz, ts, _ = uf_ref.shape
    dirs = ((uf_ref, pf_ref, nf_ref, hf_ref, t), (ub_ref, pb_ref, nb_ref, hb_ref, n_tiles - 1 - t))

    @pl.when(t == 0)
    def _():
        carry_s[...] = h0_ref[...]

    def load_time_major(d):
        u_ref, prev_ref, next_ref, _, tile = dirs[d]
        for p in range(ts // PERM_STEPS):
            steps = slice(p * PERM_STEPS, (p + 1) * PERM_STEPS)
            u_bm = u_ref[:, steps, :].reshape(bsz * PERM_STEPS, LRU_WIDTH)
            u_tm = jnp.dot(perm_ref[...], u_bm, preferred_element_type=F32)
            ext_s[d, CONV_LEFT + p * PERM_STEPS:CONV_LEFT + (p + 1) * PERM_STEPS] = (
                u_tm.reshape(PERM_STEPS, bsz, LRU_WIDTH))
        edge = jnp.concatenate([prev_ref[:, HALO // 2:, :].reshape(bsz * HALO // 2, LRU_WIDTH),
                                next_ref[:, :HALO // 2, :].reshape(bsz * HALO // 2, LRU_WIDTH)], axis=0)
        halo = jnp.dot(halo_ref[...], edge, preferred_element_type=F32)
        ext_s[d, 0:CONV_LEFT] = (halo[0:CONV_LEFT * bsz]
                                 * jnp.where(tile > 0, 1.0, 0.0)).reshape(CONV_LEFT, bsz, LRU_WIDTH)
        ext_s[d, CONV_LEFT + ts:] = (halo[CONV_LEFT * bsz:(CONV_LEFT + 1) * bsz]
                                     * jnp.where(tile < n_tiles - 1, 1.0, 0.0)).reshape(1, bsz, LRU_WIDTH)

    def coefficients(d):
        xc = cb_ref[...][None]
        for j in range(CONV_WIDTH):
            xc = xc + ext_s[d, j:j + ts] * cw_ref[j:j + 1, :][None]
        xh = xc.reshape(ts * bsz, LRU_WIDTH)
        xb = xh.astype(BF16)
        lam = lam_ref[d]
        sp = jnp.maximum(-lam, 0.0) + jnp.log1p(jnp.exp(-jnp.abs(lam)))
        decay = sp * np.float32(-0.5 * LRU_C * np.log2(np.e))
        for j, k0 in enumerate(GATE_K0):
            cols = slice(j * GATE_COLS, (j + 1) * GATE_COLS)
            gates = jnp.dot(xb[:, k0:k0 + GATE_K], wg_ref[d, j], preferred_element_type=F32)
            t_r = jnp.tanh(gates[:, :GATE_COLS] + bg_ref[d, 0:1, cols])
            t_i = jnp.tanh(gates[:, GATE_COLS:] + bg_ref[d, 1:2, cols])
            a = jnp.exp2(decay[:, cols] * t_r + decay[:, cols])
            om = 1.0 - a * a
            coef = om * lax.rsqrt(jnp.maximum(om, 1e-30))
            a_s[d, :, :, cols] = a.reshape(ts, bsz, GATE_COLS)
            b_s[d, :, :, cols] = (coef * ((t_i + 1.0) * xh[:, cols])).reshape(ts, bsz, GATE_COLS)

    def scan(d):
        h = carry_s[d]
        for s in range(ts):
            idx = (ts - 1 - s) if d == 1 else s
            h = a_s[d, idx] * h + b_s[d, idx]
            b_s[d, idx] = h
        carry_s[d] = h

    def store_batch_major(d):
        h_ref = dirs[d][3]
        if h_ref is None:
            return
        for p in range(ts // PERM_STEPS):
            steps = slice(p * PERM_STEPS, (p + 1) * PERM_STEPS)
            h_tm = b_s[d, steps].reshape(PERM_STEPS * bsz, LRU_WIDTH).astype(BF16)
            h_bm = jnp.dot(permt_ref[...], h_tm, preferred_element_type=F32)
            h_ref[:, steps, :] = h_bm.reshape(bsz, PERM_STEPS, LRU_WIDTH).astype(BF16)

    load_time_major(0)
    load_time_major(1)
    coefficients(0)
    scan(0)
    coefficients(1)
    store_batch_major(0)
    scan(1)
    store_batch_major(1)

    @pl.when(t == n_tiles - 1)
    def _():
        hlast_ref[...] = carry_s[...]


def _perm_matrices(bsz):
    n = bsz * PERM_STEPS
    r_out = np.arange(n)
    s, b = r_out // bsz, r_out % bsz
    perm = np.zeros((n, n), np.float32)
    perm[r_out, b * PERM_STEPS + s] = 1.0
    half = HALO // 2
    halo = np.zeros((CONV_WIDTH * bsz, 2 * bsz * half), np.float32)
    for b_ in range(bsz):
        for k in range(CONV_LEFT):
            halo[k * bsz + b_, b_ * half + half - CONV_LEFT + k] = 1.0
        halo[CONV_LEFT * bsz + b_, bsz * half + b_ * half] = 1.0
    return jnp.asarray(perm, BF16), jnp.asarray(perm.T, BF16), jnp.asarray(halo, BF16)


def _lru_call(u, h0, conv_w, conv_b, w_gate, b_gate, lam, emit_h):
    bsz, n_tok, _ = u.shape
    ts = SCAN_TILE
    n_tiles = n_tok // ts
    per = ts // HALO
    perm, perm_t, halo = _perm_matrices(bsz)

    def tile_specs(pos):
        return [pl.BlockSpec((bsz, ts, LRU_WIDTH), lambda t: (0, pos(t), 0)),
                pl.BlockSpec((bsz, HALO, LRU_WIDTH), lambda t: (0, jnp.maximum(pos(t) * per - 1, 0), 0)),
                pl.BlockSpec((bsz, HALO, LRU_WIDTH),
                             lambda t: (0, jnp.minimum((pos(t) + 1) * per, n_tok // HALO - 1), 0))]

    fwd = lambda t: t
    bwd = lambda t: n_tiles - 1 - t
    in_specs = tile_specs(fwd) + tile_specs(bwd) + [
        _resident((2, bsz, LRU_WIDTH)),
        _resident((CONV_WIDTH, LRU_WIDTH)),
        _resident((1, LRU_WIDTH)),
        _resident((2, len(GATE_K0), GATE_K, 2 * GATE_COLS)),
        _resident((2, 2, LRU_WIDTH)),
        _resident((2, 1, LRU_WIDTH)),
        _resident(perm.shape), _resident(perm_t.shape), _resident(halo.shape),
    ]
    out_specs = [pl.BlockSpec((bsz, ts, LRU_WIDTH), lambda t: (0, fwd(t), 0)),
                 pl.BlockSpec((bsz, ts, LRU_WIDTH), lambda t: (0, bwd(t), 0)),
                 pl.BlockSpec((2, bsz, LRU_WIDTH), lambda t: (0, 0, 0))]
    out_shape = [jax.ShapeDtypeStruct((bsz, n_tok, LRU_WIDTH), BF16),
                 jax.ShapeDtypeStruct((bsz, n_tok, LRU_WIDTH), BF16),
                 jax.ShapeDtypeStruct((2, bsz, LRU_WIDTH), F32)]
    if not emit_h:
        out_specs, out_shape = out_specs[2:], out_shape[2:]
    return pl.pallas_call(
        functools.partial(_lru_kernel, n_tiles=n_tiles, emit_h=emit_h),
        grid=(n_tiles,), in_specs=in_specs, out_specs=out_specs, out_shape=out_shape,
        scratch_shapes=[pltpu.VMEM((2, ts + CONV_WIDTH - 1, bsz, LRU_WIDTH), F32),
                        pltpu.VMEM((2, ts, bsz, LRU_WIDTH), F32),
                        pltpu.VMEM((2, ts, bsz, LRU_WIDTH), F32),
                        pltpu.VMEM((2, bsz, LRU_WIDTH), F32)],
        compiler_params=pltpu.CompilerParams(dimension_semantics=("arbitrary",),
                                             vmem_limit_bytes=VMEM_LIMIT),
        name="lru",
    )(u, u, u, u, u, u, h0, 0.5 * conv_w, 0.5 * conv_b.reshape(1, LRU_WIDTH), w_gate, b_gate,
      lam.reshape(2, 1, LRU_WIDTH), perm, perm_t, halo)


def _cast_specs(weights, n_steps, step_of):
    in_specs, out_specs, out_shape = [], [], []
    for w in weights:
        rows, cols = w.shape
        n_blocks = rows // CAST_ROWS
        assert rows % CAST_ROWS == 0 and n_blocks <= n_steps
        index_map = lambda *idx, n_blocks=n_blocks: (jnp.minimum(step_of(*idx), n_blocks - 1), 0)
        in_specs.append(pl.BlockSpec((CAST_ROWS, cols), index_map))
        out_specs.append(pl.BlockSpec((CAST_ROWS, cols), index_map))
        out_shape.append(jax.ShapeDtypeStruct((rows, cols), BF16))
    return in_specs, out_specs, out_shape


def _cast_blocks(w_refs, o_refs):
    for w_ref, o_ref in zip(w_refs, o_refs):
        o_ref[...] = w_ref[...].astype(BF16)


def _attn_kernel(q_ref, kc_ref, vc_ref, kl_ref, vl_ref, *rest):
    n_cast = (len(rest) - 6) // 2
    w_refs, o_ref, wo_refs = rest[:n_cast], rest[n_cast], rest[n_cast + 1:2 * n_cast + 1]
    sa_s, sb_s, acc_s, m_s, l_s = rest[2 * n_cast + 1:]
    _cast_blocks(w_refs, wo_refs)
    n_chunks = kl_ref.shape[1] // KEY_CHUNK

    def scores(k, g):
        qg = q_ref[0, :, g * HEAD_DIM:(g + 1) * HEAD_DIM]
        return lax.dot_general(k, qg, (((1,), (1,)), ((), ())), preferred_element_type=F32)

    def softmax_pv(st, vt, g, first):
        m_c = jnp.max(st, axis=0, keepdims=True)
        if first:
            m_new = m_c
        else:
            m_old = m_s[g]
            m_new = jnp.maximum(m_old, m_c)
            alpha = jnp.exp2(m_old - m_new)
        p = jnp.exp2(st - m_new)
        p_sum = jnp.sum(p, axis=0, keepdims=True)
        pv = jnp.dot(vt, p.astype(BF16), preferred_element_type=F32)
        if first:
            l_s[g] = p_sum
            acc_s[g] = pv
        else:
            l_s[g] = alpha * l_s[g] + p_sum
            acc_s[g] = alpha * acc_s[g] + pv
        m_s[g] = m_new

    def lat_keys(c):
        return kl_ref[0, pl.ds(pl.multiple_of(c * KEY_CHUNK, KEY_CHUNK), KEY_CHUNK), :]

    def lat_vals(c):
        return vl_ref[0, :, pl.ds(pl.multiple_of(c * KEY_CHUNK, KEY_CHUNK), KEY_CHUNK)]

    def stage(k_next, st_in, st_out, vt, first=False):
        for g in range(GROUP):
            if k_next is not None:
                st_out[g] = scores(k_next, g)
            softmax_pv(st_in[g], vt, g, first)

    k0 = kl_ref[0, 0:KEY_CHUNK, :]
    for g in range(GROUP):
        sc = scores(kc_ref[0], g)
        sa_s[g] = scores(k0, g)
        softmax_pv(sc, vc_ref[0], g, True)

    def body(i, carry):
        stage(lat_keys(2 * i + 1), sa_s, sb_s, lat_vals(2 * i))
        stage(lat_keys(2 * i + 2), sb_s, sa_s, lat_vals(2 * i + 1))
        return carry

    lax.fori_loop(0, n_chunks // 2 - 1, body, 0)
    stage(lat_keys(n_chunks - 1), sa_s, sb_s, lat_vals(n_chunks - 2))
    stage(None, sb_s, None, lat_vals(n_chunks - 1))
    for g in range(GROUP):
        o = (acc_s[g] / l_s[g]).T
        o_ref[0, :, g * HEAD_DIM:(g + 1) * HEAD_DIM] = o.astype(BF16)


def _attn_call(q, k_ctx, vt_ctx, k_lat, vt_lat, *weights):
    bsz, n_tok, _ = q.shape
    n_ctx = k_ctx.shape[1]
    tq = Q_TILE
    gw = GROUP * HEAD_DIM
    n_qt = n_tok // tq
    keys = lambda n: pl.BlockSpec((1, n, HEAD_DIM), lambda b, j, i: (b, 0, j))
    vals = lambda n: pl.BlockSpec((1, HEAD_DIM, n), lambda b, j, i: (b, j, 0))
    w_in, w_out, w_shape = _cast_specs(weights, bsz * N_KV_HEADS * n_qt,
                                       lambda b, j, i: (b * N_KV_HEADS + j) * n_qt + i)
    return pl.pallas_call(
        _attn_kernel,
        grid=(bsz, N_KV_HEADS, n_qt),
        in_specs=[pl.BlockSpec((1, tq, gw), lambda b, j, i: (b, i, j)),
                  keys(n_ctx), pl.BlockSpec((1, HEAD_DIM, n_ctx), lambda b, j, i: (0, j, b)),
                  keys(n_tok), vals(n_tok)] + w_in,
        out_specs=[pl.BlockSpec((1, tq, gw), lambda b, j, i: (b, i, j))] + w_out,
        out_shape=[jax.ShapeDtypeStruct((bsz, n_tok, ATTN_WIDTH), BF16)] + w_shape,
        scratch_shapes=[pltpu.VMEM((GROUP, KEY_CHUNK, tq), F32),
                        pltpu.VMEM((GROUP, KEY_CHUNK, tq), F32),
                        pltpu.VMEM((GROUP, HEAD_DIM, tq), F32),
                        pltpu.VMEM((GROUP, 1, tq), F32),
                        pltpu.VMEM((GROUP, 1, tq), F32)],
        compiler_params=pltpu.CompilerParams(
            dimension_semantics=("arbitrary", "arbitrary", "arbitrary"),
            vmem_limit_bytes=VMEM_LIMIT),
        name="attn",
    )(q, k_ctx, vt_ctx, k_lat, vt_lat, *weights)


def _attn_fixed_shift_kernel(shift_ref, q_ref, kc_ref, vc_ref, kl_ref, vl_ref, *rest):
    n_cast = (len(rest) - 1) // 2
    w_refs, o_ref, wo_refs = rest[:n_cast], rest[n_cast], rest[n_cast + 1:]
    _cast_blocks(w_refs, wo_refs)
    tq = Q_TILE
    shift = shift_ref[0]
    for t in range(q_ref.shape[1] // tq):
        rows = slice(t * tq, (t + 1) * tq)
        for j in range(N_KV_HEADS):
            base = j * GROUP * HEAD_DIM
            kv = slice(j * HEAD_DIM, (j + 1) * HEAD_DIM)
            q_all = jnp.concatenate([q_ref[0, rows, base + g * HEAD_DIM:base + (g + 1) * HEAD_DIM]
                                     for g in range(GROUP)], axis=0)

            def unnormalised(k, vt):
                st = lax.dot_general(k, q_all, (((1,), (1,)), ((), ())), preferred_element_type=F32)
                p = jnp.exp2(st - shift)
                pv = jnp.dot(vt, p.astype(BF16), preferred_element_type=F32)
                return jnp.sum(p, axis=0, keepdims=True), pv

            l_ctx, acc_ctx = unnormalised(kc_ref[0, :, kv], vc_ref[0, kv, :])
            l_lat, acc_lat = unnormalised(kl_ref[0, :, kv], vl_ref[0, kv, :])
            o_t = (acc_ctx + acc_lat) / (l_ctx + l_lat)
            for g in range(GROUP):
                o_ref[0, rows, base + g * HEAD_DIM:base + (g + 1) * HEAD_DIM] = (
                    o_t[:, g * tq:(g + 1) * tq].T.astype(BF16))


def _attn_fixed_shift_call(shift, q, k_ctx, vt_ctx, k_lat, vt_lat, *weights):
    bsz, n_tok, _ = q.shape
    n_ctx = k_ctx.shape[1]
    rows = FIXED_SHIFT_Q_TILES * Q_TILE
    n_qt = n_tok // rows
    keys = lambda n: pl.BlockSpec((1, n, KV_WIDTH), lambda b, i: (b, 0, 0))
    vals = lambda n: pl.BlockSpec((1, KV_WIDTH, n), lambda b, i: (b, 0, 0))
    w_in, w_out, w_shape = _cast_specs(weights, bsz * n_qt, lambda b, i: b * n_qt + i)
    return pl.pallas_call(
        _attn_fixed_shift_kernel,
        grid=(bsz, n_qt),
        in_specs=[pl.BlockSpec(memory_space=pltpu.SMEM),
                  pl.BlockSpec((1, rows, ATTN_WIDTH), lambda b, i: (b, i, 0)),
                  keys(n_ctx), pl.BlockSpec((1, KV_WIDTH, n_ctx), lambda b, i: (0, 0, b)),
                  keys(n_tok), vals(n_tok)] + w_in,
        out_specs=[pl.BlockSpec((1, rows, ATTN_WIDTH), lambda b, i: (b, i, 0))] + w_out,
        out_shape=[jax.ShapeDtypeStruct((bsz, n_tok, ATTN_WIDTH), BF16)] + w_shape,
        compiler_params=pltpu.CompilerParams(
            dimension_semantics=("arbitrary", "arbitrary"),
            vmem_limit_bytes=VMEM_LIMIT),
        name="attn_fixed_shift",
    )(shift, q, k_ctx, vt_ctx, k_lat, vt_lat, *weights)


def _attention(q, k_ctx, vt_ctx, k_lat, vt_lat, q_gain, k_gain, weights):
    bound = (SCORE_BOUND_SLACK * np.float32(np.sqrt(HEAD_DIM) * np.log2(np.e))
             * jnp.max(jnp.abs(q_gain)) * jnp.max(jnp.abs(k_gain)))
    operands = (q, k_ctx, vt_ctx, k_lat, vt_lat, *weights)
    return lax.cond(bound <= MAX_FIXED_SHIFT,
                    lambda ops: _attn_fixed_shift_call(bound.reshape(1), *ops),
                    lambda ops: _attn_call(*ops), operands)


def _mix_out_kernel(hf_ref, hb_ref, g_ref, at_ref, ga_ref, gb_ref, x_ref, mod_ref,
                    wl_ref, wa_ref, wo_ref, o_ref):
    lru = hf_ref[0].astype(F32) + hb_ref[0].astype(F32)
    za = (lru * g_ref[0].astype(F32)).astype(BF16)
    ya = jnp.dot(za, wl_ref[...], preferred_element_type=F32)
    yb = jnp.dot(at_ref[0], wa_ref[...], preferred_element_type=F32)
    mix = (ga_ref[0].astype(F32) * ya + gb_ref[0].astype(F32) * yb).astype(BF16)
    y = jnp.dot(mix, wo_ref[...], preferred_element_type=F32)
    o_ref[0] = x_ref[0] + mod_ref[0, :, 2 * D_MODEL:3 * D_MODEL] * y


def _mix_out_call(h_fwd, h_bwd, gg, attn, sga, sgb, x, mod, w_lru_bf, w_attn_bf, w_out_bf):
    bsz, n_tok, _ = x.shape
    tl = TOKEN_TILE
    tok = lambda width: pl.BlockSpec((1, tl, width), lambda b, l: (b, l, 0))
    return pl.pallas_call(
        _mix_out_kernel,
        grid=(bsz, n_tok // tl),
        in_specs=[tok(LRU_WIDTH), tok(LRU_WIDTH), tok(LRU_WIDTH), tok(ATTN_WIDTH), tok(D_MODEL),
                  tok(D_MODEL), tok(D_MODEL),
                  pl.BlockSpec((1, 1, mod.shape[-1]), lambda b, l: (b, 0, 0)),
                  _resident((LRU_WIDTH, D_MODEL)), _resident((ATTN_WIDTH, D_MODEL)),
                  _resident((D_MODEL, D_MODEL))],
        out_specs=tok(D_MODEL),
        out_shape=jax.ShapeDtypeStruct((bsz, n_tok, D_MODEL), F32),
        compiler_params=pltpu.CompilerParams(dimension_semantics=("arbitrary", "arbitrary"),
                                             vmem_limit_bytes=VMEM_LIMIT),
        name="mix_out",
    )(h_fwd, h_bwd, gg, attn, sga, sgb, x, mod, w_lru_bf, w_attn_bf, w_out_bf)


def _ffn_kernel(x_ref, mod_ref, nrm_ref, wi_ref, wo_ref, o_ref):
    x = x_ref[0]
    shift = mod_ref[0, :, 3 * D_MODEL:4 * D_MODEL]
    scale = mod_ref[0, :, 4 * D_MODEL:5 * D_MODEL]
    gate_out = mod_ref[0, :, 5 * D_MODEL:6 * D_MODEL]
    h = ((x * _rms_scale(x) * nrm_ref[...]) * (1.0 + scale) + shift).astype(BF16)
    acc = jnp.zeros(x.shape, F32)
    for lo, hi in zip(FFN_SPLITS[:-1], FFN_SPLITS[1:]):
        gate = jnp.dot(h, wi_ref[:, lo:hi], preferred_element_type=F32)
        up = jnp.dot(h, wi_ref[:, FFN_HIDDEN + lo:FFN_HIDDEN + hi], preferred_element_type=F32)
        act = (gate * _sigmoid(gate) * up).astype(BF16)
        acc = acc + jnp.dot(act, wo_ref[lo:hi, :], preferred_element_type=F32)
    o_ref[0] = x + gate_out * acc


def _ffn_call(x, mod, norm_g, w_in_bf, w_out_bf):
    bsz, n_tok, _ = x.shape
    tl = TOKEN_TILE
    tok = pl.BlockSpec((1, tl, D_MODEL), lambda b, l: (b, l, 0))
    return pl.pallas_call(
        _ffn_kernel,
        grid=(bsz, n_tok // tl),
        in_specs=[tok, pl.BlockSpec((1, 1, mod.shape[-1]), lambda b, l: (b, 0, 0)),
                  _resident((1, D_MODEL)),
                  _resident((D_MODEL, 2 * FFN_HIDDEN)), _resident((FFN_HIDDEN, D_MODEL))],
        out_specs=tok,
        out_shape=jax.ShapeDtypeStruct((bsz, n_tok, D_MODEL), F32),
        compiler_params=pltpu.CompilerParams(dimension_semantics=("arbitrary", "arbitrary"),
                                             vmem_limit_bytes=VMEM_LIMIT),
        name="ffn",
    )(x, mod, norm_g.reshape(1, D_MODEL), w_in_bf, w_out_bf)


def _rope_tables(n_tok):
    pos = np.arange(n_tok)
    inv_freq = ROPE_THETA ** (-np.arange(0, ROPE_AXIS_DIM, 2, dtype=np.float64) / ROPE_AXIS_DIM)
    ang_r = (pos // GRID_W)[:, None] * inv_freq[None, :]
    ang_c = (pos % GRID_W)[:, None] * inv_freq[None, :]
    cr, sr, cc, sc = (f(a) for a in (ang_r, ang_c) for f in (np.cos, np.sin))
    zero = np.zeros_like(sr)
    cos = np.concatenate([cr, cr, cc, cc], axis=-1)
    sin_hi = np.concatenate([-sr, zero, -sc, zero], axis=-1)
    sin_lo = np.concatenate([zero, sr, zero, sc], axis=-1)
    return tuple(jnp.asarray(t, F32) for t in (cos, sin_hi, sin_lo))


def _gate_weights(wa, wx):
    def dense(w):
        rows = [jnp.pad(w[n], ((0, 0), (n * LRU_BLOCK_DIM, LRU_WIDTH - (n + 1) * LRU_BLOCK_DIM)))
                for n in range(LRU_BLOCKS)]
        return jnp.concatenate(rows, axis=0)

    da, dx = dense(wa), dense(wx)
    tiles = []
    for j, k0 in enumerate(GATE_K0):
        cols = slice(j * GATE_COLS, (j + 1) * GATE_COLS)
        tiles.append(jnp.concatenate([da[k0:k0 + GATE_K, cols], dx[k0:k0 + GATE_K, cols]], axis=1))
    return jnp.stack(tiles).astype(BF16)


def kernel(x, c, ctx, c_ctx, w_mod, b_mod, norm_mix, w_in, conv_w, conv_b, lru_wa, lru_ba, lru_wx,
           lru_bx, lru_lambda, q_norm, k_norm, w_out_lru, w_out_attn, w_out, norm_ffn, w_ffn_in,
           w_ffn_out):
    bsz, n_tok, _ = x.shape
    assert w_mod.shape[0] == 1, "single trunk layer"
    tables = _rope_tables(n_tok)

    c_all = jnp.concatenate([c, c_ctx[None, :]], axis=0)
    c_all = jnp.pad(c_all, ((0, -c_all.shape[0] % SUBLANES), (0, 0)))
    mod_all = _mod_call(c_all, w_mod[0], b_mod[0])
    mod = mod_all[:bsz].reshape(bsz, 1, -1)
    mod_c = mod_all[bsz:bsz + 1].reshape(1, 1, -1)

    col_scale = np.where(np.arange(IN_COLS) >= OFF_GA, 0.5, 1.0).astype(np.float32)
    w_in_bf = (w_in[0] * col_scale).astype(BF16)
    n_ctx = ctx.shape[1]
    u_c, k_c, vt_c = _in_proj_call(ctx.reshape(1, bsz * n_ctx, D_MODEL), mod_c, norm_mix[0], w_in_bf,
                                   None, k_norm[0], None, latent=False)
    u_c = u_c.reshape(bsz, n_ctx, LRU_WIDTH)
    k_c = k_c.reshape(bsz, n_ctx, KV_WIDTH)
    u_l, gg, q, k_l, vt_l, sga, sgb = _in_proj_call(x, mod, norm_mix[0], w_in_bf, q_norm[0], k_norm[0],
                                                    tables, latent=True)

    w_gate = jnp.stack([_gate_weights(lru_wa[0, d], lru_wx[0, d]) for d in range(2)])
    b_gate = 0.5 * jnp.stack([lru_ba[0], lru_bx[0]], axis=1)
    lru = functools.partial(_lru_call, conv_w=conv_w[0], conv_b=conv_b[0], w_gate=w_gate, b_gate=b_gate,
                            lam=lru_lambda[0])
    (h_seed,) = lru(u_c, jnp.zeros((2, bsz, LRU_WIDTH), F32), emit_h=False)
    h_fwd, h_bwd, _ = lru(u_l, h_seed, emit_h=True)

    later_weights = (w_out_lru[0], w_out_attn[0], w_out[0], w_ffn_in[0], w_ffn_out[0])
    attn, w_lru_bf, w_attn_bf, w_out_bf, w_ffn_in_bf, w_ffn_out_bf = _attention(
        q, k_c, vt_c, k_l, vt_l, q_norm[0], k_norm[0], later_weights)

    x1 = _mix_out_call(h_fwd, h_bwd, gg, attn, sga, sgb, x, mod, w_lru_bf, w_attn_bf, w_out_bf)
    return _ffn_call(x1, mod, norm_ffn[0], w_ffn_in_bf, w_ffn_out_bf)
```

```python
import functools

import jax
import jax.numpy as jnp
import numpy as np
from jax import lax
from jax.experimental import pallas as pl
from jax.experimental.pallas import tpu as pltpu

D_MODEL = 1024
GRID_W = 64
EPS = 1e-6
LRU_WIDTH = 1280
LRU_BLOCKS = 8
LRU_BLOCK_DIM = LRU_WIDTH // LRU_BLOCKS
LRU_C = 8.0
CONV_WIDTH = 4
CONV_LEFT = 2
HEAD_DIM = 128
LANES = 128
SUBLANES = 8
N_HEADS = 8
N_KV_HEADS = 2
GROUP = N_HEADS // N_KV_HEADS
ATTN_WIDTH = N_HEADS * HEAD_DIM
KV_WIDTH = N_KV_HEADS * HEAD_DIM
ROPE_AXIS_DIM = HEAD_DIM // 2
ROPE_THETA = 10000.0
FFN_HIDDEN = 2816

OFF_U = 0
OFF_G = OFF_U + LRU_WIDTH
OFF_Q = OFF_G + LRU_WIDTH
OFF_K = OFF_Q + ATTN_WIDTH
OFF_V = OFF_K + KV_WIDTH
OFF_GA = OFF_V + KV_WIDTH
OFF_GB = OFF_GA + D_MODEL
IN_COLS = OFF_GB + D_MODEL

MOD_COLS_TILE = 1024
TOKEN_TILE = 512
SCAN_TILE = 64
PERM_STEPS = 32
HALO = 16
GATE_COLS = 256
GATE_K = 512
Q_TILE = 256
FIXED_SHIFT_Q_TILES = 2
CAST_ROWS = 64
KEY_CHUNK = 512
MXU_TILE = 256
FFN_TOKEN_TILE = 1024
FFN_SPLITS = (0, 4 * MXU_TILE, 8 * MXU_TILE, FFN_HIDDEN)
SCORE_BOUND_SLACK = 1.02
MAX_FIXED_SHIFT = 60.0
VMEM_LIMIT = 56 * 1024 * 1024

F32 = jnp.float32
BF16 = jnp.bfloat16


def _gate_k_offsets():
    offs = []
    for j in range(LRU_WIDTH // GATE_COLS):
        first_row = (j * GATE_COLS // LRU_BLOCK_DIM) * LRU_BLOCK_DIM
        last_row = ((j + 1) * GATE_COLS - 1) // LRU_BLOCK_DIM * LRU_BLOCK_DIM + LRU_BLOCK_DIM
        k0 = min(first_row // LANES * LANES, LRU_WIDTH - GATE_K)
        assert k0 <= first_row and last_row <= k0 + GATE_K
        offs.append(k0)
    return tuple(offs)


GATE_K0 = _gate_k_offsets()


def _sigmoid(x):
    return 0.5 * jnp.tanh(0.5 * x) + 0.5


def _gelu_tanh(x):
    c = np.float32(np.sqrt(2.0 / np.pi))
    inner = x * (c + np.float32(0.044715) * c * (x * x))
    return (0.5 * x) * (1.0 + jnp.tanh(inner))


def _rms_scale(x):
    return lax.rsqrt(jnp.mean(x * x, axis=-1, keepdims=True) + EPS)


def _resident(shape):
    nd = len(shape)
    return pl.BlockSpec(shape, lambda *_: (0,) * nd, pipeline_mode=pl.Buffered(1))


def _mod_kernel(c_ref, w_ref, b_ref, o_ref):
    c = c_ref[...]
    s = c * _sigmoid(c)
    o_ref[...] = jnp.dot(s, w_ref[...], preferred_element_type=F32,
                         precision=lax.Precision.HIGHEST) + b_ref[...]


def _mod_call(c_all, w_mod, b_mod):
    rows = c_all.shape[0]
    n = w_mod.shape[1]
    tn = MOD_COLS_TILE
    return pl.pallas_call(
        _mod_kernel,
        grid=(n // tn,),
        in_specs=[pl.BlockSpec((rows, D_MODEL), lambda j: (0, 0)),
                  pl.BlockSpec((D_MODEL, tn), lambda j: (0, j)),
                  pl.BlockSpec((1, tn), lambda j: (0, j))],
        out_specs=pl.BlockSpec((rows, tn), lambda j: (0, j)),
        out_shape=jax.ShapeDtypeStruct((rows, n), F32),
        compiler_params=pltpu.CompilerParams(dimension_semantics=("arbitrary",),
                                             vmem_limit_bytes=VMEM_LIMIT),
        name="mod",
    )(c_all, w_mod, b_mod.reshape(1, n))


def _gained_tables(gain, tables, scale):
    half = ROPE_AXIS_DIM // 2
    if tables is None:
        return (gain * scale,)
    cos, sin_hi, sin_lo = tables
    return (cos * (gain * scale), sin_hi * (pltpu.roll(gain, HEAD_DIM - half, 1) * scale),
            sin_lo * (pltpu.roll(gain, half, 1) * scale))


def _head_norm_rope(t, gained):
    half = ROPE_AXIS_DIM // 2
    n = t * _rms_scale(t)
    if len(gained) == 1:
        return n * gained[0]
    cos, sin_hi, sin_lo = gained
    return (n * cos + pltpu.roll(n, HEAD_DIM - half, 1) * sin_hi + pltpu.roll(n, half, 1) * sin_lo)


def _in_proj_kernel(*refs, latent, n_row_tiles, n_steps):
    if latent:
        (x0_ref, xn_ref, mod_ref, nrm_ref, w_ref, qn_ref, kn_ref, rc_ref, rh_ref, rl_ref,
         u_ref, g_ref, q_ref, k_ref, vt_ref, ga_ref, gb_ref, h_s) = refs
        tables = (rc_ref[...], rh_ref[...], rl_ref[...])
    else:
        x0_ref, xn_ref, mod_ref, nrm_ref, w_ref, kn_ref, u_ref, k_ref, vt_ref, h_s = refs
        tables = None
    step = pl.program_id(0) * n_row_tiles + pl.program_id(1)

    def normed(x, b):
        shift = mod_ref[b, :, 0:D_MODEL]
        scale = mod_ref[b, :, D_MODEL:2 * D_MODEL]
        return ((x * _rms_scale(x) * nrm_ref[...]) * (1.0 + scale) + shift).astype(BF16)

    @pl.when(step == 0)
    def _():
        h_s[...] = normed(x0_ref[0], 0)

    h = h_s[...]

    def proj(off, width):
        return jnp.dot(h, w_ref[:, off:off + width], preferred_element_type=F32)

    if latent:
        qq = proj(OFF_Q, ATTN_WIDTH)
        q_tables = _gained_tables(qn_ref[...], tables, np.float32(np.log2(np.e) / np.sqrt(HEAD_DIM)))
        for j in range(N_HEADS):
            sl = slice(j * HEAD_DIM, (j + 1) * HEAD_DIM)
            q_ref[0, :, sl] = _head_norm_rope(qq[:, sl], q_tables).astype(BF16)
    kk = proj(OFF_K, KV_WIDTH)
    k_tables = _gained_tables(kn_ref[...], tables, np.float32(1.0))
    for j in range(N_KV_HEADS):
        sl = slice(j * HEAD_DIM, (j + 1) * HEAD_DIM)
        k_ref[0, :, sl] = _head_norm_rope(kk[:, sl], k_tables).astype(BF16)
    if latent:
        g_ref[0] = proj(OFF_G, LRU_WIDTH).astype(BF16)
        ga_ref[0] = (0.5 * jnp.tanh(proj(OFF_GA, D_MODEL)) + 0.5).astype(BF16)
        gb_ref[0] = (0.5 * jnp.tanh(proj(OFF_GB, D_MODEL)) + 0.5).astype(BF16)
    vt_ref[0] = proj(OFF_V, KV_WIDTH).T.astype(BF16)
    u_ref[0] = proj(OFF_U, LRU_WIDTH).astype(BF16)

    nxt = jnp.minimum(step + 1, n_steps - 1)
    b_next = nxt // n_row_tiles if mod_ref.shape[0] > 1 else 0
    h_s[...] = normed(xn_ref[0], b_next)


def _next_tile_map(n_row_tiles, n_steps):
    def index_map(b, l):
        nxt = jnp.minimum(b * n_row_tiles + l + 1, n_steps - 1)
        return (nxt // n_row_tiles, nxt % n_row_tiles, 0)
    return index_map


def _in_proj_call(x, mod, norm_g, w_in_bf, q_gain, k_gain, tables, latent):
    bsz, n_tok, _ = x.shape
    tl = min(TOKEN_TILE, n_tok)
    n_row_tiles = n_tok // tl
    grid = (bsz, n_row_tiles)
    n_steps = bsz * n_row_tiles

    tok = lambda width: pl.BlockSpec((1, tl, width), lambda b, l: (b, l, 0))
    in_specs = [pl.BlockSpec((1, tl, D_MODEL), lambda b, l: (0, 0, 0)),
                pl.BlockSpec((1, tl, D_MODEL), _next_tile_map(n_row_tiles, n_steps)),
                _resident(mod.shape),
                _resident((1, D_MODEL)),
                _resident((D_MODEL, IN_COLS))]
    args = [x, x, mod, norm_g.reshape(1, D_MODEL), w_in_bf]
    vt_spec = pl.BlockSpec((1, KV_WIDTH, tl), lambda b, l: (b, 0, l))
    vt_shape = jax.ShapeDtypeStruct((bsz, KV_WIDTH, n_tok), BF16)
    bf = lambda width: jax.ShapeDtypeStruct((bsz, n_tok, width), BF16)
    if latent:
        in_specs += [_resident((1, HEAD_DIM)), _resident((1, HEAD_DIM))]
        in_specs += [pl.BlockSpec((tl, HEAD_DIM), lambda b, l: (l, 0))] * 3
        args += [q_gain.reshape(1, HEAD_DIM), k_gain.reshape(1, HEAD_DIM), *tables]
        out_specs = [tok(LRU_WIDTH), tok(LRU_WIDTH), tok(ATTN_WIDTH), tok(KV_WIDTH), vt_spec,
                     tok(D_MODEL), tok(D_MODEL)]
        out_shape = [bf(LRU_WIDTH), bf(LRU_WIDTH), bf(ATTN_WIDTH), bf(KV_WIDTH), vt_shape,
                     bf(D_MODEL), bf(D_MODEL)]
    else:
        in_specs += [_resident((1, HEAD_DIM))]
        args += [k_gain.reshape(1, HEAD_DIM)]
        out_specs = [tok(LRU_WIDTH), tok(KV_WIDTH), vt_spec]
        out_shape = [bf(LRU_WIDTH), bf(KV_WIDTH), vt_shape]
    return pl.pallas_call(
        functools.partial(_in_proj_kernel, latent=latent, n_row_tiles=n_row_tiles, n_steps=n_steps),
        grid=grid, in_specs=in_specs, out_specs=out_specs, out_shape=out_shape,
        scratch_shapes=[pltpu.VMEM((tl, D_MODEL), BF16)],
        compiler_params=pltpu.CompilerParams(dimension_semantics=("arbitrary", "arbitrary"),
                                             vmem_limit_bytes=VMEM_LIMIT),
        name="in_proj_latent" if latent else "in_proj_ctx",
    )(*args)


def _lru_kernel(uf_ref, pf_ref, nf_ref, ub_ref, pb_ref, nb_ref, h0_ref, cw_ref, cb_ref, wg_ref, bg_ref,
                lam_ref, perm_ref, permt_ref, halo_ref, *rest, n_tiles, emit_h):
    hf_ref, hb_ref = rest[:2] if emit_h else (None, None)
    hlast_ref, ext_s, a_s, b_s, carry_s = rest[2 if emit_h else 0:]
    t = pl.program_id(0)
    bsz, ts, _ = uf_ref.shape
    dirs = ((uf_ref, pf_ref, nf_ref, hf_ref, t), (ub_ref, pb_ref, nb_ref, hb_ref, n_tiles - 1 - t))

    @pl.when(t == 0)
    def _():
        carry_s[...] = h0_ref[...]

    def load_time_major(d):
        u_ref, prev_ref, next_ref, _, tile = dirs[d]
        for p in range(ts // PERM_STEPS):
            steps = slice(p * PERM_STEPS, (p + 1) * PERM_STEPS)
            u_bm = u_ref[:, steps, :].reshape(bsz * PERM_STEPS, LRU_WIDTH)
            u_tm = jnp.dot(perm_ref[...], u_bm, preferred_element_type=F32)
            ext_s[d, CONV_LEFT + p * PERM_STEPS:CONV_LEFT + (p + 1) * PERM_STEPS] = (
                u_tm.reshape(PERM_STEPS, bsz, LRU_WIDTH))
        edge = jnp.concatenate([prev_ref[:, HALO // 2:, :].reshape(bsz * HALO // 2, LRU_WIDTH),
                                next_ref[:, :HALO // 2, :].reshape(bsz * HALO // 2, LRU_WIDTH)], axis=0)
        halo = jnp.dot(halo_ref[...], edge, preferred_element_type=F32)
        ext_s[d, 0:CONV_LEFT] = (halo[0:CONV_LEFT * bsz]
                                 * jnp.where(tile > 0, 1.0, 0.0)).reshape(CONV_LEFT, bsz, LRU_WIDTH)
        ext_s[d, CONV_LEFT + ts:] = (halo[CONV_LEFT * bsz:(CONV_LEFT + 1) * bsz]
                                     * jnp.where(tile < n_tiles - 1, 1.0, 0.0)).reshape(1, bsz, LRU_WIDTH)

    def coefficients(d):
        xc = cb_ref[...][None]
        for j in range(CONV_WIDTH):
            xc = xc + ext_s[d, j:j + ts] * cw_ref[j:j + 1, :][None]
        xh = xc.reshape(ts * bsz, LRU_WIDTH)
        xb = xh.astype(BF16)
        lam = lam_ref[d]
        sp = jnp.maximum(-lam, 0.0) + jnp.log1p(jnp.exp(-jnp.abs(lam)))
        decay = sp * np.float32(-0.5 * LRU_C * np.log2(np.e))
        for j, k0 in enumerate(GATE_K0):
            cols = slice(j * GATE_COLS, (j + 1) * GATE_COLS)
            gates = jnp.dot(xb[:, k0:k0 + GATE_K], wg_ref[d, j], preferred_element_type=F32)
            t_r = jnp.tanh(gates[:, :GATE_COLS] + bg_ref[d, 0:1, cols])
            t_i = jnp.tanh(gates[:, GATE_COLS:] + bg_ref[d, 1:2, cols])
            a = jnp.exp2(decay[:, cols] * t_r + decay[:, cols])
            om = 1.0 - a * a
            coef = om * lax.rsqrt(jnp.maximum(om, 1e-30))
            a_s[d, :, :, cols] = a.reshape(ts, bsz, GATE_COLS)
            b_s[d, :, :, cols] = (coef * ((t_i + 1.0) * xh[:, cols])).reshape(ts, bsz, GATE_COLS)

    def scan(d):
        h = carry_s[d]
        for s in range(ts):
            idx = (ts - 1 - s) if d == 1 else s
            h = a_s[d, idx] * h + b_s[d, idx]
            b_s[d, idx] = h
        carry_s[d] = h

    def store_batch_major(d):
        h_ref = dirs[d][3]
        if h_ref is None:
            return
        for p in range(ts // PERM_STEPS):
            steps = slice(p * PERM_STEPS, (p + 1) * PERM_STEPS)
            h_tm = b_s[d, steps].reshape(PERM_STEPS * bsz, LRU_WIDTH).astype(BF16)
            h_bm = jnp.dot(permt_ref[...], h_tm, preferred_element_type=F32)
            h_ref[:, steps, :] = h_bm.reshape(bsz, PERM_STEPS, LRU_WIDTH).astype(BF16)

    load_time_major(0)
    load_time_major(1)
    coefficients(0)
    scan(0)
    coefficients(1)
    store_batch_major(0)
    scan(1)
    store_batch_major(1)

    @pl.when(t == n_tiles - 1)
    def _():
        hlast_ref[...] = carry_s[...]


def _perm_matrices(bsz):
    n = bsz * PERM_STEPS
    r_out = np.arange(n)
    s, b = r_out // bsz, r_out % bsz
    perm = np.zeros((n, n), np.float32)
    perm[r_out, b * PERM_STEPS + s] = 1.0
    half = HALO // 2
    halo = np.zeros((CONV_WIDTH * bsz, 2 * bsz * half), np.float32)
    for b_ in range(bsz):
        for k in range(CONV_LEFT):
            halo[k * bsz + b_, b_ * half + half - CONV_LEFT + k] = 1.0
        halo[CONV_LEFT * bsz + b_, bsz * half + b_ * half] = 1.0
    return jnp.asarray(perm, BF16), jnp.asarray(perm.T, BF16), jnp.asarray(halo, BF16)


def _lru_call(u, h0, conv_w, conv_b, w_gate, b_gate, lam, emit_h):
    bsz, n_tok, _ = u.shape
    ts = SCAN_TILE
    n_tiles = n_tok // ts
    per = ts // HALO
    perm, perm_t, halo = _perm_matrices(bsz)

    def tile_specs(pos):
        return [pl.BlockSpec((bsz, ts, LRU_WIDTH), lambda t: (0, pos(t), 0)),
                pl.BlockSpec((bsz, HALO, LRU_WIDTH), lambda t: (0, jnp.maximum(pos(t) * per - 1, 0), 0)),
                pl.BlockSpec((bsz, HALO, LRU_WIDTH),
                             lambda t: (0, jnp.minimum((pos(t) + 1) * per, n_tok // HALO - 1), 0))]

    fwd = lambda t: t
    bwd = lambda t: n_tiles - 1 - t
    in_specs = tile_specs(fwd) + tile_specs(bwd) + [
        _resident((2, bsz, LRU_WIDTH)),
        _resident((CONV_WIDTH, LRU_WIDTH)),
        _resident((1, LRU_WIDTH)),
        _resident((2, len(GATE_K0), GATE_K, 2 * GATE_COLS)),
        _resident((2, 2, LRU_WIDTH)),
        _resident((2, 1, LRU_WIDTH)),
        _resident(perm.shape), _resident(perm_t.shape), _resident(halo.shape),
    ]
    out_specs = [pl.BlockSpec((bsz, ts, LRU_WIDTH), lambda t: (0, fwd(t), 0)),
                 pl.BlockSpec((bsz, ts, LRU_WIDTH), lambda t: (0, bwd(t), 0)),
                 pl.BlockSpec((2, bsz, LRU_WIDTH), lambda t: (0, 0, 0))]
    out_shape = [jax.ShapeDtypeStruct((bsz, n_tok, LRU_WIDTH), BF16),
                 jax.ShapeDtypeStruct((bsz, n_tok, LRU_WIDTH), BF16),
                 jax.ShapeDtypeStruct((2, bsz, LRU_WIDTH), F32)]
    if not emit_h:
        out_specs, out_shape = out_specs[2:], out_shape[2:]
    return pl.pallas_call(
        functools.partial(_lru_kernel, n_tiles=n_tiles, emit_h=emit_h),
        grid=(n_tiles,), in_specs=in_specs, out_specs=out_specs, out_shape=out_shape,
        scratch_shapes=[pltpu.VMEM((2, ts + CONV_WIDTH - 1, bsz, LRU_WIDTH), F32),
                        pltpu.VMEM((2, ts, bsz, LRU_WIDTH), F32),
                        pltpu.VMEM((2, ts, bsz, LRU_WIDTH), F32),
                        pltpu.VMEM((2, bsz, LRU_WIDTH), F32)],
        compiler_params=pltpu.CompilerParams(dimension_semantics=("arbitrary",),
                                             vmem_limit_bytes=VMEM_LIMIT),
        name="lru",
    )(u, u, u, u, u, u, h0, 0.5 * conv_w, 0.5 * conv_b.reshape(1, LRU_WIDTH), w_gate, b_gate,
      lam.reshape(2, 1, LRU_WIDTH), perm, perm_t, halo)


def _cast_specs(weights, n_steps, step_of):
    in_specs, out_specs, out_shape = [], [], []
    for w in weights:
        rows, cols = w.shape
        n_blocks = rows // CAST_ROWS
        assert rows % CAST_ROWS == 0 and n_blocks <= n_steps
        index_map = lambda *idx, n_blocks=n_blocks: (jnp.minimum(step_of(*idx), n_blocks - 1), 0)
        in_specs.append(pl.BlockSpec((CAST_ROWS, cols), index_map))
        out_specs.append(pl.BlockSpec((CAST_ROWS, cols), index_map))
        out_shape.append(jax.ShapeDtypeStruct((rows, cols), BF16))
    return in_specs, out_specs, out_shape


def _cast_blocks(w_refs, o_refs):
    for w_ref, o_ref in zip(w_refs, o_refs):
        o_ref[...] = w_ref[...].astype(BF16)


def _attn_kernel(q_ref, kc_ref, vc_ref, kl_ref, vl_ref, *rest):
    n_cast = (len(rest) - 6) // 2
    w_refs, o_ref, wo_refs = rest[:n_cast], rest[n_cast], rest[n_cast + 1:2 * n_cast + 1]
    sa_s, sb_s, acc_s, m_s, l_s = rest[2 * n_cast + 1:]
    _cast_blocks(w_refs, wo_refs)
    n_chunks = kl_ref.shape[1] // KEY_CHUNK

    def scores(k, g):
        qg = q_ref[0, :, g * HEAD_DIM:(g + 1) * HEAD_DIM]
        return lax.dot_general(k, qg, (((1,), (1,)), ((), ())), preferred_element_type=F32)

    def softmax_pv(st, vt, g, first):
        m_c = jnp.max(st, axis=0, keepdims=True)
        if first:
            m_new = m_c
        else:
            m_old = m_s[g]
            m_new = jnp.maximum(m_old, m_c)
            alpha = jnp.exp2(m_old - m_new)
        p = jnp.exp2(st - m_new)
        p_sum = jnp.sum(p, axis=0, keepdims=True)
        pv = jnp.dot(vt, p.astype(BF16), preferred_element_type=F32)
        if first:
            l_s[g] = p_sum
            acc_s[g] = pv
        else:
            l_s[g] = alpha * l_s[g] + p_sum
            acc_s[g] = alpha * acc_s[g] + pv
        m_s[g] = m_new

    def lat_keys(c):
        return kl_ref[0, pl.ds(pl.multiple_of(c * KEY_CHUNK, KEY_CHUNK), KEY_CHUNK), :]

    def lat_vals(c):
        return vl_ref[0, :, pl.ds(pl.multiple_of(c * KEY_CHUNK, KEY_CHUNK), KEY_CHUNK)]

    def stage(k_next, st_in, st_out, vt, first=False):
        for g in range(GROUP):
            if k_next is not None:
                st_out[g] = scores(k_next, g)
            softmax_pv(st_in[g], vt, g, first)

    k0 = kl_ref[0, 0:KEY_CHUNK, :]
    for g in range(GROUP):
        sc = scores(kc_ref[0], g)
        sa_s[g] = scores(k0, g)
        softmax_pv(sc, vc_ref[0], g, True)

    def body(i, carry):
        stage(lat_keys(2 * i + 1), sa_s, sb_s, lat_vals(2 * i))
        stage(lat_keys(2 * i + 2), sb_s, sa_s, lat_vals(2 * i + 1))
        return carry

    lax.fori_loop(0, n_chunks // 2 - 1, body, 0)
    stage(lat_keys(n_chunks - 1), sa_s, sb_s, lat_vals(n_chunks - 2))
    stage(None, sb_s, None, lat_vals(n_chunks - 1))
    for g in range(GROUP):
        o = (acc_s[g] / l_s[g]).T
        o_ref[0, :, g * HEAD_DIM:(g + 1) * HEAD_DIM] = o.astype(BF16)


def _attn_call(q, k_ctx, vt_ctx, k_lat, vt_lat, *weights):
    bsz, n_tok, _ = q.shape
    n_ctx = k_ctx.shape[1]
    tq = Q_TILE
    gw = GROUP * HEAD_DIM
    n_qt = n_tok // tq
    keys = lambda n: pl.BlockSpec((1, n, HEAD_DIM), lambda b, j, i: (b, 0, j))
    vals = lambda n: pl.BlockSpec((1, HEAD_DIM, n), lambda b, j, i: (b, j, 0))
    w_in, w_out, w_shape = _cast_specs(weights, bsz * N_KV_HEADS * n_qt,
                                       lambda b, j, i: (b * N_KV_HEADS + j) * n_qt + i)
    return pl.pallas_call(
        _attn_kernel,
        grid=(bsz, N_KV_HEADS, n_qt),
        in_specs=[pl.BlockSpec((1, tq, gw), lambda b, j, i: (b, i, j)),
                  keys(n_ctx), pl.BlockSpec((1, HEAD_DIM, n_ctx), lambda b, j, i: (0, j, b)),
                  keys(n_tok), vals(n_tok)] + w_in,
        out_specs=[pl.BlockSpec((1, tq, gw), lambda b, j, i: (b, i, j))] + w_out,
        out_shape=[jax.ShapeDtypeStruct((bsz, n_tok, ATTN_WIDTH), BF16)] + w_shape,
        scratch_shapes=[pltpu.VMEM((GROUP, KEY_CHUNK, tq), F32),
                        pltpu.VMEM((GROUP, KEY_CHUNK, tq), F32),
                        pltpu.VMEM((GROUP, HEAD_DIM, tq), F32),
                        pltpu.VMEM((GROUP, 1, tq), F32),
                        pltpu.VMEM((GROUP, 1, tq), F32)],
        compiler_params=pltpu.CompilerParams(
            dimension_semantics=("arbitrary", "arbitrary", "arbitrary"),
            vmem_limit_bytes=VMEM_LIMIT),
        name="attn",
    )(q, k_ctx, vt_ctx, k_lat, vt_lat, *weights)


def _attn_fixed_shift_kernel(shift_ref, q_ref, kc_ref, vc_ref, kl_ref, vl_ref, *rest):
    n_cast = (len(rest) - 1) // 2
    w_refs, o_ref, wo_refs = rest[:n_cast], rest[n_cast], rest[n_cast + 1:]
    _cast_blocks(w_refs, wo_refs)
    tq = Q_TILE
    shift = shift_ref[0]
    for t in range(q_ref.shape[1] // tq):
        rows = slice(t * tq, (t + 1) * tq)
        for j in range(N_KV_HEADS):
            base = j * GROUP * HEAD_DIM
            kv = slice(j * HEAD_DIM, (j + 1) * HEAD_DIM)
            q_all = jnp.concatenate([q_ref[0, rows, base + g * HEAD_DIM:base + (g + 1) * HEAD_DIM]
                                     for g in range(GROUP)], axis=0)

            def unnormalised(k, vt):
                st = lax.dot_general(k, q_all, (((1,), (1,)), ((), ())), preferred_element_type=F32)
                p = jnp.exp2(st - shift)
                pv = jnp.dot(vt, p.astype(BF16), preferred_element_type=F32)
                return jnp.sum(p, axis=0, keepdims=True), pv

            l_ctx, acc_ctx = unnormalised(kc_ref[0, :, kv], vc_ref[0, kv, :])
            l_lat, acc_lat = unnormalised(kl_ref[0, :, kv], vl_ref[0, kv, :])
            o_t = (acc_ctx + acc_lat) / (l_ctx + l_lat)
            for g in range(GROUP):
                o_ref[0, rows, base + g * HEAD_DIM:base + (g + 1) * HEAD_DIM] = (
                    o_t[:, g * tq:(g + 1) * tq].T.astype(BF16))


def _attn_fixed_shift_call(shift, q, k_ctx, vt_ctx, k_lat, vt_lat, *weights):
    bsz, n_tok, _ = q.shape
    n_ctx = k_ctx.shape[1]
    rows = FIXED_SHIFT_Q_TILES * Q_TILE
    n_qt = n_tok // rows
    keys = lambda n: pl.BlockSpec((1, n, KV_WIDTH), lambda b, i: (b, 0, 0))
    vals = lambda n: pl.BlockSpec((1, KV_WIDTH, n), lambda b, i: (b, 0, 0))
    w_in, w_out, w_shape = _cast_specs(weights, bsz * n_qt, lambda b, i: b * n_qt + i)
    return pl.pallas_call(
        _attn_fixed_shift_kernel,
        grid=(bsz, n_qt),
        in_specs=[pl.BlockSpec(memory_space=pltpu.SMEM),
                  pl.BlockSpec((1, rows, ATTN_WIDTH), lambda b, i: (b, i, 0)),
                  keys(n_ctx), pl.BlockSpec((1, KV_WIDTH, n_ctx), lambda b, i: (0, 0, b)),
                  keys(n_tok), vals(n_tok)] + w_in,
        out_specs=[pl.BlockSpec((1, rows, ATTN_WIDTH), lambda b, i: (b, i, 0))] + w_out,
        out_shape=[jax.ShapeDtypeStruct((bsz, n_tok, ATTN_WIDTH), BF16)] + w_shape,
        compiler_params=pltpu.CompilerParams(
            dimension_semantics=("arbitrary", "arbitrary"),
            vmem_limit_bytes=VMEM_LIMIT),
        name="attn_fixed_shift",
    )(shift, q, k_ctx, vt_ctx, k_lat, vt_lat, *weights)


def _attention(q, k_ctx, vt_ctx, k_lat, vt_lat, q_gain, k_gain, weights):
    bound = (SCORE_BOUND_SLACK * np.float32(np.sqrt(HEAD_DIM) * np.log2(np.e))
             * jnp.max(jnp.abs(q_gain)) * jnp.max(jnp.abs(k_gain)))
    operands = (q, k_ctx, vt_ctx, k_lat, vt_lat, *weights)
    return lax.cond(bound <= MAX_FIXED_SHIFT,
                    lambda ops: _attn_fixed_shift_call(bound.reshape(1), *ops),
                    lambda ops: _attn_call(*ops), operands)


def _mix_out_kernel(hf_ref, hb_ref, g_ref, at_ref, ga_ref, gb_ref, x_ref, mod_ref,
                    wl_ref, wa_ref, wo_ref, o_ref):
    lru = hf_ref[0].astype(F32) + hb_ref[0].astype(F32)
    za = (lru * _gelu_tanh(g_ref[0].astype(F32))).astype(BF16)
    ya = jnp.dot(za, wl_ref[...], preferred_element_type=F32)
    yb = jnp.dot(at_ref[0], wa_ref[...], preferred_element_type=F32)
    mix = (ga_ref[0].astype(F32) * ya + gb_ref[0].astype(F32) * yb).astype(BF16)
    y = jnp.dot(mix, wo_ref[...], preferred_element_type=F32)
    o_ref[0] = x_ref[0] + mod_ref[0, :, 2 * D_MODEL:3 * D_MODEL] * y


def _mix_out_call(h_fwd, h_bwd, gg, attn, sga, sgb, x, mod, w_lru_bf, w_attn_bf, w_out_bf):
    bsz, n_tok, _ = x.shape
    tl = TOKEN_TILE
    tok = lambda width: pl.BlockSpec((1, tl, width), lambda b, l: (b, l, 0))
    return pl.pallas_call(
        _mix_out_kernel,
        grid=(bsz, n_tok // tl),
        in_specs=[tok(LRU_WIDTH), tok(LRU_WIDTH), tok(LRU_WIDTH), tok(ATTN_WIDTH), tok(D_MODEL),
                  tok(D_MODEL), tok(D_MODEL),
                  pl.BlockSpec((1, 1, mod.shape[-1]), lambda b, l: (b, 0, 0)),
                  _resident((LRU_WIDTH, D_MODEL)), _resident((ATTN_WIDTH, D_MODEL)),
                  _resident((D_MODEL, D_MODEL))],
        out_specs=tok(D_MODEL),
        out_shape=jax.ShapeDtypeStruct((bsz, n_tok, D_MODEL), F32),
        compiler_params=pltpu.CompilerParams(dimension_semantics=("arbitrary", "arbitrary"),
                                             vmem_limit_bytes=VMEM_LIMIT),
        name="mix_out",
    )(h_fwd, h_bwd, gg, attn, sga, sgb, x, mod, w_lru_bf, w_attn_bf, w_out_bf)


def _ffn_kernel(x_ref, mod_ref, nrm_ref, wi_ref, wo_ref, o_ref):
    x = x_ref[0]
    shift = mod_ref[0, :, 3 * D_MODEL:4 * D_MODEL]
    scale = mod_ref[0, :, 4 * D_MODEL:5 * D_MODEL]
    gate_out = mod_ref[0, :, 5 * D_MODEL:6 * D_MODEL]
    h = ((x * _rms_scale(x) * nrm_ref[...]) * (1.0 + scale) + shift).astype(BF16)
    acc = jnp.zeros(x.shape, F32)
    for lo, hi in zip(FFN_SPLITS[:-1], FFN_SPLITS[1:]):
        gate = jnp.dot(h, wi_ref[:, lo:hi], preferred_element_type=F32)
        up = jnp.dot(h, wi_ref[:, FFN_HIDDEN + lo:FFN_HIDDEN + hi], preferred_element_type=F32)
        act = (gate * _sigmoid(gate) * up).astype(BF16)
        acc = acc + jnp.dot(act, wo_ref[lo:hi, :], preferred_element_type=F32)
    o_ref[0] = x + gate_out * acc


def _ffn_call(x, mod, norm_g, w_in_bf, w_out_bf):
    bsz, n_tok, _ = x.shape
    tl = FFN_TOKEN_TILE
    tok = pl.BlockSpec((1, tl, D_MODEL), lambda b, l: (b, l, 0))
    return pl.pallas_call(
        _ffn_kernel,
        grid=(bsz, n_tok // tl),
        in_specs=[tok, pl.BlockSpec((1, 1, mod.shape[-1]), lambda b, l: (b, 0, 0)),
                  _resident((1, D_MODEL)),
                  _resident((D_MODEL, 2 * FFN_HIDDEN)), _resident((FFN_HIDDEN, D_MODEL))],
        out_specs=tok,
        out_shape=jax.ShapeDtypeStruct((bsz, n_tok, D_MODEL), F32),
        compiler_params=pltpu.CompilerParams(dimension_semantics=("arbitrary", "arbitrary"),
                                             vmem_limit_bytes=VMEM_LIMIT),
        name="ffn",
    )(x, mod, norm_g.reshape(1, D_MODEL), w_in_bf, w_out_bf)


def _rope_tables(n_tok):
    pos = np.arange(n_tok)
    inv_freq = ROPE_THETA ** (-np.arange(0, ROPE_AXIS_DIM, 2, dtype=np.float64) / ROPE_AXIS_DIM)
    ang_r = (pos // GRID_W)[:, None] * inv_freq[None, :]
    ang_c = (pos % GRID_W)[:, None] * inv_freq[None, :]
    cr, sr, cc, sc = (f(a) for a in (ang_r, ang_c) for f in (np.cos, np.sin))
    zero = np.zeros_like(sr)
    cos = np.concatenate([cr, cr, cc, cc], axis=-1)
    sin_hi = np.concatenate([-sr, zero, -sc, zero], axis=-1)
    sin_lo = np.concatenate([zero, sr, zero, sc], axis=-1)
    return tuple(jnp.asarray(t, F32) for t in (cos, sin_hi, sin_lo))


def _gate_weights(wa, wx):
    def dense(w):
        rows = [jnp.pad(w[n], ((0, 0), (n * LRU_BLOCK_DIM, LRU_WIDTH - (n + 1) * LRU_BLOCK_DIM)))
                for n in range(LRU_BLOCKS)]
        return jnp.concatenate(rows, axis=0)

    da, dx = dense(wa), dense(wx)
    tiles = []
    for j, k0 in enumerate(GATE_K0):
        cols = slice(j * GATE_COLS, (j + 1) * GATE_COLS)
        tiles.append(jnp.concatenate([da[k0:k0 + GATE_K, cols], dx[k0:k0 + GATE_K, cols]], axis=1))
    return jnp.stack(tiles).astype(BF16)


def kernel(x, c, ctx, c_ctx, w_mod, b_mod, norm_mix, w_in, conv_w, conv_b, lru_wa, lru_ba, lru_wx,
           lru_bx, lru_lambda, q_norm, k_norm, w_out_lru, w_out_attn, w_out, norm_ffn, w_ffn_in,
           w_ffn_out):
    bsz, n_tok, _ = x.shape
    assert w_mod.shape[0] == 1, "single trunk layer"
    tables = _rope_tables(n_tok)

    c_all = jnp.concatenate([c, c_ctx[None, :]], axis=0)
    c_all = jnp.pad(c_all, ((0, -c_all.shape[0] % SUBLANES), (0, 0)))
    mod_all = _mod_call(c_all, w_mod[0], b_mod[0])
    mod = mod_all[:bsz].reshape(bsz, 1, -1)
    mod_c = mod_all[bsz:bsz + 1].reshape(1, 1, -1)

    col_scale = np.where(np.arange(IN_COLS) >= OFF_GA, 0.5, 1.0).astype(np.float32)
    w_in_bf = (w_in[0] * col_scale).astype(BF16)
    n_ctx = ctx.shape[1]
    u_c, k_c, vt_c = _in_proj_call(ctx.reshape(1, bsz * n_ctx, D_MODEL), mod_c, norm_mix[0], w_in_bf,
                                   None, k_norm[0], None, latent=False)
    u_c = u_c.reshape(bsz, n_ctx, LRU_WIDTH)
    k_c = k_c.reshape(bsz, n_ctx, KV_WIDTH)
    u_l, gg, q, k_l, vt_l, sga, sgb = _in_proj_call(x, mod, norm_mix[0], w_in_bf, q_norm[0], k_norm[0],
                                                    tables, latent=True)

    w_gate = jnp.stack([_gate_weights(lru_wa[0, d], lru_wx[0, d]) for d in range(2)])
    b_gate = 0.5 * jnp.stack([lru_ba[0], lru_bx[0]], axis=1)
    lru = functools.partial(_lru_call, conv_w=conv_w[0], conv_b=conv_b[0], w_gate=w_gate, b_gate=b_gate,
                            lam=lru_lambda[0])
    (h_seed,) = lru(u_c, jnp.zeros((2, bsz, LRU_WIDTH), F32), emit_h=False)
    h_fwd, h_bwd, _ = lru(u_l, h_seed, emit_h=True)

    later_weights = (w_out_lru[0], w_out_attn[0], w_out[0], w_ffn_in[0], w_ffn_out[0])
    attn, w_lru_bf, w_attn_bf, w_out_bf, w_ffn_in_bf, w_ffn_out_bf = _attention(
        q, k_c, vt_c, k_l, vt_l, q_norm[0], k_norm[0], later_weights)

    x1 = _mix_out_call(h_fwd, h_bwd, gg, attn, sga, sgb, x, mod, w_lru_bf, w_attn_bf, w_out_bf)
    return _ffn_call(x1, mod, norm_ffn[0], w_ffn_in_bf, w_ffn_out_bf)
```

```python
import functools

import jax
import jax.numpy as jnp
import numpy as np
from jax import lax
from jax.experimental import pallas as pl
from jax.experimental.pallas import tpu as pltpu

D_MODEL = 1024
GRID_W = 64
EPS = 1e-6
LRU_WIDTH = 1280
LRU_BLOCKS = 8
LRU_BLOCK_DIM = LRU_WIDTH // LRU_BLOCKS
LRU_C = 8.0
CONV_WIDTH = 4
CONV_LEFT = 2
HEAD_DIM = 128
LANES = 128
SUBLANES = 8
N_HEADS = 8
N_KV_HEADS = 2
GROUP = N_HEADS // N_KV_HEADS
ATTN_WIDTH = N_HEADS * HEAD_DIM
KV_WIDTH = N_KV_HEADS * HEAD_DIM
ROPE_AXIS_DIM = HEAD_DIM // 2
ROPE_THETA = 10000.0
FFN_HIDDEN = 2816

OFF_U = 0
OFF_G = OFF_U + LRU_WIDTH
OFF_Q = OFF_G + LRU_WIDTH
OFF_K = OFF_Q + ATTN_WIDTH
OFF_V = OFF_K + KV_WIDTH
OFF_GA = OFF_V + KV_WIDTH
OFF_GB = OFF_GA + D_MODEL
IN_COLS = OFF_GB + D_MODEL

MOD_COLS_TILE = 1024
TOKEN_TILE = 512
SCAN_TILE = 64
PERM_STEPS = 32
HALO = 16
GATE_COLS = 256
GATE_K = 512
Q_TILE = 256
FIXED_SHIFT_Q_TILES = 2
CAST_ROWS = 64
KEY_CHUNK = 512
MXU_TILE = 256
MIX_TILES = 2
FFN_TOKEN_TILE = 1024
FFN_SPLITS = (0, 4 * MXU_TILE, 8 * MXU_TILE, FFN_HIDDEN)
SCORE_BOUND_SLACK = 1.02
MAX_FIXED_SHIFT = 60.0
VMEM_LIMIT = 56 * 1024 * 1024

F32 = jnp.float32
BF16 = jnp.bfloat16


def _gate_k_offsets():
    offs = []
    for j in range(LRU_WIDTH // GATE_COLS):
        first_row = (j * GATE_COLS // LRU_BLOCK_DIM) * LRU_BLOCK_DIM
        last_row = ((j + 1) * GATE_COLS - 1) // LRU_BLOCK_DIM * LRU_BLOCK_DIM + LRU_BLOCK_DIM
        k0 = min(first_row // LANES * LANES, LRU_WIDTH - GATE_K)
        assert k0 <= first_row and last_row <= k0 + GATE_K
        offs.append(k0)
    return tuple(offs)


GATE_K0 = _gate_k_offsets()


def _sigmoid(x):
    return 0.5 * jnp.tanh(0.5 * x) + 0.5


def _gelu_tanh(x):
    c = np.float32(np.sqrt(2.0 / np.pi))
    inner = x * (c + np.float32(0.044715) * c * (x * x))
    return (0.5 * x) * (1.0 + jnp.tanh(inner))


def _rms_scale(x):
    return lax.rsqrt(jnp.mean(x * x, axis=-1, keepdims=True) + EPS)


def _resident(shape):
    nd = len(shape)
    return pl.BlockSpec(shape, lambda *_: (0,) * nd, pipeline_mode=pl.Buffered(1))


def _mod_kernel(c_ref, w_ref, b_ref, o_ref):
    c = c_ref[...]
    s = c * _sigmoid(c)
    o_ref[...] = jnp.dot(s, w_ref[...], preferred_element_type=F32,
                         precision=lax.Precision.HIGHEST) + b_ref[...]


def _mod_call(c_all, w_mod, b_mod):
    rows = c_all.shape[0]
    n = w_mod.shape[1]
    tn = MOD_COLS_TILE
    return pl.pallas_call(
        _mod_kernel,
        grid=(n // tn,),
        in_specs=[pl.BlockSpec((rows, D_MODEL), lambda j: (0, 0)),
                  pl.BlockSpec((D_MODEL, tn), lambda j: (0, j)),
                  pl.BlockSpec((1, tn), lambda j: (0, j))],
        out_specs=pl.BlockSpec((rows, tn), lambda j: (0, j)),
        out_shape=jax.ShapeDtypeStruct((rows, n), F32),
        compiler_params=pltpu.CompilerParams(dimension_semantics=("arbitrary",),
                                             vmem_limit_bytes=VMEM_LIMIT),
        name="mod",
    )(c_all, w_mod, b_mod.reshape(1, n))


def _gained_tables(gain, tables, scale):
    half = ROPE_AXIS_DIM // 2
    if tables is None:
        return (gain * scale,)
    cos, sin_hi, sin_lo = tables
    return (cos * (gain * scale), sin_hi * (pltpu.roll(gain, HEAD_DIM - half, 1) * scale),
            sin_lo * (pltpu.roll(gain, half, 1) * scale))


def _head_norm_rope(t, gained):
    half = ROPE_AXIS_DIM // 2
    n = t * _rms_scale(t)
    if len(gained) == 1:
        return n * gained[0]
    cos, sin_hi, sin_lo = gained
    return (n * cos + pltpu.roll(n, HEAD_DIM - half, 1) * sin_hi + pltpu.roll(n, half, 1) * sin_lo)


def _in_proj_kernel(*refs, latent, n_row_tiles, n_steps):
    if latent:
        (x0_ref, xn_ref, mod_ref, nrm_ref, w_ref, qn_ref, kn_ref, rc_ref, rh_ref, rl_ref,
         u_ref, g_ref, q_ref, k_ref, vt_ref, ga_ref, gb_ref, h_s) = refs
        tables = (rc_ref[...], rh_ref[...], rl_ref[...])
    else:
        x0_ref, xn_ref, mod_ref, nrm_ref, w_ref, kn_ref, u_ref, k_ref, vt_ref, h_s = refs
        tables = None
    step = pl.program_id(0) * n_row_tiles + pl.program_id(1)

    def normed(x, b):
        shift = mod_ref[b, :, 0:D_MODEL]
        scale = mod_ref[b, :, D_MODEL:2 * D_MODEL]
        return ((x * _rms_scale(x) * nrm_ref[...]) * (1.0 + scale) + shift).astype(BF16)

    @pl.when(step == 0)
    def _():
        h_s[...] = normed(x0_ref[0], 0)

    h = h_s[...]

    def proj(off, width):
        return jnp.dot(h, w_ref[:, off:off + width], preferred_element_type=F32)

    if latent:
        qq = proj(OFF_Q, ATTN_WIDTH)
        q_tables = _gained_tables(qn_ref[...], tables, np.float32(np.log2(np.e) / np.sqrt(HEAD_DIM)))
        for j in range(N_HEADS):
            sl = slice(j * HEAD_DIM, (j + 1) * HEAD_DIM)
            q_ref[0, :, sl] = _head_norm_rope(qq[:, sl], q_tables).astype(BF16)
    kk = proj(OFF_K, KV_WIDTH)
    k_tables = _gained_tables(kn_ref[...], tables, np.float32(1.0))
    for j in range(N_KV_HEADS):
        sl = slice(j * HEAD_DIM, (j + 1) * HEAD_DIM)
        k_ref[0, :, sl] = _head_norm_rope(kk[:, sl], k_tables).astype(BF16)
    if latent:
        g_ref[0] = proj(OFF_G, LRU_WIDTH).astype(BF16)
        ga_ref[0] = (0.5 * jnp.tanh(proj(OFF_GA, D_MODEL)) + 0.5).astype(BF16)
        gb_ref[0] = (0.5 * jnp.tanh(proj(OFF_GB, D_MODEL)) + 0.5).astype(BF16)
    vt_ref[0] = proj(OFF_V, KV_WIDTH).T.astype(BF16)
    u_ref[0] = proj(OFF_U, LRU_WIDTH).astype(BF16)

    nxt = jnp.minimum(step + 1, n_steps - 1)
    b_next = nxt // n_row_tiles if mod_ref.shape[0] > 1 else 0
    h_s[...] = normed(xn_ref[0], b_next)


def _next_tile_map(n_row_tiles, n_steps):
    def index_map(b, l):
        nxt = jnp.minimum(b * n_row_tiles + l + 1, n_steps - 1)
        return (nxt // n_row_tiles, nxt % n_row_tiles, 0)
    return index_map


def _in_proj_call(x, mod, norm_g, w_in_bf, q_gain, k_gain, tables, latent):
    bsz, n_tok, _ = x.shape
    tl = min(TOKEN_TILE, n_tok)
    n_row_tiles = n_tok // tl
    grid = (bsz, n_row_tiles)
    n_steps = bsz * n_row_tiles

    tok = lambda width: pl.BlockSpec((1, tl, width), lambda b, l: (b, l, 0))
    in_specs = [pl.BlockSpec((1, tl, D_MODEL), lambda b, l: (0, 0, 0)),
                pl.BlockSpec((1, tl, D_MODEL), _next_tile_map(n_row_tiles, n_steps)),
                _resident(mod.shape),
                _resident((1, D_MODEL)),
                _resident((D_MODEL, IN_COLS))]
    args = [x, x, mod, norm_g.reshape(1, D_MODEL), w_in_bf]
    vt_spec = pl.BlockSpec((1, KV_WIDTH, tl), lambda b, l: (b, 0, l))
    vt_shape = jax.ShapeDtypeStruct((bsz, KV_WIDTH, n_tok), BF16)
    bf = lambda width: jax.ShapeDtypeStruct((bsz, n_tok, width), BF16)
    if latent:
        in_specs += [_resident((1, HEAD_DIM)), _resident((1, HEAD_DIM))]
        in_specs += [pl.BlockSpec((tl, HEAD_DIM), lambda b, l: (l, 0))] * 3
        args += [q_gain.reshape(1, HEAD_DIM), k_gain.reshape(1, HEAD_DIM), *tables]
        out_specs = [tok(LRU_WIDTH), tok(LRU_WIDTH), tok(ATTN_WIDTH), tok(KV_WIDTH), vt_spec,
                     tok(D_MODEL), tok(D_MODEL)]
        out_shape = [bf(LRU_WIDTH), bf(LRU_WIDTH), bf(ATTN_WIDTH), bf(KV_WIDTH), vt_shape,
                     bf(D_MODEL), bf(D_MODEL)]
    else:
        in_specs += [_resident((1, HEAD_DIM))]
        args += [k_gain.reshape(1, HEAD_DIM)]
        out_specs = [tok(LRU_WIDTH), tok(KV_WIDTH), vt_spec]
        out_shape = [bf(LRU_WIDTH), bf(KV_WIDTH), vt_shape]
    return pl.pallas_call(
        functools.partial(_in_proj_kernel, latent=latent, n_row_tiles=n_row_tiles, n_steps=n_steps),
        grid=grid, in_specs=in_specs, out_specs=out_specs, out_shape=out_shape,
        scratch_shapes=[pltpu.VMEM((tl, D_MODEL), BF16)],
        compiler_params=pltpu.CompilerParams(dimension_semantics=("arbitrary", "arbitrary"),
                                             vmem_limit_bytes=VMEM_LIMIT),
        name="in_proj_latent" if latent else "in_proj_ctx",
    )(*args)


def _lru_kernel(uf_ref, pf_ref, nf_ref, ub_ref, pb_ref, nb_ref, h0_ref, cw_ref, cb_ref, wg_ref, bg_ref,
                lam_ref, perm_ref, permt_ref, halo_ref, *rest, n_tiles, emit_h):
    hf_ref, hb_ref = rest[:2] if emit_h else (None, None)
    hlast_ref, ext_s, a_s, b_s, carry_s = rest[2 if emit_h else 0:]
    t = pl.program_id(0)
    bsz, ts, _ = uf_ref.shape
    dirs = ((uf_ref, pf_ref, nf_ref, hf_ref, t), (ub_ref, pb_ref, nb_ref, hb_ref, n_tiles - 1 - t))

    @pl.when(t == 0)
    def _():
        carry_s[...] = h0_ref[...]

    def load_time_major(d):
        u_ref, prev_ref, next_ref, _, tile = dirs[d]
        for p in range(ts // PERM_STEPS):
            steps = slice(p * PERM_STEPS, (p + 1) * PERM_STEPS)
            u_bm = u_ref[:, steps, :].reshape(bsz * PERM_STEPS, LRU_WIDTH)
            u_tm = jnp.dot(perm_ref[...], u_bm, preferred_element_type=F32)
            ext_s[d, CONV_LEFT + p * PERM_STEPS:CONV_LEFT + (p + 1) * PERM_STEPS] = (
                u_tm.reshape(PERM_STEPS, bsz, LRU_WIDTH))
        edge = jnp.concatenate([prev_ref[:, HALO // 2:, :].reshape(bsz * HALO // 2, LRU_WIDTH),
                                next_ref[:, :HALO // 2, :].reshape(bsz * HALO // 2, LRU_WIDTH)], axis=0)
        halo = jnp.dot(halo_ref[...], edge, preferred_element_type=F32)
        ext_s[d, 0:CONV_LEFT] = (halo[0:CONV_LEFT * bsz]
                                 * jnp.where(tile > 0, 1.0, 0.0)).reshape(CONV_LEFT, bsz, LRU_WIDTH)
        ext_s[d, CONV_LEFT + ts:] = (halo[CONV_LEFT * bsz:(CONV_LEFT + 1) * bsz]
                                     * jnp.where(tile < n_tiles - 1, 1.0, 0.0)).reshape(1, bsz, LRU_WIDTH)

    def coefficients(d):
        xc = cb_ref[...][None]
        for j in range(CONV_WIDTH):
            xc = xc + ext_s[d, j:j + ts] * cw_ref[j:j + 1, :][None]
        xh = xc.reshape(ts * bsz, LRU_WIDTH)
        xb = xh.astype(BF16)
        lam = lam_ref[d]
        sp = jnp.maximum(-lam, 0.0) + jnp.log1p(jnp.exp(-jnp.abs(lam)))
        decay = sp * np.float32(-0.5 * LRU_C * np.log2(np.e))
        for j, k0 in enumerate(GATE_K0):
            cols = slice(j * GATE_COLS, (j + 1) * GATE_COLS)
            gates = jnp.dot(xb[:, k0:k0 + GATE_K], wg_ref[d, j], preferred_element_type=F32)
            t_r = jnp.tanh(gates[:, :GATE_COLS] + bg_ref[d, 0:1, cols])
            t_i = jnp.tanh(gates[:, GATE_COLS:] + bg_ref[d, 1:2, cols])
            a = jnp.exp2(decay[:, cols] * t_r + decay[:, cols])
            om = 1.0 - a * a
            coef = om * lax.rsqrt(jnp.maximum(om, 1e-30))
            a_s[d, :, :, cols] = a.reshape(ts, bsz, GATE_COLS)
            b_s[d, :, :, cols] = (coef * ((t_i + 1.0) * xh[:, cols])).reshape(ts, bsz, GATE_COLS)

    def scan(d):
        h = carry_s[d]
        for s in range(ts):
            idx = (ts - 1 - s) if d == 1 else s
            h = a_s[d, idx] * h + b_s[d, idx]
            b_s[d, idx] = h
        carry_s[d] = h

    def store_batch_major(d):
        h_ref = dirs[d][3]
        if h_ref is None:
            return
        for p in range(ts // PERM_STEPS):
            steps = slice(p * PERM_STEPS, (p + 1) * PERM_STEPS)
            h_tm = b_s[d, steps].reshape(PERM_STEPS * bsz, LRU_WIDTH).astype(BF16)
            h_bm = jnp.dot(permt_ref[...], h_tm, preferred_element_type=F32)
            h_ref[:, steps, :] = h_bm.reshape(bsz, PERM_STEPS, LRU_WIDTH).astype(BF16)

    load_time_major(0)
    load_time_major(1)
    coefficients(0)
    scan(0)
    coefficients(1)
    store_batch_major(0)
    scan(1)
    store_batch_major(1)

    @pl.when(t == n_tiles - 1)
    def _():
        hlast_ref[...] = carry_s[...]


def _perm_matrices(bsz):
    n = bsz * PERM_STEPS
    r_out = np.arange(n)
    s, b = r_out // bsz, r_out % bsz
    perm = np.zeros((n, n), np.float32)
    perm[r_out, b * PERM_STEPS + s] = 1.0
    half = HALO // 2
    halo = np.zeros((CONV_WIDTH * bsz, 2 * bsz * half), np.float32)
    for b_ in range(bsz):
        for k in range(CONV_LEFT):
            halo[k * bsz + b_, b_ * half + half - CONV_LEFT + k] = 1.0
        halo[CONV_LEFT * bsz + b_, bsz * half + b_ * half] = 1.0
    return jnp.asarray(perm, BF16), jnp.asarray(perm.T, BF16), jnp.asarray(halo, BF16)


def _lru_call(u, h0, conv_w, conv_b, w_gate, b_gate, lam, emit_h):
    bsz, n_tok, _ = u.shape
    ts = SCAN_TILE
    n_tiles = n_tok // ts
    per = ts // HALO
    perm, perm_t, halo = _perm_matrices(bsz)

    def tile_specs(pos):
        return [pl.BlockSpec((bsz, ts, LRU_WIDTH), lambda t: (0, pos(t), 0)),
                pl.BlockSpec((bsz, HALO, LRU_WIDTH), lambda t: (0, jnp.maximum(pos(t) * per - 1, 0), 0)),
                pl.BlockSpec((bsz, HALO, LRU_WIDTH),
                             lambda t: (0, jnp.minimum((pos(t) + 1) * per, n_tok // HALO - 1), 0))]

    fwd = lambda t: t
    bwd = lambda t: n_tiles - 1 - t
    in_specs = tile_specs(fwd) + tile_specs(bwd) + [
        _resident((2, bsz, LRU_WIDTH)),
        _resident((CONV_WIDTH, LRU_WIDTH)),
        _resident((1, LRU_WIDTH)),
        _resident((2, len(GATE_K0), GATE_K, 2 * GATE_COLS)),
        _resident((2, 2, LRU_WIDTH)),
        _resident((2, 1, LRU_WIDTH)),
        _resident(perm.shape), _resident(perm_t.shape), _resident(halo.shape),
    ]
    out_specs = [pl.BlockSpec((bsz, ts, LRU_WIDTH), lambda t: (0, fwd(t), 0)),
                 pl.BlockSpec((bsz, ts, LRU_WIDTH), lambda t: (0, bwd(t), 0)),
                 pl.BlockSpec((2, bsz, LRU_WIDTH), lambda t: (0, 0, 0))]
    out_shape = [jax.ShapeDtypeStruct((bsz, n_tok, LRU_WIDTH), BF16),
                 jax.ShapeDtypeStruct((bsz, n_tok, LRU_WIDTH), BF16),
                 jax.ShapeDtypeStruct((2, bsz, LRU_WIDTH), F32)]
    if not emit_h:
        out_specs, out_shape = out_specs[2:], out_shape[2:]
    return pl.pallas_call(
        functools.partial(_lru_kernel, n_tiles=n_tiles, emit_h=emit_h),
        grid=(n_tiles,), in_specs=in_specs, out_specs=out_specs, out_shape=out_shape,
        scratch_shapes=[pltpu.VMEM((2, ts + CONV_WIDTH - 1, bsz, LRU_WIDTH), F32),
                        pltpu.VMEM((2, ts, bsz, LRU_WIDTH), F32),
                        pltpu.VMEM((2, ts, bsz, LRU_WIDTH), F32),
                        pltpu.VMEM((2, bsz, LRU_WIDTH), F32)],
        compiler_params=pltpu.CompilerParams(dimension_semantics=("arbitrary",),
                                             vmem_limit_bytes=VMEM_LIMIT),
        name="lru",
    )(u, u, u, u, u, u, h0, 0.5 * conv_w, 0.5 * conv_b.reshape(1, LRU_WIDTH), w_gate, b_gate,
      lam.reshape(2, 1, LRU_WIDTH), perm, perm_t, halo)


def _cast_specs(weights, n_steps, step_of):
    in_specs, out_specs, out_shape = [], [], []
    for w in weights:
        rows, cols = w.shape
        n_blocks = rows // CAST_ROWS
        assert rows % CAST_ROWS == 0 and n_blocks <= n_steps
        index_map = lambda *idx, n_blocks=n_blocks: (jnp.minimum(step_of(*idx), n_blocks - 1), 0)
        in_specs.append(pl.BlockSpec((CAST_ROWS, cols), index_map))
        out_specs.append(pl.BlockSpec((CAST_ROWS, cols), index_map))
        out_shape.append(jax.ShapeDtypeStruct((rows, cols), BF16))
    return in_specs, out_specs, out_shape


def _cast_blocks(w_refs, o_refs):
    for w_ref, o_ref in zip(w_refs, o_refs):
        o_ref[...] = w_ref[...].astype(BF16)


def _attn_kernel(q_ref, kc_ref, vc_ref, kl_ref, vl_ref, *rest):
    n_cast = (len(rest) - 6) // 2
    w_refs, o_ref, wo_refs = rest[:n_cast], rest[n_cast], rest[n_cast + 1:2 * n_cast + 1]
    sa_s, sb_s, acc_s, m_s, l_s = rest[2 * n_cast + 1:]
    _cast_blocks(w_refs, wo_refs)
    n_chunks = kl_ref.shape[1] // KEY_CHUNK

    def scores(k, g):
        qg = q_ref[0, :, g * HEAD_DIM:(g + 1) * HEAD_DIM]
        return lax.dot_general(k, qg, (((1,), (1,)), ((), ())), preferred_element_type=F32)

    def softmax_pv(st, vt, g, first):
        m_c = jnp.max(st, axis=0, keepdims=True)
        if first:
            m_new = m_c
        else:
            m_old = m_s[g]
            m_new = jnp.maximum(m_old, m_c)
            alpha = jnp.exp2(m_old - m_new)
        p = jnp.exp2(st - m_new)
        p_sum = jnp.sum(p, axis=0, keepdims=True)
        pv = jnp.dot(vt, p.astype(BF16), preferred_element_type=F32)
        if first:
            l_s[g] = p_sum
            acc_s[g] = pv
        else:
            l_s[g] = alpha * l_s[g] + p_sum
            acc_s[g] = alpha * acc_s[g] + pv
        m_s[g] = m_new

    def lat_keys(c):
        return kl_ref[0, pl.ds(pl.multiple_of(c * KEY_CHUNK, KEY_CHUNK), KEY_CHUNK), :]

    def lat_vals(c):
        return vl_ref[0, :, pl.ds(pl.multiple_of(c * KEY_CHUNK, KEY_CHUNK), KEY_CHUNK)]

    def stage(k_next, st_in, st_out, vt, first=False):
        for g in range(GROUP):
            if k_next is not None:
                st_out[g] = scores(k_next, g)
            softmax_pv(st_in[g], vt, g, first)

    k0 = kl_ref[0, 0:KEY_CHUNK, :]
    for g in range(GROUP):
        sc = scores(kc_ref[0], g)
        sa_s[g] = scores(k0, g)
        softmax_pv(sc, vc_ref[0], g, True)

    def body(i, carry):
        stage(lat_keys(2 * i + 1), sa_s, sb_s, lat_vals(2 * i))
        stage(lat_keys(2 * i + 2), sb_s, sa_s, lat_vals(2 * i + 1))
        return carry

    lax.fori_loop(0, n_chunks // 2 - 1, body, 0)
    stage(lat_keys(n_chunks - 1), sa_s, sb_s, lat_vals(n_chunks - 2))
    stage(None, sb_s, None, lat_vals(n_chunks - 1))
    for g in range(GROUP):
        o = (acc_s[g] / l_s[g]).T
        o_ref[0, :, g * HEAD_DIM:(g + 1) * HEAD_DIM] = o.astype(BF16)


def _attn_call(q, k_ctx, vt_ctx, k_lat, vt_lat, *weights):
    bsz, n_tok, _ = q.shape
    n_ctx = k_ctx.shape[1]
    tq = Q_TILE
    gw = GROUP * HEAD_DIM
    n_qt = n_tok // tq
    keys = lambda n: pl.BlockSpec((1, n, HEAD_DIM), lambda b, j, i: (b, 0, j))
    vals = lambda n: pl.BlockSpec((1, HEAD_DIM, n), lambda b, j, i: (b, j, 0))
    w_in, w_out, w_shape = _cast_specs(weights, bsz * N_KV_HEADS * n_qt,
                                       lambda b, j, i: (b * N_KV_HEADS + j) * n_qt + i)
    return pl.pallas_call(
        _attn_kernel,
        grid=(bsz, N_KV_HEADS, n_qt),
        in_specs=[pl.BlockSpec((1, tq, gw), lambda b, j, i: (b, i, j)),
                  keys(n_ctx), pl.BlockSpec((1, HEAD_DIM, n_ctx), lambda b, j, i: (0, j, b)),
                  keys(n_tok), vals(n_tok)] + w_in,
        out_specs=[pl.BlockSpec((1, tq, gw), lambda b, j, i: (b, i, j))] + w_out,
        out_shape=[jax.ShapeDtypeStruct((bsz, n_tok, ATTN_WIDTH), BF16)] + w_shape,
        scratch_shapes=[pltpu.VMEM((GROUP, KEY_CHUNK, tq), F32),
                        pltpu.VMEM((GROUP, KEY_CHUNK, tq), F32),
                        pltpu.VMEM((GROUP, HEAD_DIM, tq), F32),
                        pltpu.VMEM((GROUP, 1, tq), F32),
                        pltpu.VMEM((GROUP, 1, tq), F32)],
        compiler_params=pltpu.CompilerParams(
            dimension_semantics=("arbitrary", "arbitrary", "arbitrary"),
            vmem_limit_bytes=VMEM_LIMIT),
        name="attn",
    )(q, k_ctx, vt_ctx, k_lat, vt_lat, *weights)


def _attn_fixed_shift_kernel(shift_ref, q_ref, kc_ref, vc_ref, kl_ref, vl_ref, *rest):
    n_cast = (len(rest) - 1) // 2
    w_refs, o_ref, wo_refs = rest[:n_cast], rest[n_cast], rest[n_cast + 1:]
    _cast_blocks(w_refs, wo_refs)
    tq = Q_TILE
    shift = shift_ref[0]
    for t in range(q_ref.shape[1] // tq):
        rows = slice(t * tq, (t + 1) * tq)
        for j in range(N_KV_HEADS):
            base = j * GROUP * HEAD_DIM
            kv = slice(j * HEAD_DIM, (j + 1) * HEAD_DIM)
            q_all = jnp.concatenate([q_ref[0, rows, base + g * HEAD_DIM:base + (g + 1) * HEAD_DIM]
                                     for g in range(GROUP)], axis=0)

            def unnormalised(k, vt):
                st = lax.dot_general(k, q_all, (((1,), (1,)), ((), ())), preferred_element_type=F32)
                p = jnp.exp2(st - shift)
                pv = jnp.dot(vt, p.astype(BF16), preferred_element_type=F32)
                return jnp.sum(p, axis=0, keepdims=True), pv

            l_ctx, acc_ctx = unnormalised(kc_ref[0, :, kv], vc_ref[0, kv, :])
            l_lat, acc_lat = unnormalised(kl_ref[0, :, kv], vl_ref[0, kv, :])
            o_t = (acc_ctx + acc_lat) / (l_ctx + l_lat)
            for g in range(GROUP):
                o_ref[0, rows, base + g * HEAD_DIM:base + (g + 1) * HEAD_DIM] = (
                    o_t[:, g * tq:(g + 1) * tq].T.astype(BF16))


def _attn_fixed_shift_call(shift, q, k_ctx, vt_ctx, k_lat, vt_lat, *weights):
    bsz, n_tok, _ = q.shape
    n_ctx = k_ctx.shape[1]
    rows = FIXED_SHIFT_Q_TILES * Q_TILE
    n_qt = n_tok // rows
    keys = lambda n: pl.BlockSpec((1, n, KV_WIDTH), lambda b, i: (b, 0, 0))
    vals = lambda n: pl.BlockSpec((1, KV_WIDTH, n), lambda b, i: (b, 0, 0))
    w_in, w_out, w_shape = _cast_specs(weights, bsz * n_qt, lambda b, i: b * n_qt + i)
    return pl.pallas_call(
        _attn_fixed_shift_kernel,
        grid=(bsz, n_qt),
        in_specs=[pl.BlockSpec(memory_space=pltpu.SMEM),
                  pl.BlockSpec((1, rows, ATTN_WIDTH), lambda b, i: (b, i, 0)),
                  keys(n_ctx), pl.BlockSpec((1, KV_WIDTH, n_ctx), lambda b, i: (0, 0, b)),
                  keys(n_tok), vals(n_tok)] + w_in,
        out_specs=[pl.BlockSpec((1, rows, ATTN_WIDTH), lambda b, i: (b, i, 0))] + w_out,
        out_shape=[jax.ShapeDtypeStruct((bsz, n_tok, ATTN_WIDTH), BF16)] + w_shape,
        compiler_params=pltpu.CompilerParams(
            dimension_semantics=("arbitrary", "arbitrary"),
            vmem_limit_bytes=VMEM_LIMIT),
        name="attn_fixed_shift",
    )(shift, q, k_ctx, vt_ctx, k_lat, vt_lat, *weights)


def _attention(q, k_ctx, vt_ctx, k_lat, vt_lat, q_gain, k_gain, weights):
    bound = (SCORE_BOUND_SLACK * np.float32(np.sqrt(HEAD_DIM) * np.log2(np.e))
             * jnp.max(jnp.abs(q_gain)) * jnp.max(jnp.abs(k_gain)))
    operands = (q, k_ctx, vt_ctx, k_lat, vt_lat, *weights)
    return lax.cond(bound <= MAX_FIXED_SHIFT,
                    lambda ops: _attn_fixed_shift_call(bound.reshape(1), *ops),
                    lambda ops: _attn_call(*ops), operands)


def _mix_out_kernel(hf_ref, hb_ref, g_ref, at_ref, ga_ref, gb_ref, mod_ref,
                    wl_ref, wa_ref, wo_ref, o_ref):
    gate = mod_ref[0, :, 2 * D_MODEL:3 * D_MODEL]
    for t in range(o_ref.shape[1] // TOKEN_TILE):
        rows = slice(t * TOKEN_TILE, (t + 1) * TOKEN_TILE)
        yb = jnp.dot(at_ref[0, rows], wa_ref[...], preferred_element_type=F32)
        gated_b = gb_ref[0, rows].astype(F32) * yb
        lru = hf_ref[0, rows].astype(F32) + hb_ref[0, rows].astype(F32)
        za = (lru * _gelu_tanh(g_ref[0, rows].astype(F32))).astype(BF16)
        ya = jnp.dot(za, wl_ref[...], preferred_element_type=F32)
        mix = (ga_ref[0, rows].astype(F32) * ya + gated_b).astype(BF16)
        y = jnp.dot(mix, wo_ref[...], preferred_element_type=F32)
        o_ref[0, rows] = (gate * y).astype(BF16)


def _mix_out_call(h_fwd, h_bwd, gg, attn, sga, sgb, mod, w_lru_bf, w_attn_bf, w_out_bf):
    bsz, n_tok, _ = attn.shape
    tl = MIX_TILES * TOKEN_TILE
    tok = lambda width: pl.BlockSpec((1, tl, width), lambda b, l: (b, l, 0))
    return pl.pallas_call(
        _mix_out_kernel,
        grid=(bsz, n_tok // tl),
        in_specs=[tok(LRU_WIDTH), tok(LRU_WIDTH), tok(LRU_WIDTH), tok(ATTN_WIDTH), tok(D_MODEL),
                  tok(D_MODEL),
                  pl.BlockSpec((1, 1, mod.shape[-1]), lambda b, l: (b, 0, 0)),
                  _resident((LRU_WIDTH, D_MODEL)), _resident((ATTN_WIDTH, D_MODEL)),
                  _resident((D_MODEL, D_MODEL))],
        out_specs=tok(D_MODEL),
        out_shape=jax.ShapeDtypeStruct((bsz, n_tok, D_MODEL), BF16),
        compiler_params=pltpu.CompilerParams(dimension_semantics=("arbitrary", "arbitrary"),
                                             vmem_limit_bytes=VMEM_LIMIT),
        name="mix_out",
    )(h_fwd, h_bwd, gg, attn, sga, sgb, mod, w_lru_bf, w_attn_bf, w_out_bf)


def _ffn_kernel(x_ref, dy_ref, mod_ref, nrm_ref, wi_ref, wo_ref, o_ref):
    x = x_ref[0] + dy_ref[0].astype(F32)
    shift = mod_ref[0, :, 3 * D_MODEL:4 * D_MODEL]
    scale = mod_ref[0, :, 4 * D_MODEL:5 * D_MODEL]
    gate_out = mod_ref[0, :, 5 * D_MODEL:6 * D_MODEL]
    h = ((x * _rms_scale(x) * nrm_ref[...]) * (1.0 + scale) + shift).astype(BF16)
    acc = jnp.zeros(x.shape, F32)
    for lo, hi in zip(FFN_SPLITS[:-1], FFN_SPLITS[1:]):
        gate = jnp.dot(h, wi_ref[:, lo:hi], preferred_element_type=F32)
        up = jnp.dot(h, wi_ref[:, FFN_HIDDEN + lo:FFN_HIDDEN + hi], preferred_element_type=F32)
        act = (gate * _sigmoid(gate) * up).astype(BF16)
        acc = acc + jnp.dot(act, wo_ref[lo:hi, :], preferred_element_type=F32)
    o_ref[0] = x + gate_out * acc


def _ffn_call(x, dy, mod, norm_g, w_in_bf, w_out_bf):
    bsz, n_tok, _ = x.shape
    tl = FFN_TOKEN_TILE
    tok = pl.BlockSpec((1, tl, D_MODEL), lambda b, l: (b, l, 0))
    return pl.pallas_call(
        _ffn_kernel,
        grid=(bsz, n_tok // tl),
        in_specs=[tok, tok, pl.BlockSpec((1, 1, mod.shape[-1]), lambda b, l: (b, 0, 0)),
                  _resident((1, D_MODEL)),
                  _resident((D_MODEL, 2 * FFN_HIDDEN)), _resident((FFN_HIDDEN, D_MODEL))],
        out_specs=tok,
        out_shape=jax.ShapeDtypeStruct((bsz, n_tok, D_MODEL), F32),
        compiler_params=pltpu.CompilerParams(dimension_semantics=("arbitrary", "arbitrary"),
                                             vmem_limit_bytes=VMEM_LIMIT),
        name="ffn",
    )(x, dy, mod, norm_g.reshape(1, D_MODEL), w_in_bf, w_out_bf)


def _rope_tables(n_tok):
    pos = np.arange(n_tok)
    inv_freq = ROPE_THETA ** (-np.arange(0, ROPE_AXIS_DIM, 2, dtype=np.float64) / ROPE_AXIS_DIM)
    ang_r = (pos // GRID_W)[:, None] * inv_freq[None, :]
    ang_c = (pos % GRID_W)[:, None] * inv_freq[None, :]
    cr, sr, cc, sc = (f(a) for a in (ang_r, ang_c) for f in (np.cos, np.sin))
    zero = np.zeros_like(sr)
    cos = np.concatenate([cr, cr, cc, cc], axis=-1)
    sin_hi = np.concatenate([-sr, zero, -sc, zero], axis=-1)
    sin_lo = np.concatenate([zero, sr, zero, sc], axis=-1)
    return tuple(jnp.asarray(t, F32) for t in (cos, sin_hi, sin_lo))


def _gate_weights(wa, wx):
    def dense(w):
        rows = [jnp.pad(w[n], ((0, 0), (n * LRU_BLOCK_DIM, LRU_WIDTH - (n + 1) * LRU_BLOCK_DIM)))
                for n in range(LRU_BLOCKS)]
        return jnp.concatenate(rows, axis=0)

    da, dx = dense(wa), dense(wx)
    tiles = []
    for j, k0 in enumerate(GATE_K0):
        cols = slice(j * GATE_COLS, (j + 1) * GATE_COLS)
        tiles.append(jnp.concatenate([da[k0:k0 + GATE_K, cols], dx[k0:k0 + GATE_K, cols]], axis=1))
    return jnp.stack(tiles).astype(BF16)


def kernel(x, c, ctx, c_ctx, w_mod, b_mod, norm_mix, w_in, conv_w, conv_b, lru_wa, lru_ba, lru_wx,
           lru_bx, lru_lambda, q_norm, k_norm, w_out_lru, w_out_attn, w_out, norm_ffn, w_ffn_in,
           w_ffn_out):
    bsz, n_tok, _ = x.shape
    assert w_mod.shape[0] == 1, "single trunk layer"
    tables = _rope_tables(n_tok)

    c_all = jnp.concatenate([c, c_ctx[None, :]], axis=0)
    c_all = jnp.pad(c_all, ((0, -c_all.shape[0] % SUBLANES), (0, 0)))
    mod_all = _mod_call(c_all, w_mod[0], b_mod[0])
    mod = mod_all[:bsz].reshape(bsz, 1, -1)
    mod_c = mod_all[bsz:bsz + 1].reshape(1, 1, -1)

    col_scale = np.where(np.arange(IN_COLS) >= OFF_GA, 0.5, 1.0).astype(np.float32)
    w_in_bf = (w_in[0] * col_scale).astype(BF16)
    n_ctx = ctx.shape[1]
    u_c, k_c, vt_c = _in_proj_call(ctx.reshape(1, bsz * n_ctx, D_MODEL), mod_c, norm_mix[0], w_in_bf,
                                   None, k_norm[0], None, latent=False)
    u_c = u_c.reshape(bsz, n_ctx, LRU_WIDTH)
    k_c = k_c.reshape(bsz, n_ctx, KV_WIDTH)
    u_l, gg, q, k_l, vt_l, sga, sgb = _in_proj_call(x, mod, norm_mix[0], w_in_bf, q_norm[0], k_norm[0],
                                                    tables, latent=True)

    w_gate = jnp.stack([_gate_weights(lru_wa[0, d], lru_wx[0, d]) for d in range(2)])
    b_gate = 0.5 * jnp.stack([lru_ba[0], lru_bx[0]], axis=1)
    lru = functools.partial(_lru_call, conv_w=conv_w[0], conv_b=conv_b[0], w_gate=w_gate, b_gate=b_gate,
                            lam=lru_lambda[0])
    (h_seed,) = lru(u_c, jnp.zeros((2, bsz, LRU_WIDTH), F32), emit_h=False)
    h_fwd, h_bwd, _ = lru(u_l, h_seed, emit_h=True)

    later_weights = (w_out_lru[0], w_out_attn[0], w_out[0], w_ffn_in[0], w_ffn_out[0])
    attn, w_lru_bf, w_attn_bf, w_out_bf, w_ffn_in_bf, w_ffn_out_bf = _attention(
        q, k_c, vt_c, k_l, vt_l, q_norm[0], k_norm[0], later_weights)

    dy = _mix_out_call(h_fwd, h_bwd, gg, attn, sga, sgb, mod, w_lru_bf, w_attn_bf, w_out_bf)
    return _ffn_call(x, dy, mod, norm_ffn[0], w_ffn_in_bf, w_ffn_out_bf)
```
